```python
import jax
import jax.numpy as jnp
from jax import lax
import numpy as np

D_MODEL = 2048
BATCH = 2
SEQ = 8192
DEPTH = 1

ATT_HEADS = 16
ATT_HEAD_DIM = 128
Q_RANK = 512
KV_RANK = 256
IDX_HEADS = 16
IDX_DIM = 128
IDX_TOPK = 256
Q_BLOCK = 128
GDN_QK_HEADS = 16
GDN_V_HEADS = 32
GDN_HEAD_DIM = 128
GDN_CONV = 4
GDN_CHUNK = 64
N_EXPERTS = 32
TOP_K = 4
D_FF = 2048
SWIGLU_LIMIT = 7.0
SWIGLU_ALPHA = 1.702
MOE_BLOCK = 512
EPS = 1e-6

ATT_WIDTH = ATT_HEADS * ATT_HEAD_DIM
GDN_QK_WIDTH = GDN_QK_HEADS * GDN_HEAD_DIM
GDN_V_WIDTH = GDN_V_HEADS * GDN_HEAD_DIM
GDN_CONV_CH = 2 * GDN_QK_WIDTH + GDN_V_WIDTH
IN_SPLITS = (Q_RANK, KV_RANK, IDX_DIM, IDX_HEADS,
             GDN_QK_WIDTH, GDN_QK_WIDTH, GDN_V_WIDTH, GDN_V_HEADS, GDN_V_HEADS, GDN_V_WIDTH,
             D_MODEL, D_MODEL)
D_IN = sum(IN_SPLITS)

kernel_name = 'hybrid_dsa_gdn_moe_adaln_block'


def rms_norm(x, w):
    xf = x.astype(jnp.float32)
    y = xf * lax.rsqrt(jnp.mean(xf * xf, axis=-1, keepdims=True) + EPS)
    return (y * w.astype(jnp.float32)).astype(x.dtype)


def l2_normalize(x):
    xf = x.astype(jnp.float32)
    return xf * lax.rsqrt(jnp.sum(xf * xf, axis=-1, keepdims=True) + EPS)


def split_cols(t, widths):
    outs = []
    off = 0
    for wdt in widths:
        outs.append(t[..., off:off + wdt])
        off += wdt
    return outs


def dsa_mixer(q_lat, kv_lat, k_idx, w_idx, q_lat_norm_w, kv_lat_norm_w, idx_k_norm_w, w_uq, w_uqi, w_uk, w_uv):
    bsz, s, _ = q_lat.shape
    topk = min(IDX_TOPK, s // 4)
    cq = rms_norm(q_lat, q_lat_norm_w)
    ckv = rms_norm(kv_lat, kv_lat_norm_w)
    q = jnp.einsum('bsr,rhd->bshd', cq, w_uq)
    q_abs = jnp.einsum('bshd,rhd->bshr', q, w_uk) * (ATT_HEAD_DIM ** -0.5)
    qi = jnp.einsum('bsr,rhd->bshd', cq, w_uqi) * (IDX_DIM ** -0.5)
    ki = rms_norm(k_idx, idx_k_norm_w)
    wi = w_idx * (IDX_HEADS ** -0.5)
    key_pos = jnp.arange(s)

    def block(start):
        qi_b = lax.dynamic_slice_in_dim(qi, start, Q_BLOCK, axis=1)
        wi_b = lax.dynamic_slice_in_dim(wi, start, Q_BLOCK, axis=1)
        qa_b = lax.dynamic_slice_in_dim(q_abs, start, Q_BLOCK, axis=1)
        t = start + jnp.arange(Q_BLOCK)
        rel = jax.nn.relu(jnp.einsum('bqhd,bsd->bqhs', qi_b, ki)).astype(jnp.float32)
        score = jnp.einsum('bqhs,bqh->bqs', rel, wi_b.astype(jnp.float32))
        causal = key_pos[None, :] <= t[:, None]
        score = jnp.where(causal[None], score, -jnp.inf)
        _, idx = lax.top_k(score, topk)
        valid = idx <= t[None, :, None]
        c_sel = jax.vmap(lambda cb, ib: cb[ib])(ckv, idx)
        logits = jnp.einsum('bqhr,bqkr->bqhk', qa_b, c_sel).astype(jnp.float32)
        logits = jnp.where(valid[:, :, None, :], logits, -jnp.inf)
        p = jax.nn.softmax(logits, axis=-1).astype(c_sel.dtype)
        return jnp.einsum('bqhk,bqkr->bqhr', p, c_sel)

    starts = jnp.arange(s // Q_BLOCK) * Q_BLOCK
    o_lat = lax.map(block, starts)
    o_lat = jnp.moveaxis(o_lat, 0, 1).reshape(bsz, s, ATT_HEADS, KV_RANK)
    o = jnp.einsum('bshr,rhd->bshd', o_lat, w_uv)
    return o.reshape(bsz, s, ATT_WIDTH)


def causal_conv_silu(u, w):
    y = lax.conv_general_dilated(u, w[:, None, :].astype(u.dtype), window_strides=(1,),
                                 padding=[(GDN_CONV - 1, 0)],
                                 dimension_numbers=('NWC', 'WIO', 'NWC'),
                                 feature_group_count=u.shape[-1])
    return jax.nn.silu(y)


def chunk_gated_delta_rule(q, k, v, g, beta):
    bsz, s, nh, dk = q.shape
    dv = v.shape[-1]
    n = s // GDN_CHUNK

    def chunks(t):
        return jnp.moveaxis(t.reshape(bsz, n, GDN_CHUNK, nh, -1), 3, 1)

    q, k, v = chunks(q), chunks(k), chunks(v)
    beta = chunks(beta[..., None])[..., 0]
    g = jnp.cumsum(chunks(g[..., None])[..., 0], axis=-1)
    k_beta = k * beta[..., None]
    v_beta = v * beta[..., None]
    pos = jnp.arange(GDN_CHUNK)
    incl = pos[:, None] >= pos[None, :]
    strict = pos[:, None] > pos[None, :]
    decay = jnp.exp(jnp.where(incl, g[..., :, None] - g[..., None, :], -jnp.inf))
    a_mat = jnp.where(strict, jnp.einsum('bhncd,bhnmd->bhncm', k_beta, k) * decay, 0.0)
    eye = jnp.eye(GDN_CHUNK, dtype=jnp.float32)
    t_mat = lax.linalg.triangular_solve(a_mat + eye, jnp.broadcast_to(eye, a_mat.shape),
                                        left_side=True, lower=True, unit_diagonal=True)
    u = t_mat @ v_beta
    w = t_mat @ (k_beta * jnp.exp(g)[..., None])
    qk = jnp.where(incl, jnp.einsum('bhncd,bhnmd->bhncm', q, k) * decay, 0.0)
    g_last = g[..., -1]
    k_tail = k * jnp.exp(g_last[..., None] - g)[..., None]
    q_dec = q * jnp.exp(g)[..., None]

    def step(state, xs):
        u_n, w_n, qk_n, kt_n, qd_n, gl_n = xs
        v_new = u_n - w_n @ state
        o_n = qd_n @ state + qk_n @ v_new
        state = state * jnp.exp(gl_n)[..., None, None] + jnp.einsum('bhcd,bhce->bhde', kt_n, v_new)
        return state, o_n

    xs = (jnp.moveaxis(u, 2, 0), jnp.moveaxis(w, 2, 0), jnp.moveaxis(qk, 2, 0),
          jnp.moveaxis(k_tail, 2, 0), jnp.moveaxis(q_dec, 2, 0), jnp.moveaxis(g_last, 2, 0))
    state0 = jnp.zeros((bsz, nh, dk, dv), jnp.float32)
    _, o = lax.scan(step, state0, xs)
    return jnp.transpose(o, (1, 0, 3, 2, 4)).reshape(bsz, s, nh, dv)


def gated_deltanet(q, k, v, a_raw, b_raw, z, conv_w, a_log, dt_bias, norm_w):
    bsz, s, _ = q.shape
    qkv = causal_conv_silu(jnp.concatenate([q, k, v], axis=-1), conv_w)
    q, k, v = split_cols(qkv, (GDN_QK_WIDTH, GDN_QK_WIDTH, GDN_V_WIDTH))
    rep = GDN_V_HEADS // GDN_QK_HEADS
    q = jnp.repeat(l2_normalize(q.reshape(bsz, s, GDN_QK_HEADS, GDN_HEAD_DIM)) * (GDN_HEAD_DIM ** -0.5), rep, axis=2)
    k = jnp.repeat(l2_normalize(k.reshape(bsz, s, GDN_QK_HEADS, GDN_HEAD_DIM)), rep, axis=2)
    v = v.reshape(bsz, s, GDN_V_HEADS, GDN_HEAD_DIM).astype(jnp.float32)
    beta = jax.nn.sigmoid(b_raw.astype(jnp.float32))
    g = -jnp.exp(a_log.astype(jnp.float32)) * jax.nn.softplus(a_raw.astype(jnp.float32) + dt_bias.astype(jnp.float32))
    o = chunk_gated_delta_rule(q, k, v, g, beta)
    o = rms_norm(o, norm_w) * jax.nn.silu(z.reshape(bsz, s, GDN_V_HEADS, GDN_HEAD_DIM).astype(jnp.float32))
    return o.reshape(bsz, s, GDN_V_WIDTH).astype(z.dtype)


def moe_ffn(h, w_router, b_router, w_gu, b_gu, w_down, b_down):
    n_tok, d = h.shape
    logits = (h @ w_router + b_router).astype(jnp.float32)
    top_logit, top_e = lax.top_k(logits, TOP_K)
    gate = jax.nn.softmax(top_logit, axis=-1)
    n_asg = n_tok * TOP_K
    flat_e = top_e.reshape(-1)
    flat_gate = gate.reshape(-1)
    order = jnp.argsort(flat_e)
    sorted_e = flat_e[order]
    sorted_tok = (order // TOP_K).astype(jnp.int32)
    counts = jnp.bincount(flat_e, length=N_EXPERTS)
    start = jnp.cumsum(counts) - counts
    padded = (counts + MOE_BLOCK - 1) // MOE_BLOCK * MOE_BLOCK
    pad_end = jnp.cumsum(padded)
    pad_start = pad_end - padded
    dest = pad_start[sorted_e] + jnp.arange(n_asg) - start[sorted_e]
    n_blocks = -(-n_asg // MOE_BLOCK) + N_EXPERTS
    n_rows = n_blocks * MOE_BLOCK
    row_tok = jnp.zeros((n_rows,), jnp.int32).at[dest].set(sorted_tok)
    row_gate = jnp.zeros((n_rows,), jnp.float32).at[dest].set(flat_gate[order])
    block_e = jnp.minimum(jnp.searchsorted(pad_end, jnp.arange(n_blocks) * MOE_BLOCK, side='right'), N_EXPERTS - 1)
    x_rows = h[row_tok].reshape(n_blocks, MOE_BLOCK, d)

    def expert_block(args):
        xb, e = args
        gu = xb @ w_gu[e] + b_gu[e]
        g_ = jnp.minimum(gu[:, :D_FF], SWIGLU_LIMIT)
        u_ = jnp.clip(gu[:, D_FF:], -SWIGLU_LIMIT, SWIGLU_LIMIT)
        act = (u_ + 1.0) * (g_ * jax.nn.sigmoid(SWIGLU_ALPHA * g_))
        return act @ w_down[e] + b_down[e]

    y_rows = lax.map(expert_block, (x_rows, block_e)).reshape(n_rows, d)
    y_rows = y_rows * row_gate[:, None].astype(y_rows.dtype)
    return jax.ops.segment_sum(y_rows, row_tok, num_segments=n_tok)


def setup_inputs(seed: int = 0) -> dict:
    key = jax.random.key(seed)
    ks = jax.random.split(key, 32)
    f32 = jnp.float32
    L = DEPTH

    def nrm(k, shape, scale):
        return jax.random.normal(k, shape, f32) * scale

    def gain(k, shape):
        return 1.0 + 0.02 * jax.random.normal(k, shape, f32)

    dt = jax.random.uniform(ks[15], (L, GDN_V_HEADS), f32, 0.001, 0.1)
    return {
        'x': nrm(ks[0], (BATCH, SEQ, D_MODEL), 1.0),
        'c': nrm(ks[1], (BATCH, D_MODEL), 1.0),
        'ada_w': nrm(ks[2], (L, D_MODEL, 6 * D_MODEL), 0.5 * D_MODEL ** -0.5),
        'ada_b': nrm(ks[3], (L, 6 * D_MODEL), 0.02),
        'norm1_w': gain(ks[4], (L, D_MODEL)),
        'w_in': nrm(ks[5], (L, D_MODEL, D_IN), D_MODEL ** -0.5),
        'q_lat_norm_w': gain(ks[6], (L, Q_RANK)),
        'kv_lat_norm_w': gain(ks[7], (L, KV_RANK)),
        'idx_k_norm_w': gain(ks[8], (L, IDX_DIM)),
        'w_uq': nrm(ks[9], (L, Q_RANK, ATT_HEADS, ATT_HEAD_DIM), Q_RANK ** -0.5),
        'w_uqi': nrm(ks[10], (L, Q_RANK, IDX_HEADS, IDX_DIM), Q_RANK ** -0.5),
        'w_uk': nrm(ks[11], (L, KV_RANK, ATT_HEADS, ATT_HEAD_DIM), KV_RANK ** -0.5),
        'w_uv': nrm(ks[12], (L, KV_RANK, ATT_HEADS, ATT_HEAD_DIM), KV_RANK ** -0.5),
        'conv_w': nrm(ks[13], (L, GDN_CONV, GDN_CONV_CH), GDN_CONV ** -0.5),
        'a_log': jnp.log(jax.random.uniform(ks[14], (L, GDN_V_HEADS), f32, 1.0, 16.0)),
        'dt_bias': jnp.log(jnp.expm1(dt)),
        'gdn_norm_w': gain(ks[16], (L, GDN_HEAD_DIM)),
        'w_branch_a': nrm(ks[17], (L, ATT_WIDTH, D_MODEL), ATT_WIDTH ** -0.5),
        'w_branch_b': nrm(ks[18], (L, GDN_V_WIDTH, D_MODEL), GDN_V_WIDTH ** -0.5),
        'w_out': nrm(ks[19], (L, D_MODEL, D_MODEL), D_MODEL ** -0.5),
        'norm2_w': gain(ks[20], (L, D_MODEL)),
        'w_router': nrm(ks[21], (L, D_MODEL, N_EXPERTS), D_MODEL ** -0.5),
        'b_router': nrm(ks[22], (L, N_EXPERTS), 0.01),
        'w_gu': nrm(ks[23], (L, N_EXPERTS, D_MODEL, 2 * D_FF), D_MODEL ** -0.5),
        'b_gu': nrm(ks[24], (L, N_EXPERTS, 2 * D_FF), 0.01),
        'w_down': nrm(ks[25], (L, N_EXPERTS, D_FF, D_MODEL), D_FF ** -0.5),
        'b_down': nrm(ks[26], (L, N_EXPERTS, D_MODEL), 0.01),
        'final_norm_w': gain(ks[27], (D_MODEL,)),
    }


def reference(x, c, ada_w, ada_b, norm1_w, w_in, q_lat_norm_w, kv_lat_norm_w, idx_k_norm_w,
              w_uq, w_uqi, w_uk, w_uv, conv_w, a_log, dt_bias, gdn_norm_w,
              w_branch_a, w_branch_b, w_out, norm2_w, w_router, b_router,
              w_gu, b_gu, w_down, b_down, final_norm_w):
    bsz, s, d = x.shape
    for l in range(DEPTH):
        mod = jnp.einsum('bd,de->be', jax.nn.silu(c), ada_w[l]) + ada_b[l]
        shift1, scale1, gate1, shift2, scale2, gate2 = jnp.split(mod, 6, axis=-1)
        h = rms_norm(x, norm1_w[l]) * (1.0 + scale1[:, None, :]) + shift1[:, None, :]
        proj = h @ w_in[l]
        (q_lat, kv_lat, k_idx, w_idx, g_q, g_k, g_v, g_a, g_b, g_z,
         gate_a, gate_b) = split_cols(proj, IN_SPLITS)
        y_a = dsa_mixer(q_lat, kv_lat, k_idx, w_idx, q_lat_norm_w[l], kv_lat_norm_w[l], idx_k_norm_w[l],
                        w_uq[l], w_uqi[l], w_uk[l], w_uv[l]) @ w_branch_a[l]
        y_b = gated_deltanet(g_q, g_k, g_v, g_a, g_b, g_z, conv_w[l], a_log[l], dt_bias[l],
                             gdn_norm_w[l]) @ w_branch_b[l]
        merged = jax.nn.sigmoid(gate_a) * y_a + jax.nn.sigmoid(gate_b) * y_b
        x = x + gate1[:, None, :] * (merged @ w_out[l])
        h2 = rms_norm(x, norm2_w[l]) * (1.0 + scale2[:, None, :]) + shift2[:, None, :]
        y = moe_ffn(h2.reshape(bsz * s, d), w_router[l], b_router[l], w_gu[l], b_gu[l],
                    w_down[l], b_down[l]).reshape(bsz, s, d)
        x = x + gate2[:, None, :] * y
    return rms_norm(x, final_norm_w)
```

```python
import functools

import jax
import jax.numpy as jnp
from jax import lax
from jax.experimental import pallas as pl
from jax.experimental.pallas import tpu as pltpu

F32 = jnp.float32
BF16 = jnp.bfloat16
I32 = jnp.int32

EPS = 1e-6
LANES = 128
VMEM_LIMIT = 56 * 1024 * 1024

IDX_TOPK = 256
GDN_CHUNK = 64
TOP_K = 4
SWIGLU_LIMIT = 7.0
SWIGLU_ALPHA = 1.702
MOE_BM = 512
INT_MIN = -(2 ** 31)
INT_MAX = 2 ** 31 - 1


def _cp(*sem):
    return pltpu.CompilerParams(dimension_semantics=sem, vmem_limit_bytes=VMEM_LIMIT)


def _tile(n, pref):
    if n <= pref:
        return n
    t = pref - pref % LANES
    while n % t:
        t -= LANES
    return t


def _rms(x, w):
    return x * lax.rsqrt(jnp.mean(x * x, axis=-1, keepdims=True) + EPS) * w


def _dot(a, b):
    return jnp.dot(a, b, preferred_element_type=F32)


def _dot_nt(a, b):
    return lax.dot_general(a, b, (((1,), (1,)), ((), ())), preferred_element_type=F32)


def _dot_tn(a, b):
    return lax.dot_general(a, b, (((0,), (0,)), ((), ())), preferred_element_type=F32)


def _ada_body(ct_ref, w_ref, b_ref, o_ref, *, nb):
    ct = ct_ref[...]
    s = ct * jax.nn.sigmoid(ct)
    w = w_ref[...]
    for b in range(nb):
        o_ref[b:b + 1, :] = jnp.sum(w * s[:, b:b + 1], axis=0, keepdims=True) + b_ref[...]


def _ada(c, w, b):
    nb, d = c.shape
    n = w.shape[1]
    tn = _tile(n, 1024)
    return pl.pallas_call(
        functools.partial(_ada_body, nb=nb),
        grid=(n // tn,),
        in_specs=[pl.BlockSpec((d, nb), lambda j: (0, 0)),
                  pl.BlockSpec((d, tn), lambda j: (0, j)),
                  pl.BlockSpec((1, tn), lambda j: (0, j))],
        out_specs=pl.BlockSpec((nb, tn), lambda j: (0, j)),
        out_shape=jax.ShapeDtypeStruct((nb, n), F32),
        compiler_params=_cp("parallel"),
        name="ada",
    )(c.T, w, b.reshape(1, n))


def _inproj_body(x_ref, nw_ref, sc_ref, sh_ref, w_ref, o_ref, h_ref):
    @pl.when(pl.program_id(1) == 0)
    def _():
        h = _rms(x_ref[...], nw_ref[...]) * (1.0 + sc_ref[...]) + sh_ref[...]
        h_ref[...] = h.astype(BF16)

    o_ref[...] = _dot(h_ref[...], w_ref[...])


def _inproj(x2, nw, scale, shift, w16, seq):
    t, d = x2.shape
    n = w16.shape[1]
    tm = min(seq, 1024)
    tn = _tile(n, 512)
    per_b = seq // tm
    return pl.pallas_call(
        _inproj_body,
        grid=(t // tm, n // tn),
        in_specs=[pl.BlockSpec((tm, d), lambda i, j: (i, 0)),
                  pl.BlockSpec((1, d), lambda i, j: (0, 0)),
                  pl.BlockSpec((None, 1, d), lambda i, j: (i // per_b, 0, 0)),
                  pl.BlockSpec((None, 1, d), lambda i, j: (i // per_b, 0, 0)),
                  pl.BlockSpec((d, tn), lambda i, j: (0, j))],
        out_specs=pl.BlockSpec((tm, tn), lambda i, j: (i, j)),
        out_shape=jax.ShapeDtypeStruct((t, n), F32),
        scratch_shapes=[pltpu.VMEM((tm, d), BF16)],
        compiler_params=_cp("parallel", "arbitrary"),
        name="inproj",
    )(x2, nw.reshape(1, d), scale[:, None, :], shift[:, None, :], w16)


def _dsa_prep_body(lat_ref, qnw_ref, kvnw_ref, knw_ref, wuq_ref, wuqi_ref, wuk_ref,
                   qabs_ref, qi_ref, ckv_ref, ckvt_ref, ki_ref, wi_ref,
                   *, qr, kvr, di, nh, dh, nhi, q_scale, i_scale, w_scale, wi_off):
    lat = lat_ref[...]
    cq_t = _rms(lat[:, :qr], qnw_ref[...]).T.astype(BF16)
    q_t = _dot(wuq_ref[...], cq_t)
    for h in range(nh):
        q_h = q_t[h * dh:(h + 1) * dh].astype(BF16)
        qabs_ref[h] = (_dot(wuk_ref[h], q_h) * q_scale).astype(BF16)
    qi_t = _dot(wuqi_ref[...], cq_t) * i_scale
    for h in range(nhi):
        qi_ref[h] = qi_t[h * di:(h + 1) * di].astype(BF16)
    ckv = _rms(lat[:, qr:qr + kvr], kvnw_ref[...])
    ckv_ref[...] = ckv.astype(BF16)
    ckvt_ref[...] = ckv.T.astype(BF16)
    ki_ref[...] = _rms(lat[:, qr + kvr:qr + kvr + di], knw_ref[...]).astype(BF16)
    misc_t = lat[:, qr + kvr + di:].T
    wi_ref[...] = misc_t[wi_off:wi_off + nhi] * w_scale


def _dsa_prep(proj, bsz, seq, qnw, kvnw, knw, wuq_t, wuqi_t, wuk_r, dims):
    qr, kvr, di, nh, dh, nhi, wi_off = dims
    tm = min(seq, 512)
    ns = seq // tm
    lat_w = qr + kvr + di + LANES
    body = functools.partial(
        _dsa_prep_body, qr=qr, kvr=kvr, di=di, nh=nh, dh=dh, nhi=nhi,
        q_scale=dh ** -0.5, i_scale=di ** -0.5, w_scale=nhi ** -0.5, wi_off=wi_off)
    full = lambda shape: pl.BlockSpec(shape, lambda b, s: (0,) * len(shape))
    return pl.pallas_call(
        body,
        grid=(bsz, ns),
        in_specs=[pl.BlockSpec((tm, lat_w), lambda b, s: (b * ns + s, 0)),
                  full((1, qr)), full((1, kvr)), full((1, di)),
                  full((nh * dh, qr)), full((nhi * di, qr)), full((nh, kvr, dh))],
        out_specs=[pl.BlockSpec((None, nh, kvr, tm), lambda b, s: (b, 0, 0, s)),
                   pl.BlockSpec((None, nhi, di, tm), lambda b, s: (b, 0, 0, s)),
                   pl.BlockSpec((None, tm, kvr), lambda b, s: (b, s, 0)),
                   pl.BlockSpec((None, kvr, tm), lambda b, s: (b, 0, s)),
                   pl.BlockSpec((None, tm, di), lambda b, s: (b, s, 0)),
                   pl.BlockSpec((None, nhi, tm), lambda b, s: (b, 0, s))],
        out_shape=[jax.ShapeDtypeStruct((bsz, nh, kvr, seq), BF16),
                   jax.ShapeDtypeStruct((bsz, nhi, di, seq), BF16),
                   jax.ShapeDtypeStruct((bsz, seq, kvr), BF16),
                   jax.ShapeDtypeStruct((bsz, kvr, seq), BF16),
                   jax.ShapeDtypeStruct((bsz, seq, di), BF16),
                   jax.ShapeDtypeStruct((bsz, nhi, seq), F32)],
        compiler_params=_cp("parallel", "parallel"),
        name="dsa_prep",
    )(proj, qnw.reshape(1, qr), kvnw.reshape(1, kvr), knw.reshape(1, di), wuq_t, wuqi_t, wuk_r)


def _dsa_attn_body(ki_ref, ckv_ref, ckvt_ref, qi_ref, qabs_ref, wi_ref, wuv_ref, o_ref,
                   keys_ref, acc_ref, m_ref, l_ref, ot_ref, thr_ref, jlim_ref,
                   *, tq, tk, nh, nhi, dh, topk, seq_bits):
    qb = pl.program_id(1)
    nk = ((qb + 1) * tq + tk - 1) // tk
    q_pos = qb * tq + lax.broadcasted_iota(I32, (tk, tq), 1)
    k_iota = lax.broadcasted_iota(I32, (tk, tq), 0)

    def score_tile(kt, _):
        ki_t = ki_ref[pl.ds(pl.multiple_of(kt * tk, tk), tk), :]
        acc = jnp.zeros((tk, tq), F32)
        for h in range(nhi):
            rel = _dot(ki_t, qi_ref[h])
            acc = acc + wi_ref[h:h + 1, :] * jnp.maximum(rel, 0.0)
        bits = lax.bitcast_convert_type(acc, I32)
        key = jnp.where(bits < 0, bits ^ INT_MAX, bits)
        key = jnp.where(kt * tk + k_iota <= q_pos, key, INT_MIN)
        keys_ref[pl.ds(pl.multiple_of(kt * tk, tk), tk), :] = key
        return 0

    lax.fori_loop(0, nk, score_tile, 0)

    def count(pred_fn):
        def tile(kt, c):
            k_t = keys_ref[pl.ds(pl.multiple_of(kt * tk, tk), tk), :]
            hit = jnp.where(pred_fn(k_t, kt), 1, 0).astype(I32)
            return c + jnp.sum(hit.reshape(tk // 8, 8, tq), axis=0)
        c8 = lax.fori_loop(0, nk, tile, jnp.zeros((8, tq), I32))
        return jnp.sum(c8, axis=0, keepdims=True)

    def bit_step(it, thr):
        cand = thr ^ lax.shift_left(jnp.int32(1), 31 - it)
        cnt = count(lambda k_t, kt: k_t >= cand)
        return jnp.where(cnt >= topk, cand, thr)

    thr = lax.fori_loop(0, 32, bit_step, jnp.full((1, tq), INT_MIN, I32))
    thr = jnp.maximum(thr, INT_MIN + 1)
    cnt_gt = count(lambda k_t, kt: k_t > thr)
    cnt_ge = count(lambda k_t, kt: k_t >= thr)
    thr_ref[...] = thr
    jlim_ref[...] = jnp.full((1, tq), INT_MAX, I32)

    @pl.when(jnp.max(cnt_ge) > topk)
    def _():
        need = topk - cnt_gt

        def pos_step(it, p):
            cand = p + lax.shift_left(jnp.int32(1), seq_bits - 1 - it)
            cnt = count(lambda k_t, kt: (k_t == thr) & (kt * tk + k_iota < cand))
            return jnp.where(cnt < need, cand, p)

        jlim_ref[...] = lax.fori_loop(0, seq_bits, pos_step, jnp.zeros((1, tq), I32))

    m_ref[...] = jnp.full(m_ref.shape, -jnp.inf, F32)
    l_ref[...] = jnp.zeros(l_ref.shape, F32)
    acc_ref[...] = jnp.zeros(acc_ref.shape, F32)

    def attn_tile(kt, _):
        off = pl.multiple_of(kt * tk, tk)
        k_t = keys_ref[pl.ds(off, tk), :]
        thr_b = thr_ref[...]
        mask = (k_t > thr_b) | ((k_t == thr_b) & (kt * tk + k_iota <= jlim_ref[...]))
        ckv_t = ckv_ref[pl.ds(off, tk), :]
        ckvt_t = ckvt_ref[:, pl.ds(off, tk)]
        for h in range(nh):
            s = jnp.where(mask, _dot(ckv_t, qabs_ref[h]), -jnp.inf)
            m_old = m_ref[h]
            m_new = jnp.maximum(m_old, jnp.max(s, axis=0, keepdims=True))
            m_use = jnp.where(m_new == -jnp.inf, 0.0, m_new)
            p = jnp.exp(s - m_use)
            alpha = jnp.exp(m_old - m_use)
            l_ref[h] = alpha * l_ref[h] + jnp.sum(p, axis=0, keepdims=True)
            acc_ref[h] = alpha * acc_ref[h] + _dot(ckvt_t, p.astype(BF16))
            m_ref[h] = m_new
        return 0

    lax.fori_loop(0, nk, attn_tile, 0)

    for h in range(nh):
        o_lat = (acc_ref[h] / l_ref[h]).astype(BF16)
        ot_ref[h * dh:(h + 1) * dh, :] = _dot(wuv_ref[h], o_lat)
    o_ref[...] = ot_ref[...].T.astype(BF16)


def _dsa_attn(ki, ckv, ckvt, qi_t, qabs_t, wi_t, wuv_t):
    bsz, seq, di = ki.shape
    kvr = ckv.shape[2]
    nh, dh = wuv_t.shape[0], wuv_t.shape[1]
    nhi = qi_t.shape[1]
    tq = min(seq, 256)
    tk = tq
    topk = min(IDX_TOPK, seq // 4)
    body = functools.partial(_dsa_attn_body, tq=tq, tk=tk, nh=nh, nhi=nhi, dh=dh, topk=topk,
                             seq_bits=max(1, (seq - 1).bit_length()))
    return pl.pallas_call(
        body,
        grid=(bsz, seq // tq),
        in_specs=[pl.BlockSpec((None, seq, di), lambda b, q: (b, 0, 0)),
                  pl.BlockSpec((None, seq, kvr), lambda b, q: (b, 0, 0)),
                  pl.BlockSpec((None, kvr, seq), lambda b, q: (b, 0, 0)),
                  pl.BlockSpec((None, nhi, di, tq), lambda b, q: (b, 0, 0, q)),
                  pl.BlockSpec((None, nh, kvr, tq), lambda b, q: (b, 0, 0, q)),
                  pl.BlockSpec((None, nhi, tq), lambda b, q: (b, 0, q)),
                  pl.BlockSpec((nh, dh, kvr), lambda b, q: (0, 0, 0))],
        out_specs=pl.BlockSpec((None, tq, nh * dh), lambda b, q: (b, q, 0)),
        out_shape=jax.ShapeDtypeStruct((bsz, seq, nh * dh), BF16),
        scratch_shapes=[pltpu.VMEM((seq, tq), I32),
                        pltpu.VMEM((nh, kvr, tq), F32),
                        pltpu.VMEM((nh, 1, tq), F32),
                        pltpu.VMEM((nh, 1, tq), F32),
                        pltpu.VMEM((nh * dh, tq), F32),
                        pltpu.VMEM((1, tq), I32),
                        pltpu.VMEM((1, tq), I32)],
        compiler_params=_cp("parallel", "parallel"),
        name="dsa_attn",
    )(ki, ckv, ckvt, qi_t, qabs_t, wi_t, wuv_t)


def _gdn_conv_body(u_ref, w_ref, o_ref, carry_ref, *, ts, cb, hd, nq_blocks, q_scale):
    cblk = pl.program_id(1)

    @pl.when(pl.program_id(2) == 0)
    def _():
        carry_ref[...] = jnp.zeros(carry_ref.shape, F32)

    u = u_ref[...]
    w = w_ref[...]
    taps = w.shape[0]
    prev = carry_ref[...]
    u8 = u[0:8, :]
    row8 = lax.broadcasted_iota(I32, (8, cb), 0)
    y = u * w[taps - 1:taps, :]
    y8 = u8 * w[taps - 1:taps, :]
    for j in range(1, taps):
        wj = w[taps - 1 - j:taps - j, :]
        y = y + pltpu.roll(u, j, axis=0) * wj
        y8 = y8 + jnp.where(row8 < j, pltpu.roll(prev, j, axis=0), pltpu.roll(u8, j, axis=0)) * wj
    carry_ref[...] = u[ts - 8:, :]

    def finish(yy, rows, mode):
        yy = yy * jax.nn.sigmoid(yy)
        if mode is None:
            o_ref[rows, :] = yy
            return
        for hh in range(cb // hd):
            x = yy[:, hh * hd:(hh + 1) * hd]
            inv = lax.rsqrt(jnp.sum(x * x, axis=-1, keepdims=True) + EPS)
            o_ref[rows, hh * hd:(hh + 1) * hd] = x * (inv * mode)

    def emit(mode):
        finish(y, slice(0, ts), mode)
        finish(y8, slice(0, 8), mode)

    @pl.when(cblk < nq_blocks)
    def _():
        emit(q_scale)

    @pl.when((cblk >= nq_blocks) & (cblk < 2 * nq_blocks))
    def _():
        emit(1.0)

    @pl.when(cblk >= 2 * nq_blocks)
    def _():
        emit(None)


def _gdn_conv(proj, conv_w, bsz, seq, col0, qk_w, v_w, hd):
    ts = min(seq, 512)
    cb = min(qk_w, 1024)
    ns = seq // ts
    ncb = (2 * qk_w + v_w) // cb
    c0 = col0 // cb
    body = functools.partial(_gdn_conv_body, ts=ts, cb=cb, hd=hd, nq_blocks=qk_w // cb, q_scale=hd ** -0.5)
    return pl.pallas_call(
        body,
        grid=(bsz, ncb, ns),
        in_specs=[pl.BlockSpec((ts, cb), lambda b, c, s: (b * ns + s, c0 + c)),
                  pl.BlockSpec((conv_w.shape[0], cb), lambda b, c, s: (0, c))],
        out_specs=pl.BlockSpec((ts, cb), lambda b, c, s: (b * ns + s, c)),
        out_shape=jax.ShapeDtypeStruct((bsz * seq, 2 * qk_w + v_w), F32),
        scratch_shapes=[pltpu.VMEM((8, cb), F32)],
        compiler_params=_cp("parallel", "parallel", "arbitrary"),
        name="gdn_conv",
    )(proj, conv_w)


def _gdn_gate_body(m_ref, alog_ref, dtb_ref, gc_ref, beta_ref, gct_ref, *, ts, chunk, nhv):
    x = m_ref[...]
    z = x + dtb_ref[...]
    softplus = jnp.maximum(z, 0.0) + jnp.log(1.0 + jnp.exp(-jnp.abs(z)))
    g = -jnp.exp(alog_ref[...]) * softplus
    pos = lax.broadcasted_iota(I32, (ts, LANES), 0) % chunk
    d = 1
    while d < chunk:
        g = g + jnp.where(pos >= d, pltpu.roll(g, d, axis=0), 0.0)
        d *= 2
    gc_ref[...] = g
    gct_ref[...] = g.T
    beta_ref[...] = pltpu.roll(jax.nn.sigmoid(x), LANES - nhv, axis=1)


def _gdn_gates(proj, alog_p, dtb_p, bsz, seq, misc_blk, nhv):
    ts = min(seq, 512)
    ns = seq // ts
    body = functools.partial(_gdn_gate_body, ts=ts, chunk=GDN_CHUNK, nhv=nhv)
    return pl.pallas_call(
        body,
        grid=(bsz, ns),
        in_specs=[pl.BlockSpec((ts, LANES), lambda b, s: (b * ns + s, misc_blk)),
                  pl.BlockSpec((1, LANES), lambda b, s: (0, 0)),
                  pl.BlockSpec((1, LANES), lambda b, s: (0, 0))],
        out_specs=[pl.BlockSpec((ts, LANES), lambda b, s: (b * ns + s, 0)),
                   pl.BlockSpec((ts, LANES), lambda b, s: (b * ns + s, 0)),
                   pl.BlockSpec((None, LANES, ts), lambda b, s: (b, 0, s))],
        out_shape=[jax.ShapeDtypeStruct((bsz * seq, LANES), F32),
                   jax.ShapeDtypeStruct((bsz * seq, LANES), F32),
                   jax.ShapeDtypeStruct((bsz, LANES, seq), F32)],
        compiler_params=_cp("parallel", "parallel"),
        name="gdn_gates",
    )(proj, alog_p, dtb_p)


def _gdn_core_body(q_ref, k_ref, v_ref, gc_ref, beta_ref, gr_ref, z_ref, nw_ref, o_ref, st_ref,
                   *, ts, chunk):
    hv = pl.program_id(1)

    @pl.when(pl.program_id(2) == 0)
    def _():
        st_ref[...] = jnp.zeros(st_ref.shape, F32)

    lane = lax.broadcasted_iota(I32, (ts, LANES), 1)
    gc_all = jnp.sum(jnp.where(lane == hv, gc_ref[...], 0.0), axis=-1, keepdims=True)
    beta_all = jnp.sum(jnp.where(lane == hv, beta_ref[...], 0.0), axis=-1, keepdims=True)
    ri = lax.broadcasted_iota(I32, (chunk, chunk), 0)
    ci = lax.broadcasted_iota(I32, (chunk, chunk), 1)
    incl = ri >= ci
    strict = ri > ci
    eye = jnp.where(ri == ci, 1.0, 0.0).astype(F32)
    nw = nw_ref[...]
    state = st_ref[...]
    n_dbl = max(1, (chunk - 1).bit_length()) - 1

    for c in range(ts // chunk):
        sl = slice(c * chunk, (c + 1) * chunk)
        q = q_ref[sl, :]
        k = k_ref[sl, :]
        v = v_ref[sl, :]
        gc = gc_all[sl, :]
        beta = beta_all[sl, :]
        gr = gr_ref[c:c + 1, :]
        g_last = gc[chunk - 1:chunk, :]
        k16 = k.astype(BF16)
        kb = k * beta
        decay = jnp.where(incl, jnp.exp(gc - gr), 0.0)
        a_mat = jnp.where(strict, _dot_nt(kb.astype(BF16), k16) * decay, 0.0)
        qk = jnp.where(incl, _dot_nt(q.astype(BF16), k16) * decay, 0.0)
        t_mat = eye - a_mat
        pw = a_mat
        for _ in range(n_dbl):
            pw16 = pw.astype(BF16)
            pw = _dot(pw16, pw16)
            t_mat = t_mat + _dot(t_mat.astype(BF16), pw.astype(BF16))
        t16 = t_mat.astype(BF16)
        eg = jnp.exp(gc)
        u = _dot(t16, (v * beta).astype(BF16))
        w = _dot(t16, (kb * eg).astype(BF16))
        st16 = state.astype(BF16)
        v_new = u - _dot(w.astype(BF16), st16)
        vn16 = v_new.astype(BF16)
        o = _dot((q * eg).astype(BF16), st16) + _dot(qk.astype(BF16), vn16)
        k_tail = (k * jnp.exp(g_last - gc)).astype(BF16)
        state = state * jnp.exp(g_last) + _dot_tn(k_tail, vn16)
        zz = z_ref[sl, :]
        o_ref[sl, :] = (_rms(o, nw) * (zz * jax.nn.sigmoid(zz))).astype(o_ref.dtype)

    st_ref[...] = state


def _gdn_core(qkv, gc, beta, gct4, proj, norm_w, bsz, seq, nqk, nhv, hd, z_blk0):
    ts = min(seq, 512)
    ns = seq // ts
    rep = nhv // nqk
    body = functools.partial(_gdn_core_body, ts=ts, chunk=GDN_CHUNK)
    row = lambda b, h, s: b * ns + s
    return pl.pallas_call(
        body,
        grid=(bsz, nhv, ns),
        in_specs=[pl.BlockSpec((ts, hd), lambda b, h, s: (row(b, h, s), h // rep)),
                  pl.BlockSpec((ts, hd), lambda b, h, s: (row(b, h, s), nqk + h // rep)),
                  pl.BlockSpec((ts, hd), lambda b, h, s: (row(b, h, s), 2 * nqk + h)),
                  pl.BlockSpec((ts, LANES), lambda b, h, s: (row(b, h, s), 0)),
                  pl.BlockSpec((ts, LANES), lambda b, h, s: (row(b, h, s), 0)),
                  pl.BlockSpec((None, None, ts // GDN_CHUNK, GDN_CHUNK), lambda b, h, s: (b, h, s, 0)),
                  pl.BlockSpec((ts, hd), lambda b, h, s: (row(b, h, s), z_blk0 + h)),
                  pl.BlockSpec((1, hd), lambda b, h, s: (0, 0))],
        out_specs=pl.BlockSpec((ts, hd), lambda b, h, s: (row(b, h, s), h)),
        out_shape=jax.ShapeDtypeStruct((bsz * seq, nhv * hd), BF16),
        scratch_shapes=[pltpu.VMEM((hd, hd), F32)],
        compiler_params=_cp("parallel", "parallel", "arbitrary"),
        name="gdn_core",
    )(qkv, qkv, qkv, gc, beta, gct4, proj, norm_w.reshape(1, hd))


def _merge_body(a_ref, b_ref, wa_ref, wb_ref, ga_ref, gb_ref, o_ref):
    ya = _dot(a_ref[...], wa_ref[...])
    yb = _dot(b_ref[...], wb_ref[...])
    o_ref[...] = (jax.nn.sigmoid(ga_ref[...]) * ya + jax.nn.sigmoid(gb_ref[...]) * yb).astype(o_ref.dtype)


def _merge(o_a, o_b, wa16, wb16, proj, ga_col0, gb_col0):
    t, ka = o_a.shape
    kb = o_b.shape[1]
    d = wa16.shape[1]
    tm = min(t, 512)
    tn = _tile(d, 512)
    ga0, gb0 = ga_col0 // tn, gb_col0 // tn
    return pl.pallas_call(
        _merge_body,
        grid=(t // tm, d // tn),
        in_specs=[pl.BlockSpec((tm, ka), lambda i, j: (i, 0)),
                  pl.BlockSpec((tm, kb), lambda i, j: (i, 0)),
                  pl.BlockSpec((ka, tn), lambda i, j: (0, j)),
                  pl.BlockSpec((kb, tn), lambda i, j: (0, j)),
                  pl.BlockSpec((tm, tn), lambda i, j: (i, ga0 + j)),
                  pl.BlockSpec((tm, tn), lambda i, j: (i, gb0 + j))],
        out_specs=pl.BlockSpec((tm, tn), lambda i, j: (i, j)),
        out_shape=jax.ShapeDtypeStruct((t, d), BF16),
        compiler_params=_cp("parallel", "parallel"),
        name="merge",
    )(o_a, o_b, wa16, wb16, proj, proj)


def _outproj_router_body(m_ref, x_ref, wo_ref, g1_ref, nw_ref, sc_ref, sh_ref, wr_ref, br_ref,
                         x1_ref, h2_ref, te_ref, tg_ref, *, topk):
    x1 = x_ref[...] + g1_ref[...] * _dot(m_ref[...], wo_ref[...])
    x1_ref[...] = x1
    h2 = _rms(x1, nw_ref[...]) * (1.0 + sc_ref[...]) + sh_ref[...]
    h2_ref[...] = h2
    logits = _dot(h2.astype(BF16), wr_ref[...]) + br_ref[...]
    lane = lax.broadcasted_iota(I32, logits.shape, 1)
    te = jnp.zeros(logits.shape, I32)
    tg = jnp.zeros(logits.shape, F32)
    denom = jnp.zeros((logits.shape[0], 1), F32)
    m0 = None
    for kk in range(topk):
        mx = jnp.max(logits, axis=-1, keepdims=True)
        idx = jnp.min(jnp.where(logits == mx, lane, LANES), axis=-1, keepdims=True)
        if kk == 0:
            m0 = mx
        e = jnp.exp(mx - m0)
        denom = denom + e
        te = jnp.where(lane == kk, idx, te)
        tg = jnp.where(lane == kk, e, tg)
        logits = jnp.where(lane == idx, -jnp.inf, logits)
    te_ref[...] = te
    tg_ref[...] = tg / denom


def _outproj_router(merged, x2, wo16, gate1, nw, scale, shift, wr16, br_p, seq):
    t, d = x2.shape
    tm = min(seq, 256)
    per_b = seq // tm
    vec = lambda: pl.BlockSpec((None, 1, d), lambda i: (i // per_b, 0, 0))
    return pl.pallas_call(
        functools.partial(_outproj_router_body, topk=TOP_K),
        grid=(t // tm,),
        in_specs=[pl.BlockSpec((tm, d), lambda i: (i, 0)),
                  pl.BlockSpec((tm, d), lambda i: (i, 0)),
                  pl.BlockSpec((d, d), lambda i: (0, 0)),
                  vec(),
                  pl.BlockSpec((1, d), lambda i: (0, 0)),
                  vec(), vec(),
                  pl.BlockSpec((d, LANES), lambda i: (0, 0)),
                  pl.BlockSpec((1, LANES), lambda i: (0, 0))],
        out_specs=[pl.BlockSpec((tm, d), lambda i: (i, 0)),
                   pl.BlockSpec((tm, d), lambda i: (i, 0)),
                   pl.BlockSpec((tm, LANES), lambda i: (i, 0)),
                   pl.BlockSpec((tm, LANES), lambda i: (i, 0))],
        out_shape=[jax.ShapeDtypeStruct((t, d), F32),
                   jax.ShapeDtypeStruct((t, d), F32),
                   jax.ShapeDtypeStruct((t, LANES), I32),
                   jax.ShapeDtypeStruct((t, LANES), F32)],
        compiler_params=_cp("parallel"),
        name="outproj_router",
    )(merged, x2, wo16, gate1[:, None, :], nw.reshape(1, d), scale[:, None, :], shift[:, None, :], wr16, br_p)


def _moe_rank_body(te_ref, pos_ref, cnt_ref, carry_ref, *, tt, topk):
    @pl.when(pl.program_id(0) == 0)
    def _():
        carry_ref[...] = jnp.zeros(carry_ref.shape, F32)

    te = te_ref[...]
    lane = lax.broadcasted_iota(I32, (tt, LANES), 1)
    onehot = jnp.zeros((tt, LANES), F32)
    for kk in range(topk):
        onehot = onehot + jnp.where(lane == te[:, kk:kk + 1], 1.0, 0.0)
    r = lax.broadcasted_iota(I32, (tt, tt), 0)
    c = lax.broadcasted_iota(I32, (tt, tt), 1)
    below = jnp.where(r > c, 1.0, 0.0).astype(BF16)
    rank = _dot(below, onehot.astype(BF16)) + carry_ref[0:1, :]
    pos = jnp.zeros((tt, LANES), I32)
    for kk in range(topk):
        p = jnp.sum(jnp.where(lane == te[:, kk:kk + 1], rank, 0.0), axis=-1, keepdims=True)
        pos = jnp.where(lane == kk, p.astype(I32), pos)
    pos_ref[...] = pos
    carry_ref[...] = carry_ref[...] + jnp.sum(onehot, axis=0, keepdims=True)
    cnt_ref[...] = carry_ref[...]


def _moe_rank(te):
    t = te.shape[0]
    tt = min(t, 512)
    return pl.pallas_call(
        functools.partial(_moe_rank_body, tt=tt, topk=TOP_K),
        grid=(t // tt,),
        in_specs=[pl.BlockSpec((tt, LANES), lambda i: (i, 0))],
        out_specs=[pl.BlockSpec((tt, LANES), lambda i: (i, 0)),
                   pl.BlockSpec((8, LANES), lambda i: (0, 0))],
        out_shape=[jax.ShapeDtypeStruct((t, LANES), I32),
                   jax.ShapeDtypeStruct((8, LANES), F32)],
        scratch_shapes=[pltpu.VMEM((8, LANES), F32)],
        compiler_params=_cp("arbitrary"),
        name="moe_rank",
    )(te)


def _moe_dispatch_body(dest_ref, h_ref, zero_ref, x_hbm, sem, *, tt, topk):
    del zero_ref
    base = pl.program_id(0) * tt * topk

    def row_copy(r, d):
        return pltpu.make_async_copy(h_ref.at[pl.ds(r // topk, 1)], x_hbm.at[pl.ds(d, 1)], sem)

    def issue(r, _):
        row_copy(r, dest_ref[base + r]).start()
        return 0

    def drain(r, _):
        row_copy(r, dest_ref[base + r]).wait()
        return 0

    lax.fori_loop(0, tt * topk, issue, 0)
    lax.fori_loop(0, tt * topk, drain, 0)


def _moe_dispatch(dest_flat, h2, n_rows):
    t, d = h2.shape
    tt = min(t, 128)
    grid_spec = pltpu.PrefetchScalarGridSpec(
        num_scalar_prefetch=1,
        grid=(t // tt,),
        in_specs=[pl.BlockSpec((tt, d), lambda i, dest: (i, 0)),
                  pl.BlockSpec(memory_space=pl.ANY)],
        out_specs=pl.BlockSpec(memory_space=pl.ANY),
        scratch_shapes=[pltpu.SemaphoreType.DMA(())])
    return pl.pallas_call(
        functools.partial(_moe_dispatch_body, tt=tt, topk=TOP_K),
        grid_spec=grid_spec,
        out_shape=jax.ShapeDtypeStruct((n_rows, d), F32),
        input_output_aliases={2: 0},
        compiler_params=_cp("arbitrary"),
        name="moe_dispatch",
    )(dest_flat, h2, jnp.zeros((n_rows, d), F32))


def _moe_gu_body(be_ref, na_ref, x_ref, wg_ref, wu_ref, bg_ref, bu_ref, act_ref, wg16_ref, wu16_ref):
    i = pl.program_id(1)
    prev = be_ref[jnp.maximum(i - 1, 0)]

    @pl.when((i == 0) | (be_ref[i] != prev))
    def _():
        wg16_ref[...] = wg_ref[...].astype(BF16)
        wu16_ref[...] = wu_ref[...].astype(BF16)

    @pl.when(i < na_ref[0])
    def _():
        x = x_ref[...].astype(BF16)
        g = jnp.minimum(_dot(x, wg16_ref[...]) + bg_ref[...], SWIGLU_LIMIT)
        u = jnp.clip(_dot(x, wu16_ref[...]) + bu_ref[...], -SWIGLU_LIMIT, SWIGLU_LIMIT)
        act_ref[...] = ((u + 1.0) * (g * jax.nn.sigmoid(SWIGLU_ALPHA * g))).astype(act_ref.dtype)

    @pl.when(i >= na_ref[0])
    def _():
        act_ref[...] = jnp.zeros(act_ref.shape, act_ref.dtype)


def _moe_gu(block_e, n_act, x_rows, w_gu, b_gu):
    n_rows, d = x_rows.shape
    ne, _, ff2 = w_gu.shape
    ff = ff2 // 2
    bm = MOE_BM
    tn = _tile(ff, 512)
    nt = ff // tn
    rowblk = lambda j, i, be, na: (jnp.minimum(i, na[0] - 1), 0)
    grid_spec = pltpu.PrefetchScalarGridSpec(
        num_scalar_prefetch=2,
        grid=(nt, n_rows // bm),
        in_specs=[pl.BlockSpec((bm, d), rowblk),
                  pl.BlockSpec((None, d, tn), lambda j, i, be, na: (be[i], 0, j)),
                  pl.BlockSpec((None, d, tn), lambda j, i, be, na: (be[i], 0, nt + j)),
                  pl.BlockSpec((None, 1, tn), lambda j, i, be, na: (be[i], 0, j)),
                  pl.BlockSpec((None, 1, tn), lambda j, i, be, na: (be[i], 0, nt + j))],
        out_specs=pl.BlockSpec((bm, tn), lambda j, i, be, na: (i, j)),
        scratch_shapes=[pltpu.VMEM((d, tn), BF16), pltpu.VMEM((d, tn), BF16)])
    return pl.pallas_call(
        _moe_gu_body,
        grid_spec=grid_spec,
        out_shape=jax.ShapeDtypeStruct((n_rows, ff), BF16),
        compiler_params=_cp("arbitrary", "arbitrary"),
        name="moe_gu",
    )(block_e, n_act, x_rows, w_gu, w_gu, b_gu.reshape(ne, 1, ff2), b_gu.reshape(ne, 1, ff2))


def _moe_down_body(be_ref, na_ref, a_ref, w_ref, b_ref, y_ref, w16_ref):
    i = pl.program_id(1)
    prev = be_ref[jnp.maximum(i - 1, 0)]

    @pl.when((i == 0) | (be_ref[i] != prev))
    def _():
        w16_ref[...] = w_ref[...].astype(BF16)

    @pl.when(i < na_ref[0])
    def _():
        y_ref[...] = _dot(a_ref[...], w16_ref[...]) + b_ref[...]

    @pl.when(i >= na_ref[0])
    def _():
        y_ref[...] = jnp.zeros(y_ref.shape, y_ref.dtype)


def _moe_down(block_e, n_act, act, w_down, b_down):
    n_rows, ff = act.shape
    ne, _, d = w_down.shape
    bm = MOE_BM
    tn = _tile(d, 1024)
    grid_spec = pltpu.PrefetchScalarGridSpec(
        num_scalar_prefetch=2,
        grid=(d // tn, n_rows // bm),
        in_specs=[pl.BlockSpec((bm, ff), lambda j, i, be, na: (jnp.minimum(i, na[0] - 1), 0)),
                  pl.BlockSpec((None, ff, tn), lambda j, i, be, na: (be[i], 0, j)),
                  pl.BlockSpec((None, 1, tn), lambda j, i, be, na: (be[i], 0, j))],
        out_specs=pl.BlockSpec((bm, tn), lambda j, i, be, na: (i, j)),
        scratch_shapes=[pltpu.VMEM((ff, tn), BF16)])
    return pl.pallas_call(
        _moe_down_body,
        grid_spec=grid_spec,
        out_shape=jax.ShapeDtypeStruct((n_rows, d), F32),
        compiler_params=_cp("arbitrary", "arbitrary"),
        name="moe_down",
    )(block_e, n_act, act, w_down, b_down.reshape(ne, 1, d))


def _moe_combine_body(dest_ref, x1_ref, tg_ref, g2_ref, fw_ref, y_hbm, o_ref, buf_ref, sem, *, tt, topk):
    base = pl.program_id(0) * tt * topk

    def row_copy(r, d):
        return pltpu.make_async_copy(y_hbm.at[pl.ds(d, 1)], buf_ref.at[r % topk, pl.ds(r // topk, 1)], sem)

    def issue(r, _):
        row_copy(r, dest_ref[base + r]).start()
        return 0

    def drain(r, _):
        row_copy(r, dest_ref[base + r]).wait()
        return 0

    lax.fori_loop(0, tt * topk, issue, 0)
    lax.fori_loop(0, tt * topk, drain, 0)
    tg = tg_ref[...]
    y = tg[:, 0:1] * buf_ref[0]
    for kk in range(1, topk):
        y = y + tg[:, kk:kk + 1] * buf_ref[kk]
    o_ref[...] = _rms(x1_ref[...] + g2_ref[...] * y, fw_ref[...])


def _moe_combine(dest_flat, x1, tg, gate2, final_w, y_rows, seq):
    t, d = x1.shape
    tt = min(seq, 128)
    per_b = seq // tt
    grid_spec = pltpu.PrefetchScalarGridSpec(
        num_scalar_prefetch=1,
        grid=(t // tt,),
        in_specs=[pl.BlockSpec((tt, d), lambda i, dest: (i, 0)),
                  pl.BlockSpec((tt, LANES), lambda i, dest: (i, 0)),
                  pl.BlockSpec((None, 1, d), lambda i, dest: (i // per_b, 0, 0)),
                  pl.BlockSpec((1, d), lambda i, dest: (0, 0)),
                  pl.BlockSpec(memory_space=pl.ANY)],
        out_specs=pl.BlockSpec((tt, d), lambda i, dest: (i, 0)),
        scratch_shapes=[pltpu.VMEM((TOP_K, tt, d), F32), pltpu.SemaphoreType.DMA(())])
    return pl.pallas_call(
        functools.partial(_moe_combine_body, tt=tt, topk=TOP_K),
        grid_spec=grid_spec,
        out_shape=jax.ShapeDtypeStruct((t, d), F32),
        compiler_params=_cp("arbitrary"),
        name="moe_combine",
    )(dest_flat, x1, tg, gate2[:, None, :], final_w.reshape(1, d), y_rows)


def _pad_lanes(v, off=0, fill=0.0):
    out = jnp.full((1, LANES), fill, F32)
    return out.at[0, off:off + v.shape[0]].set(v.astype(F32))


def _layer(x2, c, bsz, seq, p, final_w):
    d = x2.shape[1]
    qr, nh, dh = p["w_uq"].shape
    kvr = p["w_uk"].shape[0]
    nhi, di = p["w_uqi"].shape[1:]
    hd = p["gdn_norm_w"].shape[0]
    nhv = p["a_log"].shape[0]
    v_w = nhv * hd
    qk_w = (p["conv_w"].shape[1] - v_w) // 2
    nqk = qk_w // hd
    ne = p["w_router"].shape[1]

    mod = _ada(c, p["ada_w"], p["ada_b"])
    shift1, scale1, gate1, shift2, scale2, gate2 = jnp.split(mod, 6, axis=-1)

    widths = (qr, kvr, di, nhi, qk_w, qk_w, v_w, nhv, nhv, v_w, d, d)
    offs = [0]
    for wd in widths:
        offs.append(offs[-1] + wd)
    col = lambda k: p["w_in"][:, offs[k]:offs[k + 1]]
    pad = jnp.zeros((d, LANES - 2 * nhv - nhi), F32)
    w_in = jnp.concatenate([col(0), col(1), col(2), col(7), col(8), col(3), pad,
                            col(4), col(5), col(6), col(9), col(10), col(11)], axis=1).astype(BF16)
    lat_w = qr + kvr + di + LANES
    gq_col = lat_w
    z_col = gq_col + 2 * qk_w + v_w
    ga_col = z_col + v_w
    gb_col = ga_col + d

    proj = _inproj(x2, p["norm1_w"], scale1, shift1, w_in, seq)

    wuq_t = p["w_uq"].reshape(qr, nh * dh).T.astype(BF16)
    wuqi_t = p["w_uqi"].reshape(qr, nhi * di).T.astype(BF16)
    wuk_r = jnp.transpose(p["w_uk"], (1, 0, 2)).astype(BF16)
    wuv_t = jnp.transpose(p["w_uv"], (1, 2, 0)).astype(BF16)
    qabs_t, qi_t, ckv, ckvt, ki, wi_t = _dsa_prep(
        proj, bsz, seq, p["q_lat_norm_w"], p["kv_lat_norm_w"], p["idx_k_norm_w"], wuq_t, wuqi_t, wuk_r,
        (qr, kvr, di, nh, dh, nhi, 2 * nhv))
    o_a = _dsa_attn(ki, ckv, ckvt, qi_t, qabs_t, wi_t, wuv_t).reshape(bsz * seq, nh * dh)

    qkv = _gdn_conv(proj, p["conv_w"], bsz, seq, gq_col, qk_w, v_w, hd)
    gc, beta, gct = _gdn_gates(proj, _pad_lanes(p["a_log"]), _pad_lanes(p["dt_bias"]), bsz, seq,
                               (lat_w - LANES) // LANES, nhv)
    gct4 = gct.reshape(bsz, LANES, seq // GDN_CHUNK, GDN_CHUNK)
    o_b = _gdn_core(qkv, gc, beta, gct4, proj, p["gdn_norm_w"], bsz, seq, nqk, nhv, hd, z_col // hd)

    merged = _merge(o_a, o_b, p["w_branch_a"].astype(BF16), p["w_branch_b"].astype(BF16), proj, ga_col, gb_col)

    wr16 = jnp.zeros((d, LANES), F32).at[:, :ne].set(p["w_router"]).astype(BF16)
    br_p = _pad_lanes(p["b_router"], fill=-1e30)
    x1, h2, te, tg = _outproj_router(merged, x2, p["w_out"].astype(BF16), gate1, p["norm2_w"], scale2, shift2,
                                     wr16, br_p, seq)

    t = bsz * seq
    pos, cnt = _moe_rank(te)
    counts = cnt[0, :ne].astype(I32)
    padded = (counts + MOE_BM - 1) // MOE_BM * MOE_BM
    pad_end = jnp.cumsum(padded)
    pad_start = pad_end - padded
    dest = (pad_start[te[:, :TOP_K]] + pos[:, :TOP_K]).reshape(-1).astype(I32)
    n_blocks = -(-(t * TOP_K) // MOE_BM) + ne
    n_act = (pad_end[-1] // MOE_BM).astype(I32).reshape(1)
    blk = jnp.minimum(jnp.arange(n_blocks, dtype=I32), n_act[0] - 1) * MOE_BM
    block_e = jnp.minimum(jnp.searchsorted(pad_end, blk, side="right"), ne - 1).astype(I32)
    x_rows = _moe_dispatch(dest, h2, n_blocks * MOE_BM)
    act = _moe_gu(block_e, n_act, x_rows, p["w_gu"], p["b_gu"])
    y_rows = _moe_down(block_e, n_act, act, p["w_down"], p["b_down"])
    return _moe_combine(dest, x1, tg, gate2, final_w, y_rows, seq)


@jax.jit
def kernel(x, c, ada_w, ada_b, norm1_w, w_in, q_lat_norm_w, kv_lat_norm_w, idx_k_norm_w, w_uq, w_uqi, w_uk, w_uv, conv_w, a_log, dt_bias, gdn_norm_w, w_branch_a, w_branch_b, w_out, norm2_w, w_router, b_router, w_gu, b_gu, w_down, b_down, final_norm_w):
    bsz, seq, d = x.shape
    stacked = dict(ada_w=ada_w, ada_b=ada_b, norm1_w=norm1_w, w_in=w_in, q_lat_norm_w=q_lat_norm_w,
                   kv_lat_norm_w=kv_lat_norm_w, idx_k_norm_w=idx_k_norm_w, w_uq=w_uq, w_uqi=w_uqi, w_uk=w_uk,
                   w_uv=w_uv, conv_w=conv_w, a_log=a_log, dt_bias=dt_bias, gdn_norm_w=gdn_norm_w,
                   w_branch_a=w_branch_a, w_branch_b=w_branch_b, w_out=w_out, norm2_w=norm2_w,
                   w_router=w_router, b_router=b_router, w_gu=w_gu, b_gu=b_gu, w_down=w_down, b_down=b_down)
    depth = ada_w.shape[0]
    assert depth == 1, "the final norm is fused into the last layer's combine kernel"
    x2 = x.reshape(bsz * seq, d)
    p = {k: v[0] for k, v in stacked.items()}
    out = _layer(x2, c, bsz, seq, p, final_norm_w)
    return out.reshape(bsz, seq, d)
```

```python
import functools

import jax
import jax.numpy as jnp
from jax import lax
from jax.experimental import pallas as pl
from jax.experimental.pallas import tpu as pltpu

F32 = jnp.float32
BF16 = jnp.bfloat16
I32 = jnp.int32

EPS = 1e-6
LANES = 128
VMEM_LIMIT = 56 * 1024 * 1024

IDX_TOPK = 256
GDN_CHUNK = 64
TOP_K = 4
SWIGLU_LIMIT = 7.0
SWIGLU_ALPHA = 1.702
MOE_BM = 512
INT_MIN = -(2 ** 31)
INT_MAX = 2 ** 31 - 1


def _cp(*sem):
    return pltpu.CompilerParams(dimension_semantics=sem, vmem_limit_bytes=VMEM_LIMIT)


def _tile(n, pref):
    if n <= pref:
        return n
    t = pref - pref % LANES
    while n % t:
        t -= LANES
    return t


def _rms(x, w):
    return x * lax.rsqrt(jnp.mean(x * x, axis=-1, keepdims=True) + EPS) * w


def _dot(a, b):
    return jnp.dot(a, b, preferred_element_type=F32)


def _dot_nt(a, b):
    return lax.dot_general(a, b, (((1,), (1,)), ((), ())), preferred_element_type=F32)


def _dot_tn(a, b):
    return lax.dot_general(a, b, (((0,), (0,)), ((), ())), preferred_element_type=F32)


def _ada_body(ct_ref, w_ref, b_ref, o_ref, *, nb):
    ct = ct_ref[...]
    s = ct * jax.nn.sigmoid(ct)
    w = w_ref[...]
    for b in range(nb):
        o_ref[b:b + 1, :] = jnp.sum(w * s[:, b:b + 1], axis=0, keepdims=True) + b_ref[...]


def _ada(c, w, b):
    nb, d = c.shape
    n = w.shape[1]
    tn = _tile(n, 1024)
    return pl.pallas_call(
        functools.partial(_ada_body, nb=nb),
        grid=(n // tn,),
        in_specs=[pl.BlockSpec((d, nb), lambda j: (0, 0)),
                  pl.BlockSpec((d, tn), lambda j: (0, j)),
                  pl.BlockSpec((1, tn), lambda j: (0, j))],
        out_specs=pl.BlockSpec((nb, tn), lambda j: (0, j)),
        out_shape=jax.ShapeDtypeStruct((nb, n), F32),
        compiler_params=_cp("parallel"),
        name="ada",
    )(c.T, w, b.reshape(1, n))


def _inproj_body(x_ref, nw_ref, sc_ref, sh_ref, w_ref, o_ref, h_ref):
    @pl.when(pl.program_id(1) == 0)
    def _():
        h = _rms(x_ref[...], nw_ref[...]) * (1.0 + sc_ref[...]) + sh_ref[...]
        h_ref[...] = h.astype(BF16)

    o_ref[...] = _dot(h_ref[...], w_ref[...])


def _inproj(x2, nw, scale, shift, w16, seq):
    t, d = x2.shape
    n = w16.shape[1]
    tm = min(seq, 1024)
    tn = _tile(n, 512)
    per_b = seq // tm
    return pl.pallas_call(
        _inproj_body,
        grid=(t // tm, n // tn),
        in_specs=[pl.BlockSpec((tm, d), lambda i, j: (i, 0)),
                  pl.BlockSpec((1, d), lambda i, j: (0, 0)),
                  pl.BlockSpec((None, 1, d), lambda i, j: (i // per_b, 0, 0)),
                  pl.BlockSpec((None, 1, d), lambda i, j: (i // per_b, 0, 0)),
                  pl.BlockSpec((d, tn), lambda i, j: (0, j))],
        out_specs=pl.BlockSpec((tm, tn), lambda i, j: (i, j)),
        out_shape=jax.ShapeDtypeStruct((t, n), F32),
        scratch_shapes=[pltpu.VMEM((tm, d), BF16)],
        compiler_params=_cp("parallel", "arbitrary"),
        name="inproj",
    )(x2, nw.reshape(1, d), scale[:, None, :], shift[:, None, :], w16)


def _dsa_prep_body(lat_ref, qnw_ref, kvnw_ref, knw_ref, wuq_ref, wuqi_ref, wuk_ref,
                   qabs_ref, qi_ref, ckv_ref, ckvt_ref, ki_ref, wi_ref,
                   *, qr, kvr, di, nh, dh, nhi, q_scale, i_scale, w_scale, wi_off):
    lat = lat_ref[...]
    cq_t = _rms(lat[:, :qr], qnw_ref[...]).T.astype(BF16)
    q_t = _dot(wuq_ref[...], cq_t)
    for h in range(nh):
        q_h = q_t[h * dh:(h + 1) * dh].astype(BF16)
        qabs_ref[h] = (_dot(wuk_ref[h], q_h) * q_scale).astype(BF16)
    qi_t = _dot(wuqi_ref[...], cq_t) * i_scale
    for h in range(nhi):
        qi_ref[h] = qi_t[h * di:(h + 1) * di].astype(BF16)
    ckv = _rms(lat[:, qr:qr + kvr], kvnw_ref[...])
    ckv_ref[...] = ckv.astype(BF16)
    ckvt_ref[...] = ckv.T.astype(BF16)
    ki_ref[...] = _rms(lat[:, qr + kvr:qr + kvr + di], knw_ref[...]).astype(BF16)
    misc_t = lat[:, qr + kvr + di:].T
    wi_ref[...] = misc_t[wi_off:wi_off + nhi] * w_scale


def _dsa_prep(proj, bsz, seq, qnw, kvnw, knw, wuq_t, wuqi_t, wuk_r, dims):
    qr, kvr, di, nh, dh, nhi, wi_off = dims
    tm = min(seq, 512)
    ns = seq // tm
    lat_w = qr + kvr + di + LANES
    body = functools.partial(
        _dsa_prep_body, qr=qr, kvr=kvr, di=di, nh=nh, dh=dh, nhi=nhi,
        q_scale=dh ** -0.5, i_scale=di ** -0.5, w_scale=nhi ** -0.5, wi_off=wi_off)
    full = lambda shape: pl.BlockSpec(shape, lambda b, s: (0,) * len(shape))
    return pl.pallas_call(
        body,
        grid=(bsz, ns),
        in_specs=[pl.BlockSpec((tm, lat_w), lambda b, s: (b * ns + s, 0)),
                  full((1, qr)), full((1, kvr)), full((1, di)),
                  full((nh * dh, qr)), full((nhi * di, qr)), full((nh, kvr, dh))],
        out_specs=[pl.BlockSpec((None, nh, kvr, tm), lambda b, s: (b, 0, 0, s)),
                   pl.BlockSpec((None, nhi, di, tm), lambda b, s: (b, 0, 0, s)),
                   pl.BlockSpec((None, tm, kvr), lambda b, s: (b, s, 0)),
                   pl.BlockSpec((None, kvr, tm), lambda b, s: (b, 0, s)),
                   pl.BlockSpec((None, tm, di), lambda b, s: (b, s, 0)),
                   pl.BlockSpec((None, nhi, tm), lambda b, s: (b, 0, s))],
        out_shape=[jax.ShapeDtypeStruct((bsz, nh, kvr, seq), BF16),
                   jax.ShapeDtypeStruct((bsz, nhi, di, seq), BF16),
                   jax.ShapeDtypeStruct((bsz, seq, kvr), BF16),
                   jax.ShapeDtypeStruct((bsz, kvr, seq), BF16),
                   jax.ShapeDtypeStruct((bsz, seq, di), BF16),
                   jax.ShapeDtypeStruct((bsz, nhi, seq), F32)],
        compiler_params=_cp("parallel", "parallel"),
        name="dsa_prep",
    )(proj, qnw.reshape(1, qr), kvnw.reshape(1, kvr), knw.reshape(1, di), wuq_t, wuqi_t, wuk_r)


def _dsa_attn_body(ki_ref, ckv_ref, ckvt_ref, qi_ref, qabs_ref, wi_ref, wuv_ref, o_ref,
                   keys_ref, acc_ref, m_ref, l_ref, ot_ref, thr_ref, jlim_ref,
                   *, tq, tk, nh, nhi, dh, topk, seq_bits):
    qb = pl.program_id(1)
    nk = ((qb + 1) * tq + tk - 1) // tk
    q_pos = qb * tq + lax.broadcasted_iota(I32, (tk, tq), 1)
    k_iota = lax.broadcasted_iota(I32, (tk, tq), 0)

    def score_tile(kt, _):
        ki_t = ki_ref[pl.ds(pl.multiple_of(kt * tk, tk), tk), :]
        acc = jnp.zeros((tk, tq), F32)
        for h in range(nhi):
            rel = _dot(ki_t, qi_ref[h])
            acc = acc + wi_ref[h:h + 1, :] * jnp.maximum(rel, 0.0)
        bits = lax.bitcast_convert_type(acc, I32)
        key = jnp.where(bits < 0, bits ^ INT_MAX, bits)
        key = jnp.where(kt * tk + k_iota <= q_pos, key, INT_MIN)
        keys_ref[pl.ds(pl.multiple_of(kt * tk, tk), tk), :] = key
        return 0

    lax.fori_loop(0, nk, score_tile, 0)

    def count(pred_fn):
        def tile(kt, c):
            k_t = keys_ref[pl.ds(pl.multiple_of(kt * tk, tk), tk), :]
            hit = jnp.where(pred_fn(k_t, kt), 1, 0).astype(I32)
            return c + jnp.sum(hit.reshape(tk // 8, 8, tq), axis=0)
        c8 = lax.fori_loop(0, nk, tile, jnp.zeros((8, tq), I32))
        return jnp.sum(c8, axis=0, keepdims=True)

    def bit_cond(carry):
        it, _, settled = carry
        return (it < 32) & (jnp.min(settled) == 0)

    def bit_step(carry):
        it, thr, settled = carry
        for _ in range(4):
            cand = thr ^ lax.shift_left(jnp.int32(1), 31 - it)
            cnt = count(lambda k_t, kt: k_t >= cand)
            take = (cnt >= topk) & (settled == 0)
            settled = jnp.where(take & (cnt == topk), 1, settled)
            thr = jnp.where(take, cand, thr)
            it = it + 1
        return it, thr, settled

    _, thr, _ = lax.while_loop(
        bit_cond, bit_step, (jnp.int32(0), jnp.full((1, tq), INT_MIN, I32), jnp.zeros((1, tq), I32)))
    thr = jnp.maximum(thr, INT_MIN + 1)
    cnt_gt = count(lambda k_t, kt: k_t > thr)
    cnt_ge = count(lambda k_t, kt: k_t >= thr)
    thr_ref[...] = thr
    jlim_ref[...] = jnp.full((1, tq), INT_MAX, I32)

    @pl.when(jnp.max(cnt_ge) > topk)
    def _():
        need = topk - cnt_gt

        def pos_step(it, p):
            cand = p + lax.shift_left(jnp.int32(1), seq_bits - 1 - it)
            cnt = count(lambda k_t, kt: (k_t == thr) & (kt * tk + k_iota < cand))
            return jnp.where(cnt < need, cand, p)

        jlim_ref[...] = lax.fori_loop(0, seq_bits, pos_step, jnp.zeros((1, tq), I32))

    m_ref[...] = jnp.full(m_ref.shape, -jnp.inf, F32)
    l_ref[...] = jnp.zeros(l_ref.shape, F32)
    acc_ref[...] = jnp.zeros(acc_ref.shape, F32)

    def attn_tile(kt, _):
        off = pl.multiple_of(kt * tk, tk)
        k_t = keys_ref[pl.ds(off, tk), :]
        thr_b = thr_ref[...]
        mask = (k_t > thr_b) | ((k_t == thr_b) & (kt * tk + k_iota <= jlim_ref[...]))
        ckv_t = ckv_ref[pl.ds(off, tk), :]
        ckvt_t = ckvt_ref[:, pl.ds(off, tk)]
        for h in range(nh):
            s = jnp.where(mask, _dot(ckv_t, qabs_ref[h]), -jnp.inf)
            m_old = m_ref[h]
            m_new = jnp.maximum(m_old, jnp.max(s, axis=0, keepdims=True))
            m_use = jnp.where(m_new == -jnp.inf, 0.0, m_new)
            p = jnp.exp(s - m_use)
            alpha = jnp.exp(m_old - m_use)
            l_ref[h] = alpha * l_ref[h] + jnp.sum(p, axis=0, keepdims=True)
            acc_ref[h] = alpha * acc_ref[h] + _dot(ckvt_t, p.astype(BF16))
            m_ref[h] = m_new
        return 0

    lax.fori_loop(0, nk, attn_tile, 0)

    for h in range(nh):
        o_lat = (acc_ref[h] / l_ref[h]).astype(BF16)
        ot_ref[h * dh:(h + 1) * dh, :] = _dot(wuv_ref[h], o_lat)
    o_ref[...] = ot_ref[...].T.astype(BF16)


def _dsa_attn(ki, ckv, ckvt, qi_t, qabs_t, wi_t, wuv_t):
    bsz, seq, di = ki.shape
    kvr = ckv.shape[2]
    nh, dh = wuv_t.shape[0], wuv_t.shape[1]
    nhi = qi_t.shape[1]
    tq = min(seq, 256)
    tk = tq
    topk = min(IDX_TOPK, seq // 4)
    body = functools.partial(_dsa_attn_body, tq=tq, tk=tk, nh=nh, nhi=nhi, dh=dh, topk=topk,
                             seq_bits=max(1, (seq - 1).bit_length()))
    return pl.pallas_call(
        body,
        grid=(bsz, seq // tq),
        in_specs=[pl.BlockSpec((None, seq, di), lambda b, q: (b, 0, 0)),
                  pl.BlockSpec((None, seq, kvr), lambda b, q: (b, 0, 0)),
                  pl.BlockSpec((None, kvr, seq), lambda b, q: (b, 0, 0)),
                  pl.BlockSpec((None, nhi, di, tq), lambda b, q: (b, 0, 0, q)),
                  pl.BlockSpec((None, nh, kvr, tq), lambda b, q: (b, 0, 0, q)),
                  pl.BlockSpec((None, nhi, tq), lambda b, q: (b, 0, q)),
                  pl.BlockSpec((nh, dh, kvr), lambda b, q: (0, 0, 0))],
        out_specs=pl.BlockSpec((None, tq, nh * dh), lambda b, q: (b, q, 0)),
        out_shape=jax.ShapeDtypeStruct((bsz, seq, nh * dh), BF16),
        scratch_shapes=[pltpu.VMEM((seq, tq), I32),
                        pltpu.VMEM((nh, kvr, tq), F32),
                        pltpu.VMEM((nh, 1, tq), F32),
                        pltpu.VMEM((nh, 1, tq), F32),
                        pltpu.VMEM((nh * dh, tq), F32),
                        pltpu.VMEM((1, tq), I32),
                        pltpu.VMEM((1, tq), I32)],
        compiler_params=_cp("parallel", "parallel"),
        name="dsa_attn",
    )(ki, ckv, ckvt, qi_t, qabs_t, wi_t, wuv_t)


def _gdn_conv_body(u_ref, w_ref, o_ref, carry_ref, *, ts, cb, hd, nq_blocks, q_scale):
    cblk = pl.program_id(1)

    @pl.when(pl.program_id(2) == 0)
    def _():
        carry_ref[...] = jnp.zeros(carry_ref.shape, F32)

    u = u_ref[...]
    w = w_ref[...]
    taps = w.shape[0]
    prev = carry_ref[...]
    u8 = u[0:8, :]
    row8 = lax.broadcasted_iota(I32, (8, cb), 0)
    y = u * w[taps - 1:taps, :]
    y8 = u8 * w[taps - 1:taps, :]
    for j in range(1, taps):
        wj = w[taps - 1 - j:taps - j, :]
        y = y + pltpu.roll(u, j, axis=0) * wj
        y8 = y8 + jnp.where(row8 < j, pltpu.roll(prev, j, axis=0), pltpu.roll(u8, j, axis=0)) * wj
    carry_ref[...] = u[ts - 8:, :]

    def finish(yy, rows, mode):
        yy = yy * jax.nn.sigmoid(yy)
        if mode is None:
            o_ref[rows, :] = yy
            return
        for hh in range(cb // hd):
            x = yy[:, hh * hd:(hh + 1) * hd]
            inv = lax.rsqrt(jnp.sum(x * x, axis=-1, keepdims=True) + EPS)
            o_ref[rows, hh * hd:(hh + 1) * hd] = x * (inv * mode)

    def emit(mode):
        finish(y, slice(0, ts), mode)
        finish(y8, slice(0, 8), mode)

    @pl.when(cblk < nq_blocks)
    def _():
        emit(q_scale)

    @pl.when((cblk >= nq_blocks) & (cblk < 2 * nq_blocks))
    def _():
        emit(1.0)

    @pl.when(cblk >= 2 * nq_blocks)
    def _():
        emit(None)


def _gdn_conv(proj, conv_w, bsz, seq, col0, qk_w, v_w, hd):
    ts = min(seq, 512)
    cb = min(qk_w, 1024)
    ns = seq // ts
    ncb = (2 * qk_w + v_w) // cb
    c0 = col0 // cb
    body = functools.partial(_gdn_conv_body, ts=ts, cb=cb, hd=hd, nq_blocks=qk_w // cb, q_scale=hd ** -0.5)
    return pl.pallas_call(
        body,
        grid=(bsz, ncb, ns),
        in_specs=[pl.BlockSpec((ts, cb), lambda b, c, s: (b * ns + s, c0 + c)),
                  pl.BlockSpec((conv_w.shape[0], cb), lambda b, c, s: (0, c))],
        out_specs=pl.BlockSpec((ts, cb), lambda b, c, s: (b * ns + s, c)),
        out_shape=jax.ShapeDtypeStruct((bsz * seq, 2 * qk_w + v_w), F32),
        scratch_shapes=[pltpu.VMEM((8, cb), F32)],
        compiler_params=_cp("parallel", "parallel", "arbitrary"),
        name="gdn_conv",
    )(proj, conv_w)


GDN_GROUP = 4
GDN_INV_BATCH = 8


def _gdn_gate_body(m_ref, alog_ref, dtb_ref, gc_ref, beta_ref, gct_ref, *, ts, chunk, nhv, ng):
    x = m_ref[...]
    z = x + dtb_ref[...]
    softplus = jnp.maximum(z, 0.0) + jnp.log(1.0 + jnp.exp(-jnp.abs(z)))
    g = -jnp.exp(alog_ref[...]) * softplus
    pos = lax.broadcasted_iota(I32, (ts, LANES), 0) % chunk
    d = 1
    while d < chunk:
        g = g + jnp.where(pos >= d, pltpu.roll(g, d, axis=0), 0.0)
        d *= 2
    gct_ref[...] = g.T
    beta = jax.nn.sigmoid(x)
    for j in range(nhv // ng):
        gc_ref[:, j * LANES:(j + 1) * LANES] = pltpu.roll(g, (LANES - j * ng) % LANES, axis=1)
        beta_ref[:, j * LANES:(j + 1) * LANES] = pltpu.roll(beta, (2 * LANES - nhv - j * ng) % LANES, axis=1)


def _gdn_gates(proj, alog_p, dtb_p, bsz, seq, misc_blk, nhv):
    ts = min(seq, 512)
    ns = seq // ts
    ng = GDN_GROUP
    gw = nhv // ng * LANES
    body = functools.partial(_gdn_gate_body, ts=ts, chunk=GDN_CHUNK, nhv=nhv, ng=ng)
    return pl.pallas_call(
        body,
        grid=(bsz, ns),
        in_specs=[pl.BlockSpec((ts, LANES), lambda b, s: (b * ns + s, misc_blk)),
                  pl.BlockSpec((1, LANES), lambda b, s: (0, 0)),
                  pl.BlockSpec((1, LANES), lambda b, s: (0, 0))],
        out_specs=[pl.BlockSpec((ts, gw), lambda b, s: (b * ns + s, 0)),
                   pl.BlockSpec((ts, gw), lambda b, s: (b * ns + s, 0)),
                   pl.BlockSpec((None, LANES, ts), lambda b, s: (b, 0, s))],
        out_shape=[jax.ShapeDtypeStruct((bsz * seq, gw), F32),
                   jax.ShapeDtypeStruct((bsz * seq, gw), F32),
                   jax.ShapeDtypeStruct((bsz, LANES, seq), F32)],
        compiler_params=_cp("parallel", "parallel"),
        name="gdn_gates",
    )(proj, alog_p, dtb_p)


def _gdn_core_body(q_ref, k_ref, v_ref, gc_ref, beta_ref, gr_ref, z_ref, nw_ref, o_ref, *scratch,
                   ts, chunk, hd, ng, rep):
    st_refs = scratch[:ng]
    gcb_ref, bb_ref, a_ref, u_ref, w_ref, qkm_ref, rhs_ref, qd_ref, ktt_ref, egl_ref = scratch[ng:]
    nc = ts // chunk

    @pl.when(pl.program_id(2) == 0)
    def _():
        for st_ref in st_refs:
            st_ref[...] = jnp.zeros(st_ref.shape, F32)

    for g in range(ng):
        gcb_ref[g] = jnp.broadcast_to(gc_ref[:, g:g + 1], (ts, LANES))
        bb_ref[g] = jnp.broadcast_to(beta_ref[:, g:g + 1], (ts, LANES))

    ri = lax.broadcasted_iota(I32, (chunk, chunk), 0)
    ci = lax.broadcasted_iota(I32, (chunk, chunk), 1)
    incl = ri >= ci
    strict = ri > ci
    eye = jnp.where(ri == ci, 1.0, 0.0).astype(F32)

    def prep(c, _):
        rows = pl.ds(pl.multiple_of(c * chunk, chunk), chunk)
        for qh in range(ng // rep):
            q = q_ref[rows, qh * hd:(qh + 1) * hd]
            k = k_ref[rows, qh * hd:(qh + 1) * hd]
            k16 = k.astype(BF16)
            kk = _dot_nt(k16, k16)
            qk = _dot_nt(q.astype(BF16), k16)
            for vh in range(rep):
                g = qh * rep + vh
                n = c * ng + g
                gcb = gcb_ref[g, rows, :]
                bb = bb_ref[g, rows, :]
                decay = jnp.where(incl, jnp.exp(gcb[:, :chunk] - gr_ref[g, pl.ds(c, 1), :]), 0.0)
                a_ref[n] = jnp.where(strict, kk * bb[:, :chunk] * decay, 0.0)
                qkm_ref[n] = (qk * decay).astype(BF16)
                eg = jnp.exp(gcb)
                rhs_ref[n, :, 0:hd] = (v_ref[rows, g * hd:(g + 1) * hd] * bb).astype(BF16)
                rhs_ref[n, :, hd:2 * hd] = (k * bb * eg).astype(BF16)
                qd_ref[n] = (q * eg).astype(BF16)
                g_last = gcb[chunk - 1:chunk, :]
                ktt_ref[n] = (k * jnp.exp(g_last - gcb)).T.astype(BF16)
                egl_ref[n] = jnp.broadcast_to(jnp.exp(g_last), (8, LANES))
        return 0

    lax.fori_loop(0, nc, prep, 0)

    n_dbl = max(1, (chunk - 1).bit_length()) - 1

    def invert(ib, _):
        ns_ = [ib * GDN_INV_BATCH + j for j in range(GDN_INV_BATCH)]
        pw = [a_ref[n] for n in ns_]
        tm = [eye - a for a in pw]
        for _ in range(n_dbl):
            pw16 = [x.astype(BF16) for x in pw]
            pw = [_dot(x, x) for x in pw16]
            tm = [t + _dot(t.astype(BF16), x.astype(BF16)) for t, x in zip(tm, pw)]
        uw = [_dot(t.astype(BF16), rhs_ref[n]) for n, t in zip(ns_, tm)]
        for n, x in zip(ns_, uw):
            u_ref[n] = x[:, :hd]
            w_ref[n] = x[:, hd:].astype(BF16)
        return 0

    lax.fori_loop(0, nc * ng // GDN_INV_BATCH, invert, 0)

    nw = nw_ref[...]

    def step(c, _):
        rows = pl.ds(pl.multiple_of(c * chunk, chunk), chunk)
        heads = range(ng)
        ns_ = [c * ng + g for g in heads]
        state = [st_refs[g][...] for g in heads]
        st16 = [x.astype(BF16) for x in state]
        ws = [_dot(w_ref[n], x) for n, x in zip(ns_, st16)]
        qs = [_dot(qd_ref[n], x) for n, x in zip(ns_, st16)]
        vn16 = [(u_ref[n] - x).astype(BF16) for n, x in zip(ns_, ws)]
        o = [y + _dot(qkm_ref[n], x) for n, x, y in zip(ns_, vn16, qs)]
        ds = [_dot(ktt_ref[n], x) for n, x in zip(ns_, vn16)]
        for g in heads:
            st_refs[g][...] = state[g] * egl_ref[ns_[g]][0:1, :] + ds[g]
            zz = z_ref[rows, g * hd:(g + 1) * hd]
            o_ref[rows, g * hd:(g + 1) * hd] = (_rms(o[g], nw) * (zz * jax.nn.sigmoid(zz))).astype(o_ref.dtype)
        return 0

    lax.fori_loop(0, nc, step, 0)


def _gdn_core(qkv, gc, beta, gct4, proj, norm_w, bsz, seq, nqk, nhv, hd, z_col):
    assert hd == LANES
    chunk = GDN_CHUNK
    ng = GDN_GROUP
    rep = nhv // nqk
    ts = min(seq, 1024)
    ns = seq // ts
    n = (ts // chunk) * ng
    assert ng % rep == 0 and nhv % ng == 0 and n % GDN_INV_BATCH == 0
    qw, vw = (ng // rep) * hd, ng * hd
    body = functools.partial(_gdn_core_body, ts=ts, chunk=chunk, hd=hd, ng=ng, rep=rep)
    row = lambda b, h, s: b * ns + s
    return pl.pallas_call(
        body,
        grid=(bsz, nhv // ng, ns),
        in_specs=[pl.BlockSpec((ts, qw), lambda b, h, s: (row(b, h, s), h)),
                  pl.BlockSpec((ts, qw), lambda b, h, s: (row(b, h, s), nqk * hd // qw + h)),
                  pl.BlockSpec((ts, vw), lambda b, h, s: (row(b, h, s), 2 * nqk * hd // vw + h)),
                  pl.BlockSpec((ts, LANES), lambda b, h, s: (row(b, h, s), h)),
                  pl.BlockSpec((ts, LANES), lambda b, h, s: (row(b, h, s), h)),
                  pl.BlockSpec((None, ng, ts // chunk, chunk), lambda b, h, s: (b, h, s, 0)),
                  pl.BlockSpec((ts, vw), lambda b, h, s: (row(b, h, s), z_col // vw + h)),
                  pl.BlockSpec((1, hd), lambda b, h, s: (0, 0))],
        out_specs=pl.BlockSpec((ts, vw), lambda b, h, s: (row(b, h, s), h)),
        out_shape=jax.ShapeDtypeStruct((bsz * seq, nhv * hd), BF16),
        scratch_shapes=[pltpu.VMEM((hd, hd), F32) for _ in range(ng)] + [
                        pltpu.VMEM((ng, ts, LANES), F32),
                        pltpu.VMEM((ng, ts, LANES), F32),
                        pltpu.VMEM((n, chunk, chunk), F32),
                        pltpu.VMEM((n, chunk, hd), F32),
                        pltpu.VMEM((n, chunk, hd), BF16),
                        pltpu.VMEM((n, chunk, chunk), BF16),
                        pltpu.VMEM((n, chunk, 2 * hd), BF16),
                        pltpu.VMEM((n, chunk, hd), BF16),
                        pltpu.VMEM((n, hd, chunk), BF16),
                        pltpu.VMEM((n, 8, LANES), F32)],
        compiler_params=_cp("parallel", "parallel", "arbitrary"),
        name="gdn_core",
    )(qkv, qkv, qkv, gc, beta, gct4, proj, norm_w.reshape(1, hd))


def _merge_body(a_ref, b_ref, wa_ref, wb_ref, ga_ref, gb_ref, o_ref):
    ya = _dot(a_ref[...], wa_ref[...])
    yb = _dot(b_ref[...], wb_ref[...])
    o_ref[...] = (jax.nn.sigmoid(ga_ref[...]) * ya + jax.nn.sigmoid(gb_ref[...]) * yb).astype(o_ref.dtype)


def _merge(o_a, o_b, wa16, wb16, proj, ga_col0, gb_col0):
    t, ka = o_a.shape
    kb = o_b.shape[1]
    d = wa16.shape[1]
    tm = min(t, 512)
    tn = _tile(d, 512)
    ga0, gb0 = ga_col0 // tn, gb_col0 // tn
    return pl.pallas_call(
        _merge_body,
        grid=(t // tm, d // tn),
        in_specs=[pl.BlockSpec((tm, ka), lambda i, j: (i, 0)),
                  pl.BlockSpec((tm, kb), lambda i, j: (i, 0)),
                  pl.BlockSpec((ka, tn), lambda i, j: (0, j)),
                  pl.BlockSpec((kb, tn), lambda i, j: (0, j)),
                  pl.BlockSpec((tm, tn), lambda i, j: (i, ga0 + j)),
                  pl.BlockSpec((tm, tn), lambda i, j: (i, gb0 + j))],
        out_specs=pl.BlockSpec((tm, tn), lambda i, j: (i, j)),
        out_shape=jax.ShapeDtypeStruct((t, d), BF16),
        compiler_params=_cp("parallel", "parallel"),
        name="merge",
    )(o_a, o_b, wa16, wb16, proj, proj)


def _outproj_router_body(m_ref, x_ref, wo_ref, g1_ref, nw_ref, sc_ref, sh_ref, wr_ref, br_ref,
                         x1_ref, h2_ref, te_ref, tg_ref, *, topk):
    x1 = x_ref[...] + g1_ref[...] * _dot(m_ref[...], wo_ref[...])
    x1_ref[...] = x1
    h2 = _rms(x1, nw_ref[...]) * (1.0 + sc_ref[...]) + sh_ref[...]
    h2_ref[...] = h2
    logits = _dot(h2.astype(BF16), wr_ref[...]) + br_ref[...]
    lane = lax.broadcasted_iota(I32, logits.shape, 1)
    te = jnp.zeros(logits.shape, I32)
    tg = jnp.zeros(logits.shape, F32)
    denom = jnp.zeros((logits.shape[0], 1), F32)
    m0 = None
    for kk in range(topk):
        mx = jnp.max(logits, axis=-1, keepdims=True)
        idx = jnp.min(jnp.where(logits == mx, lane, LANES), axis=-1, keepdims=True)
        if kk == 0:
            m0 = mx
        e = jnp.exp(mx - m0)
        denom = denom + e
        te = jnp.where(lane == kk, idx, te)
        tg = jnp.where(lane == kk, e, tg)
        logits = jnp.where(lane == idx, -jnp.inf, logits)
    te_ref[...] = te
    tg_ref[...] = tg / denom


def _outproj_router(merged, x2, wo16, gate1, nw, scale, shift, wr16, br_p, seq):
    t, d = x2.shape
    tm = min(seq, 256)
    per_b = seq // tm
    vec = lambda: pl.BlockSpec((None, 1, d), lambda i: (i // per_b, 0, 0))
    return pl.pallas_call(
        functools.partial(_outproj_router_body, topk=TOP_K),
        grid=(t // tm,),
        in_specs=[pl.BlockSpec((tm, d), lambda i: (i, 0)),
                  pl.BlockSpec((tm, d), lambda i: (i, 0)),
                  pl.BlockSpec((d, d), lambda i: (0, 0)),
                  vec(),
                  pl.BlockSpec((1, d), lambda i: (0, 0)),
                  vec(), vec(),
                  pl.BlockSpec((d, LANES), lambda i: (0, 0)),
                  pl.BlockSpec((1, LANES), lambda i: (0, 0))],
        out_specs=[pl.BlockSpec((tm, d), lambda i: (i, 0)),
                   pl.BlockSpec((tm, d), lambda i: (i, 0)),
                   pl.BlockSpec((tm, LANES), lambda i: (i, 0)),
                   pl.BlockSpec((tm, LANES), lambda i: (i, 0))],
        out_shape=[jax.ShapeDtypeStruct((t, d), F32),
                   jax.ShapeDtypeStruct((t, d), F32),
                   jax.ShapeDtypeStruct((t, LANES), I32),
                   jax.ShapeDtypeStruct((t, LANES), F32)],
        compiler_params=_cp("parallel"),
        name="outproj_router",
    )(merged, x2, wo16, gate1[:, None, :], nw.reshape(1, d), scale[:, None, :], shift[:, None, :], wr16, br_p)


def _moe_rank_body(te_ref, pos_ref, cnt_ref, carry_ref, *, tt, topk):
    @pl.when(pl.program_id(0) == 0)
    def _():
        carry_ref[...] = jnp.zeros(carry_ref.shape, F32)

    te = te_ref[...]
    lane = lax.broadcasted_iota(I32, (tt, LANES), 1)
    onehot = jnp.zeros((tt, LANES), F32)
    for kk in range(topk):
        onehot = onehot + jnp.where(lane == te[:, kk:kk + 1], 1.0, 0.0)
    r = lax.broadcasted_iota(I32, (tt, tt), 0)
    c = lax.broadcasted_iota(I32, (tt, tt), 1)
    below = jnp.where(r > c, 1.0, 0.0).astype(BF16)
    rank = _dot(below, onehot.astype(BF16)) + carry_ref[0:1, :]
    pos = jnp.zeros((tt, LANES), I32)
    for kk in range(topk):
        p = jnp.sum(jnp.where(lane == te[:, kk:kk + 1], rank, 0.0), axis=-1, keepdims=True)
        pos = jnp.where(lane == kk, p.astype(I32), pos)
    pos_ref[...] = pos
    carry_ref[...] = carry_ref[...] + jnp.sum(onehot, axis=0, keepdims=True)
    cnt_ref[...] = carry_ref[...]


def _moe_rank(te):
    t = te.shape[0]
    tt = min(t, 512)
    return pl.pallas_call(
        functools.partial(_moe_rank_body, tt=tt, topk=TOP_K),
        grid=(t // tt,),
        in_specs=[pl.BlockSpec((tt, LANES), lambda i: (i, 0))],
        out_specs=[pl.BlockSpec((tt, LANES), lambda i: (i, 0)),
                   pl.BlockSpec((8, LANES), lambda i: (0, 0))],
        out_shape=[jax.ShapeDtypeStruct((t, LANES), I32),
                   jax.ShapeDtypeStruct((8, LANES), F32)],
        scratch_shapes=[pltpu.VMEM((8, LANES), F32)],
        compiler_params=_cp("arbitrary"),
        name="moe_rank",
    )(te)


def _moe_dispatch_body(dest_ref, h_ref, zero_ref, x_hbm, sem, *, tt, topk):
    del zero_ref
    base = pl.program_id(0) * tt * topk

    def issue(t, _):
        for kk in range(topk):
            dst = x_hbm.at[pl.ds(dest_ref[base + t * topk + kk], 1)]
            pltpu.make_async_copy(h_ref.at[pl.ds(t, 1)], dst, sem).start()
        return 0

    lax.fori_loop(0, tt, issue, 0, unroll=2)
    for _ in range(topk):
        pltpu.make_async_copy(h_ref, x_hbm.at[pl.ds(0, tt)], sem).wait()


def _moe_dispatch(dest_flat, h2, n_rows):
    t, d = h2.shape
    tt = min(t, 128)
    grid_spec = pltpu.PrefetchScalarGridSpec(
        num_scalar_prefetch=1,
        grid=(t // tt,),
        in_specs=[pl.BlockSpec((tt, d), lambda i, dest: (i, 0)),
                  pl.BlockSpec(memory_space=pl.ANY)],
        out_specs=pl.BlockSpec(memory_space=pl.ANY),
        scratch_shapes=[pltpu.SemaphoreType.DMA(())])
    return pl.pallas_call(
        functools.partial(_moe_dispatch_body, tt=tt, topk=TOP_K),
        grid_spec=grid_spec,
        out_shape=jax.ShapeDtypeStruct((n_rows, d), F32),
        input_output_aliases={2: 0},
        compiler_params=_cp("arbitrary"),
        name="moe_dispatch",
    )(dest_flat, h2, jnp.zeros((n_rows, d), F32))


def _moe_gu_body(be_ref, na_ref, x_ref, wg_ref, wu_ref, bg_ref, bu_ref, act_ref, wg16_ref, wu16_ref):
    i = pl.program_id(1)
    prev = be_ref[jnp.maximum(i - 1, 0)]

    @pl.when((i == 0) | (be_ref[i] != prev))
    def _():
        wg16_ref[...] = wg_ref[...].astype(BF16)
        wu16_ref[...] = wu_ref[...].astype(BF16)

    @pl.when(i < na_ref[0])
    def _():
        x = x_ref[...].astype(BF16)
        g = jnp.minimum(_dot(x, wg16_ref[...]) + bg_ref[...], SWIGLU_LIMIT)
        u = jnp.clip(_dot(x, wu16_ref[...]) + bu_ref[...], -SWIGLU_LIMIT, SWIGLU_LIMIT)
        act_ref[...] = ((u + 1.0) * (g * jax.nn.sigmoid(SWIGLU_ALPHA * g))).astype(act_ref.dtype)

    @pl.when(i >= na_ref[0])
    def _():
        act_ref[...] = jnp.zeros(act_ref.shape, act_ref.dtype)


def _moe_gu(block_e, n_act, x_rows, w_gu, b_gu):
    n_rows, d = x_rows.shape
    ne, _, ff2 = w_gu.shape
    ff = ff2 // 2
    bm = MOE_BM
    tn = _tile(ff, 512)
    nt = ff // tn
    rowblk = lambda j, i, be, na: (jnp.minimum(i, na[0] - 1), 0)
    grid_spec = pltpu.PrefetchScalarGridSpec(
        num_scalar_prefetch=2,
        grid=(nt, n_rows // bm),
        in_specs=[pl.BlockSpec((bm, d), rowblk),
                  pl.BlockSpec((None, d, tn), lambda j, i, be, na: (be[i], 0, j)),
                  pl.BlockSpec((None, d, tn), lambda j, i, be, na: (be[i], 0, nt + j)),
                  pl.BlockSpec((None, 1, tn), lambda j, i, be, na: (be[i], 0, j)),
                  pl.BlockSpec((None, 1, tn), lambda j, i, be, na: (be[i], 0, nt + j))],
        out_specs=pl.BlockSpec((bm, tn), lambda j, i, be, na: (i, j)),
        scratch_shapes=[pltpu.VMEM((d, tn), BF16), pltpu.VMEM((d, tn), BF16)])
    return pl.pallas_call(
        _moe_gu_body,
        grid_spec=grid_spec,
        out_shape=jax.ShapeDtypeStruct((n_rows, ff), BF16),
        compiler_params=_cp("arbitrary", "arbitrary"),
        name="moe_gu",
    )(block_e, n_act, x_rows, w_gu, w_gu, b_gu.reshape(ne, 1, ff2), b_gu.reshape(ne, 1, ff2))


def _moe_down_body(be_ref, na_ref, a_ref, w_ref, b_ref, y_ref, w16_ref):
    i = pl.program_id(1)
    prev = be_ref[jnp.maximum(i - 1, 0)]

    @pl.when((i == 0) | (be_ref[i] != prev))
    def _():
        w16_ref[...] = w_ref[...].astype(BF16)

    @pl.when(i < na_ref[0])
    def _():
        y_ref[...] = _dot(a_ref[...], w16_ref[...]) + b_ref[...]

    @pl.when(i >= na_ref[0])
    def _():
        y_ref[...] = jnp.zeros(y_ref.shape, y_ref.dtype)


def _moe_down(block_e, n_act, act, w_down, b_down):
    n_rows, ff = act.shape
    ne, _, d = w_down.shape
    bm = MOE_BM
    tn = _tile(d, 1024)
    grid_spec = pltpu.PrefetchScalarGridSpec(
        num_scalar_prefetch=2,
        grid=(d // tn, n_rows // bm),
        in_specs=[pl.BlockSpec((bm, ff), lambda j, i, be, na: (jnp.minimum(i, na[0] - 1), 0)),
                  pl.BlockSpec((None, ff, tn), lambda j, i, be, na: (be[i], 0, j)),
                  pl.BlockSpec((None, 1, tn), lambda j, i, be, na: (be[i], 0, j))],
        out_specs=pl.BlockSpec((bm, tn), lambda j, i, be, na: (i, j)),
        scratch_shapes=[pltpu.VMEM((ff, tn), BF16)])
    return pl.pallas_call(
        _moe_down_body,
        grid_spec=grid_spec,
        out_shape=jax.ShapeDtypeStruct((n_rows, d), F32),
        compiler_params=_cp("arbitrary", "arbitrary"),
        name="moe_down",
    )(block_e, n_act, act, w_down, b_down.reshape(ne, 1, d))


def _moe_combine_body(dest_ref, x1_ref, tg_ref, g2_ref, fw_ref, y_hbm, o_ref, buf_ref, sem, *, tt, topk):
    base = pl.program_id(0) * tt * topk

    def issue(t, _):
        for kk in range(topk):
            src = y_hbm.at[pl.ds(dest_ref[base + t * topk + kk], 1)]
            pltpu.make_async_copy(src, buf_ref.at[kk, pl.ds(t, 1)], sem).start()
        return 0

    lax.fori_loop(0, tt, issue, 0, unroll=2)
    for kk in range(topk):
        pltpu.make_async_copy(y_hbm.at[pl.ds(0, tt)], buf_ref.at[kk], sem).wait()
    tg = tg_ref[...]
    y = tg[:, 0:1] * buf_ref[0]
    for kk in range(1, topk):
        y = y + tg[:, kk:kk + 1] * buf_ref[kk]
    o_ref[...] = _rms(x1_ref[...] + g2_ref[...] * y, fw_ref[...])


def _moe_combine(dest_flat, x1, tg, gate2, final_w, y_rows, seq):
    t, d = x1.shape
    tt = min(seq, 128)
    per_b = seq // tt
    grid_spec = pltpu.PrefetchScalarGridSpec(
        num_scalar_prefetch=1,
        grid=(t // tt,),
        in_specs=[pl.BlockSpec((tt, d), lambda i, dest: (i, 0)),
                  pl.BlockSpec((tt, LANES), lambda i, dest: (i, 0)),
                  pl.BlockSpec((None, 1, d), lambda i, dest: (i // per_b, 0, 0)),
                  pl.BlockSpec((1, d), lambda i, dest: (0, 0)),
                  pl.BlockSpec(memory_space=pl.ANY)],
        out_specs=pl.BlockSpec((tt, d), lambda i, dest: (i, 0)),
        scratch_shapes=[pltpu.VMEM((TOP_K, tt, d), F32), pltpu.SemaphoreType.DMA(())])
    return pl.pallas_call(
        functools.partial(_moe_combine_body, tt=tt, topk=TOP_K),
        grid_spec=grid_spec,
        out_shape=jax.ShapeDtypeStruct((t, d), F32),
        compiler_params=_cp("arbitrary"),
        name="moe_combine",
    )(dest_flat, x1, tg, gate2[:, None, :], final_w.reshape(1, d), y_rows)


def _pad_lanes(v, off=0, fill=0.0):
    out = jnp.full((1, LANES), fill, F32)
    return out.at[0, off:off + v.shape[0]].set(v.astype(F32))


def _layer(x2, c, bsz, seq, p, final_w):
    d = x2.shape[1]
    qr, nh, dh = p["w_uq"].shape
    kvr = p["w_uk"].shape[0]
    nhi, di = p["w_uqi"].shape[1:]
    hd = p["gdn_norm_w"].shape[0]
    nhv = p["a_log"].shape[0]
    v_w = nhv * hd
    qk_w = (p["conv_w"].shape[1] - v_w) // 2
    nqk = qk_w // hd
    ne = p["w_router"].shape[1]

    mod = _ada(c, p["ada_w"], p["ada_b"])
    shift1, scale1, gate1, shift2, scale2, gate2 = jnp.split(mod, 6, axis=-1)

    widths = (qr, kvr, di, nhi, qk_w, qk_w, v_w, nhv, nhv, v_w, d, d)
    offs = [0]
    for wd in widths:
        offs.append(offs[-1] + wd)
    col = lambda k: p["w_in"][:, offs[k]:offs[k + 1]]
    pad = jnp.zeros((d, LANES - 2 * nhv - nhi), F32)
    w_in = jnp.concatenate([col(0), col(1), col(2), col(7), col(8), col(3), pad,
                            col(4), col(5), col(6), col(9), col(10), col(11)], axis=1).astype(BF16)
    lat_w = qr + kvr + di + LANES
    gq_col = lat_w
    z_col = gq_col + 2 * qk_w + v_w
    ga_col = z_col + v_w
    gb_col = ga_col + d

    proj = _inproj(x2, p["norm1_w"], scale1, shift1, w_in, seq)

    wuq_t = p["w_uq"].reshape(qr, nh * dh).T.astype(BF16)
    wuqi_t = p["w_uqi"].reshape(qr, nhi * di).T.astype(BF16)
    wuk_r = jnp.transpose(p["w_uk"], (1, 0, 2)).astype(BF16)
    wuv_t = jnp.transpose(p["w_uv"], (1, 2, 0)).astype(BF16)
    qabs_t, qi_t, ckv, ckvt, ki, wi_t = _dsa_prep(
        proj, bsz, seq, p["q_lat_norm_w"], p["kv_lat_norm_w"], p["idx_k_norm_w"], wuq_t, wuqi_t, wuk_r,
        (qr, kvr, di, nh, dh, nhi, 2 * nhv))
    o_a = _dsa_attn(ki, ckv, ckvt, qi_t, qabs_t, wi_t, wuv_t).reshape(bsz * seq, nh * dh)

    qkv = _gdn_conv(proj, p["conv_w"], bsz, seq, gq_col, qk_w, v_w, hd)
    gc, beta, gct = _gdn_gates(proj, _pad_lanes(p["a_log"]), _pad_lanes(p["dt_bias"]), bsz, seq,
                               (lat_w - LANES) // LANES, nhv)
    gct4 = gct.reshape(bsz, LANES, seq // GDN_CHUNK, GDN_CHUNK)
    o_b = _gdn_core(qkv, gc, beta, gct4, proj, p["gdn_norm_w"], bsz, seq, nqk, nhv, hd, z_col)

    merged = _merge(o_a, o_b, p["w_branch_a"].astype(BF16), p["w_branch_b"].astype(BF16), proj, ga_col, gb_col)

    wr16 = jnp.zeros((d, LANES), F32).at[:, :ne].set(p["w_router"]).astype(BF16)
    br_p = _pad_lanes(p["b_router"], fill=-1e30)
    x1, h2, te, tg = _outproj_router(merged, x2, p["w_out"].astype(BF16), gate1, p["norm2_w"], scale2, shift2,
                                     wr16, br_p, seq)

    t = bsz * seq
    pos, cnt = _moe_rank(te)
    counts = cnt[0, :ne].astype(I32)
    padded = (counts + MOE_BM - 1) // MOE_BM * MOE_BM
    pad_end = jnp.cumsum(padded)
    pad_start = pad_end - padded
    dest = (pad_start[te[:, :TOP_K]] + pos[:, :TOP_K]).reshape(-1).astype(I32)
    n_blocks = -(-(t * TOP_K) // MOE_BM) + ne
    n_act = (pad_end[-1] // MOE_BM).astype(I32).reshape(1)
    blk = jnp.minimum(jnp.arange(n_blocks, dtype=I32), n_act[0] - 1) * MOE_BM
    block_e = jnp.minimum(jnp.sum(pad_end[None, :] <= blk[:, None], axis=1), ne - 1).astype(I32)
    x_rows = _moe_dispatch(dest, h2, n_blocks * MOE_BM)
    act = _moe_gu(block_e, n_act, x_rows, p["w_gu"], p["b_gu"])
    y_rows = _moe_down(block_e, n_act, act, p["w_down"], p["b_down"])
    return _moe_combine(dest, x1, tg, gate2, final_w, y_rows, seq)


@jax.jit
def kernel(x, c, ada_w, ada_b, norm1_w, w_in, q_lat_norm_w, kv_lat_norm_w, idx_k_norm_w, w_uq, w_uqi, w_uk, w_uv, conv_w, a_log, dt_bias, gdn_norm_w, w_branch_a, w_branch_b, w_out, norm2_w, w_router, b_router, w_gu, b_gu, w_down, b_down, final_norm_w):
    bsz, seq, d = x.shape
    stacked = dict(ada_w=ada_w, ada_b=ada_b, norm1_w=norm1_w, w_in=w_in, q_lat_norm_w=q_lat_norm_w,
                   kv_lat_norm_w=kv_lat_norm_w, idx_k_norm_w=idx_k_norm_w, w_uq=w_uq, w_uqi=w_uqi, w_uk=w_uk,
                   w_uv=w_uv, conv_w=conv_w, a_log=a_log, dt_bias=dt_bias, gdn_norm_w=gdn_norm_w,
                   w_branch_a=w_branch_a, w_branch_b=w_branch_b, w_out=w_out, norm2_w=norm2_w,
                   w_router=w_router, b_router=b_router, w_gu=w_gu, b_gu=b_gu, w_down=w_down, b_down=b_down)
    depth = ada_w.shape[0]
    assert depth == 1, "the final norm is fused into the last layer's combine kernel"
    x2 = x.reshape(bsz * seq, d)
    p = {k: v[0] for k, v in stacked.items()}
    out = _layer(x2, c, bsz, seq, p, final_norm_w)
    return out.reshape(bsz, seq, d)
```

```python
import functools

import jax
import jax.numpy as jnp
from jax import lax
from jax.experimental import pallas as pl
from jax.experimental.pallas import tpu as pltpu

F32 = jnp.float32
BF16 = jnp.bfloat16
I32 = jnp.int32

EPS = 1e-6
LANES = 128
VMEM_LIMIT = 56 * 1024 * 1024

IDX_TOPK = 256
GDN_CHUNK = 64
TOP_K = 4
SWIGLU_LIMIT = 7.0
SWIGLU_ALPHA = 1.702
MOE_BM = 512
ONES_ROWS = 16
INT_MIN = -(2 ** 31)
INT_MAX = 2 ** 31 - 1


def _cp(*sem):
    return pltpu.CompilerParams(dimension_semantics=sem, vmem_limit_bytes=VMEM_LIMIT)


def _tile(n, pref):
    if n <= pref:
        return n
    t = pref - pref % LANES
    while n % t:
        t -= LANES
    return t


def _rms(x, w):
    return x * lax.rsqrt(jnp.mean(x * x, axis=-1, keepdims=True) + EPS) * w


def _dot(a, b):
    return jnp.dot(a, b, preferred_element_type=F32)


def _dot_nt(a, b):
    return lax.dot_general(a, b, (((1,), (1,)), ((), ())), preferred_element_type=F32)


def _dot_tn(a, b):
    return lax.dot_general(a, b, (((0,), (0,)), ((), ())), preferred_element_type=F32)


def _ada_body(ct_ref, w_ref, b_ref, o_ref, *, nb):
    ct = ct_ref[...]
    s = ct * jax.nn.sigmoid(ct)
    w = w_ref[...]
    for b in range(nb):
        o_ref[b:b + 1, :] = jnp.sum(w * s[:, b:b + 1], axis=0, keepdims=True) + b_ref[...]


def _ada(c, w, b):
    nb, d = c.shape
    n = w.shape[1]
    tn = _tile(n, 1024)
    return pl.pallas_call(
        functools.partial(_ada_body, nb=nb),
        grid=(n // tn,),
        in_specs=[pl.BlockSpec((d, nb), lambda j: (0, 0)),
                  pl.BlockSpec((d, tn), lambda j: (0, j)),
                  pl.BlockSpec((1, tn), lambda j: (0, j))],
        out_specs=pl.BlockSpec((nb, tn), lambda j: (0, j)),
        out_shape=jax.ShapeDtypeStruct((nb, n), F32),
        compiler_params=_cp("parallel"),
        name="ada",
    )(c.T, w, b.reshape(1, n))


def _inproj_body(x_ref, nw_ref, sc_ref, sh_ref, w_ref, o_ref, h_ref):
    @pl.when(pl.program_id(1) == 0)
    def _():
        h = _rms(x_ref[...], nw_ref[...]) * (1.0 + sc_ref[...]) + sh_ref[...]
        h_ref[...] = h.astype(BF16)

    o_ref[...] = _dot(h_ref[...], w_ref[...])


def _inproj(x2, nw, scale, shift, w16, seq):
    t, d = x2.shape
    n = w16.shape[1]
    tm = min(seq, 1024)
    tn = _tile(n, 512)
    per_b = seq // tm
    return pl.pallas_call(
        _inproj_body,
        grid=(t // tm, n // tn),
        in_specs=[pl.BlockSpec((tm, d), lambda i, j: (i, 0)),
                  pl.BlockSpec((1, d), lambda i, j: (0, 0)),
                  pl.BlockSpec((None, 1, d), lambda i, j: (i // per_b, 0, 0)),
                  pl.BlockSpec((None, 1, d), lambda i, j: (i // per_b, 0, 0)),
                  pl.BlockSpec((d, tn), lambda i, j: (0, j))],
        out_specs=pl.BlockSpec((tm, tn), lambda i, j: (i, j)),
        out_shape=jax.ShapeDtypeStruct((t, n), F32),
        scratch_shapes=[pltpu.VMEM((tm, d), BF16)],
        compiler_params=_cp("parallel", "arbitrary"),
        name="inproj",
    )(x2, nw.reshape(1, d), scale[:, None, :], shift[:, None, :], w16)


def _dsa_prep_body(lat_ref, qnw_ref, kvnw_ref, knw_ref, wuq_ref, wuqi_ref, wuk_ref,
                   qabs_ref, qi_ref, ckv_ref, ckvt_ref, ki_ref, wi_ref,
                   *, qr, kvr, di, nh, dh, nhi, q_scale, i_scale, w_scale, wi_off):
    lat = lat_ref[...]
    cq_t = _rms(lat[:, :qr], qnw_ref[...]).T.astype(BF16)
    q_t = _dot(wuq_ref[...], cq_t)
    for h in range(nh):
        q_h = q_t[h * dh:(h + 1) * dh].astype(BF16)
        qabs_ref[h] = (_dot(wuk_ref[h], q_h) * q_scale).astype(BF16)
    qi_t = _dot(wuqi_ref[...], cq_t) * i_scale
    for h in range(nhi):
        qi_ref[h] = qi_t[h * di:(h + 1) * di].astype(BF16)
    ckv = _rms(lat[:, qr:qr + kvr], kvnw_ref[...])
    ckv_ref[...] = ckv.astype(BF16)
    ckvt_ref[0:kvr, :] = ckv.T.astype(BF16)
    ckvt_ref[kvr:, :] = jnp.ones((ONES_ROWS, ckvt_ref.shape[1]), BF16)
    ki_ref[...] = _rms(lat[:, qr + kvr:qr + kvr + di], knw_ref[...]).astype(BF16)
    misc_t = lat[:, qr + kvr + di:].T
    wi_ref[...] = misc_t[wi_off:wi_off + nhi] * w_scale


def _dsa_prep(proj, bsz, seq, qnw, kvnw, knw, wuq_t, wuqi_t, wuk_r, dims):
    qr, kvr, di, nh, dh, nhi, wi_off = dims
    tm = min(seq, 512)
    ns = seq // tm
    lat_w = qr + kvr + di + LANES
    body = functools.partial(
        _dsa_prep_body, qr=qr, kvr=kvr, di=di, nh=nh, dh=dh, nhi=nhi,
        q_scale=dh ** -0.5, i_scale=di ** -0.5, w_scale=nhi ** -0.5, wi_off=wi_off)
    full = lambda shape: pl.BlockSpec(shape, lambda b, s: (0,) * len(shape))
    return pl.pallas_call(
        body,
        grid=(bsz, ns),
        in_specs=[pl.BlockSpec((tm, lat_w), lambda b, s: (b * ns + s, 0)),
                  full((1, qr)), full((1, kvr)), full((1, di)),
                  full((nh * dh, qr)), full((nhi * di, qr)), full((nh, kvr, dh))],
        out_specs=[pl.BlockSpec((None, nh, kvr, tm), lambda b, s: (b, 0, 0, s)),
                   pl.BlockSpec((None, nhi, di, tm), lambda b, s: (b, 0, 0, s)),
                   pl.BlockSpec((None, tm, kvr), lambda b, s: (b, s, 0)),
                   pl.BlockSpec((None, kvr + ONES_ROWS, tm), lambda b, s: (b, 0, s)),
                   pl.BlockSpec((None, tm, di), lambda b, s: (b, s, 0)),
                   pl.BlockSpec((None, nhi, tm), lambda b, s: (b, 0, s))],
        out_shape=[jax.ShapeDtypeStruct((bsz, nh, kvr, seq), BF16),
                   jax.ShapeDtypeStruct((bsz, nhi, di, seq), BF16),
                   jax.ShapeDtypeStruct((bsz, seq, kvr), BF16),
                   jax.ShapeDtypeStruct((bsz, kvr + ONES_ROWS, seq), BF16),
                   jax.ShapeDtypeStruct((bsz, seq, di), BF16),
                   jax.ShapeDtypeStruct((bsz, nhi, seq), F32)],
        compiler_params=_cp("parallel", "parallel"),
        name="dsa_prep",
    )(proj, qnw.reshape(1, qr), kvnw.reshape(1, kvr), knw.reshape(1, di), wuq_t, wuqi_t, wuk_r)


def _dsa_attn_body(ki_ref, ckv_ref, ckvt_ref, qi_ref, qabs_ref, wi_ref, wuv_ref, o_ref,
                   keys_ref, acc_ref, m_ref, ot_ref, thr_ref, jlim_ref,
                   *, tq, tk, nh, nhi, dh, kvr, topk, seq_bits):
    qb = pl.program_id(1)
    nk = ((qb + 1) * tq + tk - 1) // tk
    q_pos = qb * tq + lax.broadcasted_iota(I32, (tk, tq), 1)
    k_iota = lax.broadcasted_iota(I32, (tk, tq), 0)

    def score_tile(kt, _):
        ki_t = ki_ref[pl.ds(pl.multiple_of(kt * tk, tk), tk), :]
        acc = jnp.zeros((tk, tq), F32)
        for h in range(nhi):
            rel = _dot(ki_t, qi_ref[h])
            acc = acc + wi_ref[h:h + 1, :] * jnp.maximum(rel, 0.0)
        bits = lax.bitcast_convert_type(acc, I32)
        key = jnp.where(bits < 0, bits ^ INT_MAX, bits)
        key = jnp.where(kt * tk + k_iota <= q_pos, key, INT_MIN)
        keys_ref[pl.ds(pl.multiple_of(kt * tk, tk), tk), :] = key
        return 0

    lax.fori_loop(0, nk, score_tile, 0)

    def count(pred_fn):
        def tile(kt, c):
            k_t = keys_ref[pl.ds(pl.multiple_of(kt * tk, tk), tk), :]
            hit = jnp.where(pred_fn(k_t, kt), 1, 0).astype(I32)
            return c + jnp.sum(hit.reshape(tk // 8, 8, tq), axis=0)
        c8 = lax.fori_loop(0, nk, tile, jnp.zeros((8, tq), I32))
        return jnp.sum(c8, axis=0, keepdims=True)

    def bit_cond(carry):
        it, _, settled = carry
        return (it < 32) & (jnp.min(settled) == 0)

    def bit_step(carry):
        it, thr, settled = carry
        for _ in range(4):
            cand = thr ^ lax.shift_left(jnp.int32(1), 31 - it)
            cnt = count(lambda k_t, kt: k_t >= cand)
            take = (cnt >= topk) & (settled == 0)
            settled = jnp.where(take & (cnt == topk), 1, settled)
            thr = jnp.where(take, cand, thr)
            it = it + 1
        return it, thr, settled

    _, thr, _ = lax.while_loop(
        bit_cond, bit_step, (jnp.int32(0), jnp.full((1, tq), INT_MIN, I32), jnp.zeros((1, tq), I32)))
    thr = jnp.maximum(thr, INT_MIN + 1)
    cnt_gt = count(lambda k_t, kt: k_t > thr)
    cnt_ge = count(lambda k_t, kt: k_t >= thr)
    thr_ref[...] = thr
    jlim_ref[...] = jnp.full((1, tq), INT_MAX, I32)

    @pl.when(jnp.max(cnt_ge) > topk)
    def _():
        need = topk - cnt_gt

        def pos_step(it, p):
            cand = p + lax.shift_left(jnp.int32(1), seq_bits - 1 - it)
            cnt = count(lambda k_t, kt: (k_t == thr) & (kt * tk + k_iota < cand))
            return jnp.where(cnt < need, cand, p)

        jlim_ref[...] = lax.fori_loop(0, seq_bits, pos_step, jnp.zeros((1, tq), I32))

    m_ref[...] = jnp.full(m_ref.shape, -jnp.inf, F32)
    acc_ref[...] = jnp.zeros(acc_ref.shape, F32)

    @pl.when(nk % 2 == 1)
    def _():
        keys_ref[pl.ds(pl.multiple_of(nk * tk, tk), tk), :] = jnp.full((tk, tq), INT_MIN, I32)

    ta = 2 * tk
    a_iota = lax.broadcasted_iota(I32, (ta, tq), 0)

    def attn_tile(kt, _):
        off = pl.multiple_of(kt * ta, ta)
        k_t = keys_ref[pl.ds(off, ta), :]
        thr_b = thr_ref[...]
        mask = (k_t > thr_b) | ((k_t == thr_b) & (kt * ta + a_iota <= jlim_ref[...]))
        ckv_t = ckv_ref[pl.ds(off, ta), :]
        ckvt_t = ckvt_ref[:, pl.ds(off, ta)]
        for h in range(nh):
            s = jnp.where(mask, _dot(ckv_t, qabs_ref[h]), -jnp.inf)
            m_old = m_ref[h]
            m_new = jnp.maximum(m_old, jnp.max(s, axis=0, keepdims=True))
            m_use = jnp.where(m_new == -jnp.inf, 0.0, m_new)
            p = jnp.exp(s - m_use)
            alpha = jnp.exp(m_old - m_use)
            acc_ref[h] = alpha * acc_ref[h] + _dot(ckvt_t, p.astype(BF16))
            m_ref[h] = m_new
        return 0

    lax.fori_loop(0, (nk + 1) // 2, attn_tile, 0)

    for h in range(nh):
        acc = acc_ref[h]
        o_lat = (acc[:kvr] / acc[kvr:kvr + 1]).astype(BF16)
        ot_ref[h * dh:(h + 1) * dh, :] = _dot(wuv_ref[h], o_lat)
    o_ref[...] = ot_ref[...].T.astype(BF16)


def _dsa_attn(ki, ckv, ckvt, qi_t, qabs_t, wi_t, wuv_t):
    bsz, seq, di = ki.shape
    kvr = ckv.shape[2]
    nh, dh = wuv_t.shape[0], wuv_t.shape[1]
    nhi = qi_t.shape[1]
    tq = min(seq // 2, 256)
    tk = tq
    assert seq % (2 * tk) == 0
    topk = min(IDX_TOPK, seq // 4)
    body = functools.partial(_dsa_attn_body, tq=tq, tk=tk, nh=nh, nhi=nhi, dh=dh, kvr=kvr, topk=topk,
                             seq_bits=max(1, (seq - 1).bit_length()))
    return pl.pallas_call(
        body,
        grid=(bsz, seq // tq),
        in_specs=[pl.BlockSpec((None, seq, di), lambda b, q: (b, 0, 0)),
                  pl.BlockSpec((None, seq, kvr), lambda b, q: (b, 0, 0)),
                  pl.BlockSpec((None, kvr + ONES_ROWS, seq), lambda b, q: (b, 0, 0)),
                  pl.BlockSpec((None, nhi, di, tq), lambda b, q: (b, 0, 0, q)),
                  pl.BlockSpec((None, nh, kvr, tq), lambda b, q: (b, 0, 0, q)),
                  pl.BlockSpec((None, nhi, tq), lambda b, q: (b, 0, q)),
                  pl.BlockSpec((nh, dh, kvr), lambda b, q: (0, 0, 0))],
        out_specs=pl.BlockSpec((None, tq, nh * dh), lambda b, q: (b, q, 0)),
        out_shape=jax.ShapeDtypeStruct((bsz, seq, nh * dh), BF16),
        scratch_shapes=[pltpu.VMEM((seq, tq), I32),
                        pltpu.VMEM((nh, kvr + ONES_ROWS, tq), F32),
                        pltpu.VMEM((nh, 1, tq), F32),
                        pltpu.VMEM((nh * dh, tq), F32),
                        pltpu.VMEM((1, tq), I32),
                        pltpu.VMEM((1, tq), I32)],
        compiler_params=_cp("parallel", "parallel"),
        name="dsa_attn",
    )(ki, ckv, ckvt, qi_t, qabs_t, wi_t, wuv_t)


def _gdn_conv_body(u_ref, w_ref, o_ref, carry_ref, *, ts, cb, hd, nq_blocks, q_scale):
    cblk = pl.program_id(1)

    @pl.when(pl.program_id(2) == 0)
    def _():
        carry_ref[...] = jnp.zeros(carry_ref.shape, F32)

    u = u_ref[...]
    w = w_ref[...]
    taps = w.shape[0]
    prev = carry_ref[...]
    u8 = u[0:8, :]
    row8 = lax.broadcasted_iota(I32, (8, cb), 0)
    y = u * w[taps - 1:taps, :]
    y8 = u8 * w[taps - 1:taps, :]
    for j in range(1, taps):
        wj = w[taps - 1 - j:taps - j, :]
        y = y + pltpu.roll(u, j, axis=0) * wj
        y8 = y8 + jnp.where(row8 < j, pltpu.roll(prev, j, axis=0), pltpu.roll(u8, j, axis=0)) * wj
    carry_ref[...] = u[ts - 8:, :]

    def finish(yy, rows, mode):
        yy = yy * jax.nn.sigmoid(yy)
        if mode is None:
            o_ref[rows, :] = yy
            return
        for hh in range(cb // hd):
            x = yy[:, hh * hd:(hh + 1) * hd]
            inv = lax.rsqrt(jnp.sum(x * x, axis=-1, keepdims=True) + EPS)
            o_ref[rows, hh * hd:(hh + 1) * hd] = x * (inv * mode)

    def emit(mode):
        finish(y, slice(0, ts), mode)
        finish(y8, slice(0, 8), mode)

    @pl.when(cblk < nq_blocks)
    def _():
        emit(q_scale)

    @pl.when((cblk >= nq_blocks) & (cblk < 2 * nq_blocks))
    def _():
        emit(1.0)

    @pl.when(cblk >= 2 * nq_blocks)
    def _():
        emit(None)


def _gdn_conv(proj, conv_w, bsz, seq, col0, qk_w, v_w, hd):
    ts = min(seq, 512)
    cb = min(qk_w, 1024)
    ns = seq // ts
    ncb = (2 * qk_w + v_w) // cb
    c0 = col0 // cb
    body = functools.partial(_gdn_conv_body, ts=ts, cb=cb, hd=hd, nq_blocks=qk_w // cb, q_scale=hd ** -0.5)
    return pl.pallas_call(
        body,
        grid=(bsz, ncb, ns),
        in_specs=[pl.BlockSpec((ts, cb), lambda b, c, s: (b * ns + s, c0 + c)),
                  pl.BlockSpec((conv_w.shape[0], cb), lambda b, c, s: (0, c))],
        out_specs=pl.BlockSpec((ts, cb), lambda b, c, s: (b * ns + s, c)),
        out_shape=jax.ShapeDtypeStruct((bsz * seq, 2 * qk_w + v_w), F32),
        scratch_shapes=[pltpu.VMEM((8, cb), F32)],
        compiler_params=_cp("parallel", "parallel", "arbitrary"),
        name="gdn_conv",
    )(proj, conv_w)


GDN_GROUP = 8
GDN_INV_BATCH = 16


def _gdn_gate_body(m_ref, alog_ref, dtb_ref, gc_ref, beta_ref, gct_ref, *, ts, chunk, nhv, ng):
    x = m_ref[...]
    z = x + dtb_ref[...]
    softplus = jnp.maximum(z, 0.0) + jnp.log(1.0 + jnp.exp(-jnp.abs(z)))
    g = -jnp.exp(alog_ref[...]) * softplus
    pos = lax.broadcasted_iota(I32, (ts, LANES), 0) % chunk
    d = 1
    while d < chunk:
        g = g + jnp.where(pos >= d, pltpu.roll(g, d, axis=0), 0.0)
        d *= 2
    gct_ref[...] = g.T
    beta = jax.nn.sigmoid(x)
    for j in range(nhv // ng):
        gc_ref[:, j * LANES:(j + 1) * LANES] = pltpu.roll(g, (LANES - j * ng) % LANES, axis=1)
        beta_ref[:, j * LANES:(j + 1) * LANES] = pltpu.roll(beta, (2 * LANES - nhv - j * ng) % LANES, axis=1)


def _gdn_gates(proj, alog_p, dtb_p, bsz, seq, misc_blk, nhv):
    ts = min(seq, 512)
    ns = seq // ts
    ng = GDN_GROUP
    gw = nhv // ng * LANES
    body = functools.partial(_gdn_gate_body, ts=ts, chunk=GDN_CHUNK, nhv=nhv, ng=ng)
    return pl.pallas_call(
        body,
        grid=(bsz, ns),
        in_specs=[pl.BlockSpec((ts, LANES), lambda b, s: (b * ns + s, misc_blk)),
                  pl.BlockSpec((1, LANES), lambda b, s: (0, 0)),
                  pl.BlockSpec((1, LANES), lambda b, s: (0, 0))],
        out_specs=[pl.BlockSpec((ts, gw), lambda b, s: (b * ns + s, 0)),
                   pl.BlockSpec((ts, gw), lambda b, s: (b * ns + s, 0)),
                   pl.BlockSpec((None, LANES, ts), lambda b, s: (b, 0, s))],
        out_shape=[jax.ShapeDtypeStruct((bsz * seq, gw), F32),
                   jax.ShapeDtypeStruct((bsz * seq, gw), F32),
                   jax.ShapeDtypeStruct((bsz, LANES, seq), F32)],
        compiler_params=_cp("parallel", "parallel"),
        name="gdn_gates",
    )(proj, alog_p, dtb_p)


def _gdn_core_body(q_ref, k_ref, v_ref, gc_ref, beta_ref, gr_ref, z_ref, nw_ref, o_ref, *scratch,
                   ts, chunk, hd, ng, rep):
    st_refs = scratch[:ng]
    gcb_ref, bb_ref, a_ref, u_ref, w_ref, qkm_ref, rhs_ref, qd_ref, ktt_ref, egl_ref = scratch[ng:]
    nc = ts // chunk

    @pl.when(pl.program_id(2) == 0)
    def _():
        for st_ref in st_refs:
            st_ref[...] = jnp.zeros(st_ref.shape, F32)

    for g in range(ng):
        gcb_ref[g] = jnp.broadcast_to(gc_ref[:, g:g + 1], (ts, LANES))
        bb_ref[g] = jnp.broadcast_to(beta_ref[:, g:g + 1], (ts, LANES))

    ri = lax.broadcasted_iota(I32, (chunk, chunk), 0)
    ci = lax.broadcasted_iota(I32, (chunk, chunk), 1)
    incl = ri >= ci
    strict = ri > ci
    eye = jnp.where(ri == ci, 1.0, 0.0).astype(F32)

    def prep(c, _):
        rows = pl.ds(pl.multiple_of(c * chunk, chunk), chunk)
        for qh in range(ng // rep):
            q = q_ref[rows, qh * hd:(qh + 1) * hd]
            k = k_ref[rows, qh * hd:(qh + 1) * hd]
            k16 = k.astype(BF16)
            kk = _dot_nt(k16, k16)
            qk = _dot_nt(q.astype(BF16), k16)
            for vh in range(rep):
                g = qh * rep + vh
                n = c * ng + g
                gcb = gcb_ref[g, rows, :]
                bb = bb_ref[g, rows, :]
                decay = jnp.where(incl, jnp.exp(gcb[:, :chunk] - gr_ref[g, pl.ds(c, 1), :]), 0.0)
                a_ref[n] = jnp.where(strict, kk * bb[:, :chunk] * decay, 0.0)
                qkm_ref[n] = (qk * decay).astype(BF16)
                eg = jnp.exp(gcb)
                rhs_ref[n, :, 0:hd] = (v_ref[rows, g * hd:(g + 1) * hd] * bb).astype(BF16)
                rhs_ref[n, :, hd:2 * hd] = (k * bb * eg).astype(BF16)
                qd_ref[n] = (q * eg).astype(BF16)
                g_last = gcb[chunk - 1:chunk, :]
                ktt_ref[n] = (k * jnp.exp(g_last - gcb)).T.astype(BF16)
                egl_ref[n] = jnp.broadcast_to(jnp.exp(g_last), (8, LANES))
        return 0

    lax.fori_loop(0, nc, prep, 0)

    n_dbl = max(1, (chunk - 1).bit_length()) - 1

    def invert(ib, _):
        ns_ = [ib * GDN_INV_BATCH + j for j in range(GDN_INV_BATCH)]
        pw = [a_ref[n] for n in ns_]
        tm = [eye - a for a in pw]
        pw = [_dot(x, x) for x in [a.astype(BF16) for a in pw]]
        for lvl in range(n_dbl):
            last = lvl == n_dbl - 1
            lhs = [(t if last else jnp.concatenate([t, x], axis=0)).astype(BF16) for t, x in zip(tm, pw)]
            prod = [_dot(y, x.astype(BF16)) for y, x in zip(lhs, pw)]
            tm = [t + y[:chunk] for t, y in zip(tm, prod)]
            pw = [y[chunk:] for y in prod]
        uw = [_dot(t.astype(BF16), rhs_ref[n]) for n, t in zip(ns_, tm)]
        for n, x in zip(ns_, uw):
            u_ref[n] = x[:, :hd]
            w_ref[n] = x[:, hd:].astype(BF16)
        return 0

    lax.fori_loop(0, nc * ng // GDN_INV_BATCH, invert, 0)

    nw = nw_ref[...]

    def step(c, _):
        rows = pl.ds(pl.multiple_of(c * chunk, chunk), chunk)
        heads = range(ng)
        ns_ = [c * ng + g for g in heads]
        state = [st_refs[g][...] for g in heads]
        st16 = [x.astype(BF16) for x in state]
        ws = [_dot(w_ref[n], x) for n, x in zip(ns_, st16)]
        qs = [_dot(qd_ref[n], x) for n, x in zip(ns_, st16)]
        vn16 = [(u_ref[n] - x).astype(BF16) for n, x in zip(ns_, ws)]
        o = [y + _dot(qkm_ref[n], x) for n, x, y in zip(ns_, vn16, qs)]
        ds = [_dot(ktt_ref[n], x) for n, x in zip(ns_, vn16)]
        for g in heads:
            st_refs[g][...] = state[g] * egl_ref[ns_[g]][0:1, :] + ds[g]
            zz = z_ref[rows, g * hd:(g + 1) * hd]
            o_ref[rows, g * hd:(g + 1) * hd] = (_rms(o[g], nw) * (zz * jax.nn.sigmoid(zz))).astype(o_ref.dtype)
        return 0

    lax.fori_loop(0, nc, step, 0)


def _gdn_core(qkv, gc, beta, gct4, proj, norm_w, bsz, seq, nqk, nhv, hd, z_col):
    assert hd == LANES
    chunk = GDN_CHUNK
    ng = GDN_GROUP
    rep = nhv // nqk
    ts = min(seq, 512)
    ns = seq // ts
    n = (ts // chunk) * ng
    assert ng % rep == 0 and nhv % ng == 0 and n % GDN_INV_BATCH == 0
    qw, vw = (ng // rep) * hd, ng * hd
    body = functools.partial(_gdn_core_body, ts=ts, chunk=chunk, hd=hd, ng=ng, rep=rep)
    row = lambda b, h, s: b * ns + s
    return pl.pallas_call(
        body,
        grid=(bsz, nhv // ng, ns),
        in_specs=[pl.BlockSpec((ts, qw), lambda b, h, s: (row(b, h, s), h)),
                  pl.BlockSpec((ts, qw), lambda b, h, s: (row(b, h, s), nqk * hd // qw + h)),
                  pl.BlockSpec((ts, vw), lambda b, h, s: (row(b, h, s), 2 * nqk * hd // vw + h)),
                  pl.BlockSpec((ts, LANES), lambda b, h, s: (row(b, h, s), h)),
                  pl.BlockSpec((ts, LANES), lambda b, h, s: (row(b, h, s), h)),
                  pl.BlockSpec((None, ng, ts // chunk, chunk), lambda b, h, s: (b, h, s, 0)),
                  pl.BlockSpec((ts, vw), lambda b, h, s: (row(b, h, s), z_col // vw + h)),
                  pl.BlockSpec((1, hd), lambda b, h, s: (0, 0))],
        out_specs=pl.BlockSpec((ts, vw), lambda b, h, s: (row(b, h, s), h)),
        out_shape=jax.ShapeDtypeStruct((bsz * seq, nhv * hd), BF16),
        scratch_shapes=[pltpu.VMEM((hd, hd), F32) for _ in range(ng)] + [
                        pltpu.VMEM((ng, ts, LANES), F32),
                        pltpu.VMEM((ng, ts, LANES), F32),
                        pltpu.VMEM((n, chunk, chunk), F32),
                        pltpu.VMEM((n, chunk, hd), F32),
                        pltpu.VMEM((n, chunk, hd), BF16),
                        pltpu.VMEM((n, chunk, chunk), BF16),
                        pltpu.VMEM((n, chunk, 2 * hd), BF16),
                        pltpu.VMEM((n, chunk, hd), BF16),
                        pltpu.VMEM((n, hd, chunk), BF16),
                        pltpu.VMEM((n, 8, LANES), F32)],
        compiler_params=_cp("parallel", "parallel", "arbitrary"),
        name="gdn_core",
    )(qkv, qkv, qkv, gc, beta, gct4, proj, norm_w.reshape(1, hd))


def _merge_body(a_ref, b_ref, wa_ref, wb_ref, ga_ref, gb_ref, o_ref):
    ya = _dot(a_ref[...], wa_ref[...])
    yb = _dot(b_ref[...], wb_ref[...])
    o_ref[...] = (jax.nn.sigmoid(ga_ref[...]) * ya + jax.nn.sigmoid(gb_ref[...]) * yb).astype(o_ref.dtype)


def _merge(o_a, o_b, wa16, wb16, proj, ga_col0, gb_col0):
    t, ka = o_a.shape
    kb = o_b.shape[1]
    d = wa16.shape[1]
    tm = min(t, 512)
    tn = _tile(d, 512)
    ga0, gb0 = ga_col0 // tn, gb_col0 // tn
    return pl.pallas_call(
        _merge_body,
        grid=(t // tm, d // tn),
        in_specs=[pl.BlockSpec((tm, ka), lambda i, j: (i, 0)),
                  pl.BlockSpec((tm, kb), lambda i, j: (i, 0)),
                  pl.BlockSpec((ka, tn), lambda i, j: (0, j)),
                  pl.BlockSpec((kb, tn), lambda i, j: (0, j)),
                  pl.BlockSpec((tm, tn), lambda i, j: (i, ga0 + j)),
                  pl.BlockSpec((tm, tn), lambda i, j: (i, gb0 + j))],
        out_specs=pl.BlockSpec((tm, tn), lambda i, j: (i, j)),
        out_shape=jax.ShapeDtypeStruct((t, d), BF16),
        compiler_params=_cp("parallel", "parallel"),
        name="merge",
    )(o_a, o_b, wa16, wb16, proj, proj)


def _outproj_router_body(m_ref, x_ref, wo_ref, g1_ref, nw_ref, sc_ref, sh_ref, wr_ref, br_ref,
                         x1_ref, h2_ref, te_ref, tg_ref, *, topk):
    x1 = x_ref[...] + g1_ref[...] * _dot(m_ref[...], wo_ref[...])
    x1_ref[...] = x1
    h2 = _rms(x1, nw_ref[...]) * (1.0 + sc_ref[...]) + sh_ref[...]
    h2_ref[...] = h2
    logits = _dot(h2.astype(BF16), wr_ref[...]) + br_ref[...]
    lane = lax.broadcasted_iota(I32, logits.shape, 1)
    te = jnp.zeros(logits.shape, I32)
    tg = jnp.zeros(logits.shape, F32)
    denom = jnp.zeros((logits.shape[0], 1), F32)
    m0 = None
    for kk in range(topk):
        mx = jnp.max(logits, axis=-1, keepdims=True)
        idx = jnp.min(jnp.where(logits == mx, lane, LANES), axis=-1, keepdims=True)
        if kk == 0:
            m0 = mx
        e = jnp.exp(mx - m0)
        denom = denom + e
        te = jnp.where(lane == kk, idx, te)
        tg = jnp.where(lane == kk, e, tg)
        logits = jnp.where(lane == idx, -jnp.inf, logits)
    te_ref[...] = te
    tg_ref[...] = tg / denom


def _outproj_router(merged, x2, wo16, gate1, nw, scale, shift, wr16, br_p, seq):
    t, d = x2.shape
    tm = min(seq, 256)
    per_b = seq // tm
    vec = lambda: pl.BlockSpec((None, 1, d), lambda i: (i // per_b, 0, 0))
    return pl.pallas_call(
        functools.partial(_outproj_router_body, topk=TOP_K),
        grid=(t // tm,),
        in_specs=[pl.BlockSpec((tm, d), lambda i: (i, 0)),
                  pl.BlockSpec((tm, d), lambda i: (i, 0)),
                  pl.BlockSpec((d, d), lambda i: (0, 0)),
                  vec(),
                  pl.BlockSpec((1, d), lambda i: (0, 0)),
                  vec(), vec(),
                  pl.BlockSpec((d, LANES), lambda i: (0, 0)),
                  pl.BlockSpec((1, LANES), lambda i: (0, 0))],
        out_specs=[pl.BlockSpec((tm, d), lambda i: (i, 0)),
                   pl.BlockSpec((tm, d), lambda i: (i, 0)),
                   pl.BlockSpec((tm, LANES), lambda i: (i, 0)),
                   pl.BlockSpec((tm, LANES), lambda i: (i, 0))],
        out_shape=[jax.ShapeDtypeStruct((t, d), F32),
                   jax.ShapeDtypeStruct((t, d), F32),
                   jax.ShapeDtypeStruct((t, LANES), I32),
                   jax.ShapeDtypeStruct((t, LANES), F32)],
        compiler_params=_cp("parallel"),
        name="outproj_router",
    )(merged, x2, wo16, gate1[:, None, :], nw.reshape(1, d), scale[:, None, :], shift[:, None, :], wr16, br_p)


def _moe_rank_body(te_ref, pos_ref, cnt_ref, carry_ref, *, tt, topk):
    @pl.when(pl.program_id(0) == 0)
    def _():
        carry_ref[...] = jnp.zeros(carry_ref.shape, F32)

    te = te_ref[...]
    lane = lax.broadcasted_iota(I32, (tt, LANES), 1)
    onehot = jnp.zeros((tt, LANES), F32)
    for kk in range(topk):
        onehot = onehot + jnp.where(lane == te[:, kk:kk + 1], 1.0, 0.0)
    r = lax.broadcasted_iota(I32, (tt, tt), 0)
    c = lax.broadcasted_iota(I32, (tt, tt), 1)
    below = jnp.where(r > c, 1.0, 0.0).astype(BF16)
    rank = _dot(below, onehot.astype(BF16)) + carry_ref[0:1, :]
    pos = jnp.zeros((tt, LANES), I32)
    for kk in range(topk):
        p = jnp.sum(jnp.where(lane == te[:, kk:kk + 1], rank, 0.0), axis=-1, keepdims=True)
        pos = jnp.where(lane == kk, p.astype(I32), pos)
    pos_ref[...] = pos
    carry_ref[...] = carry_ref[...] + jnp.sum(onehot, axis=0, keepdims=True)
    cnt_ref[...] = carry_ref[...]


def _moe_rank(te):
    t = te.shape[0]
    tt = min(t, 512)
    return pl.pallas_call(
        functools.partial(_moe_rank_body, tt=tt, topk=TOP_K),
        grid=(t // tt,),
        in_specs=[pl.BlockSpec((tt, LANES), lambda i: (i, 0))],
        out_specs=[pl.BlockSpec((tt, LANES), lambda i: (i, 0)),
                   pl.BlockSpec((8, LANES), lambda i: (0, 0))],
        out_shape=[jax.ShapeDtypeStruct((t, LANES), I32),
                   jax.ShapeDtypeStruct((8, LANES), F32)],
        scratch_shapes=[pltpu.VMEM((8, LANES), F32)],
        compiler_params=_cp("arbitrary"),
        name="moe_rank",
    )(te)


def _moe_dispatch_body(dest_ref, h_ref, zero_ref, x_hbm, sem, *, tt, topk):
    del zero_ref
    base = pl.program_id(0) * tt * topk

    def issue(t, _):
        for kk in range(topk):
            dst = x_hbm.at[pl.ds(dest_ref[base + t * topk + kk], 1)]
            pltpu.make_async_copy(h_ref.at[pl.ds(t, 1)], dst, sem).start()
        return 0

    lax.fori_loop(0, tt, issue, 0, unroll=2)
    for _ in range(topk):
        pltpu.make_async_copy(h_ref, x_hbm.at[pl.ds(0, tt)], sem).wait()


def _moe_dispatch(dest_flat, h2, n_rows):
    t, d = h2.shape
    tt = min(t, 128)
    grid_spec = pltpu.PrefetchScalarGridSpec(
        num_scalar_prefetch=1,
        grid=(t // tt,),
        in_specs=[pl.BlockSpec((tt, d), lambda i, dest: (i, 0)),
                  pl.BlockSpec(memory_space=pl.ANY)],
        out_specs=pl.BlockSpec(memory_space=pl.ANY),
        scratch_shapes=[pltpu.SemaphoreType.DMA(())])
    return pl.pallas_call(
        functools.partial(_moe_dispatch_body, tt=tt, topk=TOP_K),
        grid_spec=grid_spec,
        out_shape=jax.ShapeDtypeStruct((n_rows, d), F32),
        input_output_aliases={2: 0},
        compiler_params=_cp("arbitrary"),
        name="moe_dispatch",
    )(dest_flat, h2, jnp.zeros((n_rows, d), F32))


def _moe_gu_body(be_ref, na_ref, x_ref, wg_ref, wu_ref, bg_ref, bu_ref, act_ref, wg16_ref, wu16_ref):
    i = pl.program_id(1)
    prev = be_ref[jnp.maximum(i - 1, 0)]

    @pl.when((i == 0) | (be_ref[i] != prev))
    def _():
        wg16_ref[...] = wg_ref[...].astype(BF16)
        wu16_ref[...] = wu_ref[...].astype(BF16)

    @pl.when(i < na_ref[0])
    def _():
        x = x_ref[...].astype(BF16)
        g = jnp.minimum(_dot(x, wg16_ref[...]) + bg_ref[...], SWIGLU_LIMIT)
        u = jnp.clip(_dot(x, wu16_ref[...]) + bu_ref[...], -SWIGLU_LIMIT, SWIGLU_LIMIT)
        act_ref[...] = ((u + 1.0) * (g * jax.nn.sigmoid(SWIGLU_ALPHA * g))).astype(act_ref.dtype)

    @pl.when(i >= na_ref[0])
    def _():
        act_ref[...] = jnp.zeros(act_ref.shape, act_ref.dtype)


def _moe_gu(block_e, n_act, x_rows, w_gu, b_gu):
    n_rows, d = x_rows.shape
    ne, _, ff2 = w_gu.shape
    ff = ff2 // 2
    bm = MOE_BM
    tn = _tile(ff, 1024)
    nt = ff // tn
    rowblk = lambda j, i, be, na: (jnp.minimum(i, na[0] - 1), 0)
    grid_spec = pltpu.PrefetchScalarGridSpec(
        num_scalar_prefetch=2,
        grid=(nt, n_rows // bm),
        in_specs=[pl.BlockSpec((bm, d), rowblk),
                  pl.BlockSpec((None, d, tn), lambda j, i, be, na: (be[i], 0, j)),
                  pl.BlockSpec((None, d, tn), lambda j, i, be, na: (be[i], 0, nt + j)),
                  pl.BlockSpec((None, 1, tn), lambda j, i, be, na: (be[i], 0, j)),
                  pl.BlockSpec((None, 1, tn), lambda j, i, be, na: (be[i], 0, nt + j))],
        out_specs=pl.BlockSpec((bm, tn), lambda j, i, be, na: (i, j)),
        scratch_shapes=[pltpu.VMEM((d, tn), BF16), pltpu.VMEM((d, tn), BF16)])
    return pl.pallas_call(
        _moe_gu_body,
        grid_spec=grid_spec,
        out_shape=jax.ShapeDtypeStruct((n_rows, ff), BF16),
        compiler_params=_cp("arbitrary", "arbitrary"),
        name="moe_gu",
    )(block_e, n_act, x_rows, w_gu, w_gu, b_gu.reshape(ne, 1, ff2), b_gu.reshape(ne, 1, ff2))


def _moe_down_body(be_ref, na_ref, a_ref, w_ref, b_ref, y_ref, w16_ref):
    i = pl.program_id(1)
    prev = be_ref[jnp.maximum(i - 1, 0)]

    @pl.when((i == 0) | (be_ref[i] != prev))
    def _():
        w16_ref[...] = w_ref[...].astype(BF16)

    @pl.when(i < na_ref[0])
    def _():
        y_ref[...] = _dot(a_ref[...], w16_ref[...]) + b_ref[...]

    @pl.when(i >= na_ref[0])
    def _():
        y_ref[...] = jnp.zeros(y_ref.shape, y_ref.dtype)


def _moe_down(block_e, n_act, act, w_down, b_down):
    n_rows, ff = act.shape
    ne, _, d = w_down.shape
    bm = MOE_BM
    tn = _tile(d, 2048)
    grid_spec = pltpu.PrefetchScalarGridSpec(
        num_scalar_prefetch=2,
        grid=(d // tn, n_rows // bm),
        in_specs=[pl.BlockSpec((bm, ff), lambda j, i, be, na: (jnp.minimum(i, na[0] - 1), 0)),
                  pl.BlockSpec((None, ff, tn), lambda j, i, be, na: (be[i], 0, j)),
                  pl.BlockSpec((None, 1, tn), lambda j, i, be, na: (be[i], 0, j))],
        out_specs=pl.BlockSpec((bm, tn), lambda j, i, be, na: (i, j)),
        scratch_shapes=[pltpu.VMEM((ff, tn), BF16)])
    return pl.pallas_call(
        _moe_down_body,
        grid_spec=grid_spec,
        out_shape=jax.ShapeDtypeStruct((n_rows, d), F32),
        compiler_params=_cp("arbitrary", "arbitrary"),
        name="moe_down",
    )(block_e, n_act, act, w_down, b_down.reshape(ne, 1, d))


def _moe_combine_body(dest_ref, x1_ref, tg_ref, g2_ref, fw_ref, y_hbm, o_ref, buf_ref, sem, *, tt, topk):
    base = pl.program_id(0) * tt * topk

    def issue(t, _):
        for kk in range(topk):
            src = y_hbm.at[pl.ds(dest_ref[base + t * topk + kk], 1)]
            pltpu.make_async_copy(src, buf_ref.at[kk, pl.ds(t, 1)], sem).start()
        return 0

    lax.fori_loop(0, tt, issue, 0, unroll=2)
    for kk in range(topk):
        pltpu.make_async_copy(y_hbm.at[pl.ds(0, tt)], buf_ref.at[kk], sem).wait()
    tg = tg_ref[...]
    y = tg[:, 0:1] * buf_ref[0]
    for kk in range(1, topk):
        y = y + tg[:, kk:kk + 1] * buf_ref[kk]
    o_ref[...] = _rms(x1_ref[...] + g2_ref[...] * y, fw_ref[...])


def _moe_combine(dest_flat, x1, tg, gate2, final_w, y_rows, seq):
    t, d = x1.shape
    tt = min(seq, 128)
    per_b = seq // tt
    grid_spec = pltpu.PrefetchScalarGridSpec(
        num_scalar_prefetch=1,
        grid=(t // tt,),
        in_specs=[pl.BlockSpec((tt, d), lambda i, dest: (i, 0)),
                  pl.BlockSpec((tt, LANES), lambda i, dest: (i, 0)),
                  pl.BlockSpec((None, 1, d), lambda i, dest: (i // per_b, 0, 0)),
                  pl.BlockSpec((1, d), lambda i, dest: (0, 0)),
                  pl.BlockSpec(memory_space=pl.ANY)],
        out_specs=pl.BlockSpec((tt, d), lambda i, dest: (i, 0)),
        scratch_shapes=[pltpu.VMEM((TOP_K, tt, d), F32), pltpu.SemaphoreType.DMA(())])
    return pl.pallas_call(
        functools.partial(_moe_combine_body, tt=tt, topk=TOP_K),
        grid_spec=grid_spec,
        out_shape=jax.ShapeDtypeStruct((t, d), F32),
        compiler_params=_cp("arbitrary"),
        name="moe_combine",
    )(dest_flat, x1, tg, gate2[:, None, :], final_w.reshape(1, d), y_rows)


def _pad_lanes(v, off=0, fill=0.0):
    out = jnp.full((1, LANES), fill, F32)
    return out.at[0, off:off + v.shape[0]].set(v.astype(F32))


def _layer(x2, c, bsz, seq, p, final_w):
    d = x2.shape[1]
    qr, nh, dh = p["w_uq"].shape
    kvr = p["w_uk"].shape[0]
    nhi, di = p["w_uqi"].shape[1:]
    hd = p["gdn_norm_w"].shape[0]
    nhv = p["a_log"].shape[0]
    v_w = nhv * hd
    qk_w = (p["conv_w"].shape[1] - v_w) // 2
    nqk = qk_w // hd
    ne = p["w_router"].shape[1]

    mod = _ada(c, p["ada_w"], p["ada_b"])
    shift1, scale1, gate1, shift2, scale2, gate2 = jnp.split(mod, 6, axis=-1)

    widths = (qr, kvr, di, nhi, qk_w, qk_w, v_w, nhv, nhv, v_w, d, d)
    offs = [0]
    for wd in widths:
        offs.append(offs[-1] + wd)
    col = lambda k: p["w_in"][:, offs[k]:offs[k + 1]]
    pad = jnp.zeros((d, LANES - 2 * nhv - nhi), F32)
    w_in = jnp.concatenate([col(0), col(1), col(2), col(7), col(8), col(3), pad,
                            col(4), col(5), col(6), col(9), col(10), col(11)], axis=1).astype(BF16)
    lat_w = qr + kvr + di + LANES
    gq_col = lat_w
    z_col = gq_col + 2 * qk_w + v_w
    ga_col = z_col + v_w
    gb_col = ga_col + d

    proj = _inproj(x2, p["norm1_w"], scale1, shift1, w_in, seq)

    wuq_t = p["w_uq"].reshape(qr, nh * dh).T.astype(BF16)
    wuqi_t = p["w_uqi"].reshape(qr, nhi * di).T.astype(BF16)
    wuk_r = jnp.transpose(p["w_uk"], (1, 0, 2)).astype(BF16)
    wuv_t = jnp.transpose(p["w_uv"], (1, 2, 0)).astype(BF16)
    qabs_t, qi_t, ckv, ckvt, ki, wi_t = _dsa_prep(
        proj, bsz, seq, p["q_lat_norm_w"], p["kv_lat_norm_w"], p["idx_k_norm_w"], wuq_t, wuqi_t, wuk_r,
        (qr, kvr, di, nh, dh, nhi, 2 * nhv))
    o_a = _dsa_attn(ki, ckv, ckvt, qi_t, qabs_t, wi_t, wuv_t).reshape(bsz * seq, nh * dh)

    qkv = _gdn_conv(proj, p["conv_w"], bsz, seq, gq_col, qk_w, v_w, hd)
    gc, beta, gct = _gdn_gates(proj, _pad_lanes(p["a_log"]), _pad_lanes(p["dt_bias"]), bsz, seq,
                               (lat_w - LANES) // LANES, nhv)
    gct4 = gct.reshape(bsz, LANES, seq // GDN_CHUNK, GDN_CHUNK)
    o_b = _gdn_core(qkv, gc, beta, gct4, proj, p["gdn_norm_w"], bsz, seq, nqk, nhv, hd, z_col)

    merged = _merge(o_a, o_b, p["w_branch_a"].astype(BF16), p["w_branch_b"].astype(BF16), proj, ga_col, gb_col)

    wr16 = jnp.zeros((d, LANES), F32).at[:, :ne].set(p["w_router"]).astype(BF16)
    br_p = _pad_lanes(p["b_router"], fill=-1e30)
    x1, h2, te, tg = _outproj_router(merged, x2, p["w_out"].astype(BF16), gate1, p["norm2_w"], scale2, shift2,
                                     wr16, br_p, seq)

    t = bsz * seq
    pos, cnt = _moe_rank(te)
    counts = cnt[0, :ne].astype(I32)
    padded = (counts + MOE_BM - 1) // MOE_BM * MOE_BM
    pad_end = jnp.cumsum(padded)
    pad_start = pad_end - padded
    dest = (pad_start[te[:, :TOP_K]] + pos[:, :TOP_K]).reshape(-1).astype(I32)
    n_blocks = -(-(t * TOP_K) // MOE_BM) + ne
    n_act = (pad_end[-1] // MOE_BM).astype(I32).reshape(1)
    blk = jnp.minimum(jnp.arange(n_blocks, dtype=I32), n_act[0] - 1) * MOE_BM
    block_e = jnp.minimum(jnp.sum(pad_end[None, :] <= blk[:, None], axis=1), ne - 1).astype(I32)
    x_rows = _moe_dispatch(dest, h2, n_blocks * MOE_BM)
    act = _moe_gu(block_e, n_act, x_rows, p["w_gu"], p["b_gu"])
    y_rows = _moe_down(block_e, n_act, act, p["w_down"], p["b_down"])
    return _moe_combine(dest, x1, tg, gate2, final_w, y_rows, seq)


@jax.jit
def kernel(x, c, ada_w, ada_b, norm1_w, w_in, q_lat_norm_w, kv_lat_norm_w, idx_k_norm_w, w_uq, w_uqi, w_uk, w_uv, conv_w, a_log, dt_bias, gdn_norm_w, w_branch_a, w_branch_b, w_out, norm2_w, w_router, b_router, w_gu, b_gu, w_down, b_down, final_norm_w):
    bsz, seq, d = x.shape
    stacked = dict(ada_w=ada_w, ada_b=ada_b, norm1_w=norm1_w, w_in=w_in, q_lat_norm_w=q_lat_norm_w,
                   kv_lat_norm_w=kv_lat_norm_w, idx_k_norm_w=idx_k_norm_w, w_uq=w_uq, w_uqi=w_uqi, w_uk=w_uk,
                   w_uv=w_uv, conv_w=conv_w, a_log=a_log, dt_bias=dt_bias, gdn_norm_w=gdn_norm_w,
                   w_branch_a=w_branch_a, w_branch_b=w_branch_b, w_out=w_out, norm2_w=norm2_w,
                   w_router=w_router, b_router=b_router, w_gu=w_gu, b_gu=b_gu, w_down=w_down, b_down=b_down)
    depth = ada_w.shape[0]
    assert depth == 1, "the final norm is fused into the last layer's combine kernel"
    x2 = x.reshape(bsz * seq, d)
    p = {k: v[0] for k, v in stacked.items()}
    out = _layer(x2, c, bsz, seq, p, final_norm_w)
    return out.reshape(bsz, seq, d)
```

```python
import functools

import jax
import jax.numpy as jnp
from jax import lax
from jax.experimental import pallas as pl
from jax.experimental.pallas import tpu as pltpu

F32 = jnp.float32
BF16 = jnp.bfloat16
I32 = jnp.int32

EPS = 1e-6
LANES = 128
VMEM_LIMIT = 56 * 1024 * 1024

IDX_TOPK = 256
GDN_CHUNK = 64
TOP_K = 4
SWIGLU_LIMIT = 7.0
SWIGLU_ALPHA = 1.702
MOE_BM = 512
ONES_ROWS = 16
INT_MIN = -(2 ** 31)
INT_MAX = 2 ** 31 - 1


def _cp(*sem):
    return pltpu.CompilerParams(dimension_semantics=sem, vmem_limit_bytes=VMEM_LIMIT)


def _tile(n, pref):
    if n <= pref:
        return n
    t = pref - pref % LANES
    while n % t:
        t -= LANES
    return t


def _rms(x, w):
    return x * lax.rsqrt(jnp.mean(x * x, axis=-1, keepdims=True) + EPS) * w


def _sigmoid(x):
    return 0.5 * jnp.tanh(0.5 * x) + 0.5


def _dot(a, b):
    return jnp.dot(a, b, preferred_element_type=F32)


def _dot_nt(a, b):
    return lax.dot_general(a, b, (((1,), (1,)), ((), ())), preferred_element_type=F32)


def _dot_tn(a, b):
    return lax.dot_general(a, b, (((0,), (0,)), ((), ())), preferred_element_type=F32)


def _ada_body(ct_ref, w_ref, b_ref, o_ref, *, nb):
    ct = ct_ref[...]
    s = ct * jax.nn.sigmoid(ct)
    w = w_ref[...]
    for b in range(nb):
        o_ref[b:b + 1, :] = jnp.sum(w * s[:, b:b + 1], axis=0, keepdims=True) + b_ref[...]


def _ada(c, w, b):
    nb, d = c.shape
    n = w.shape[1]
    tn = _tile(n, 1024)
    return pl.pallas_call(
        functools.partial(_ada_body, nb=nb),
        grid=(n // tn,),
        in_specs=[pl.BlockSpec((d, nb), lambda j: (0, 0)),
                  pl.BlockSpec((d, tn), lambda j: (0, j)),
                  pl.BlockSpec((1, tn), lambda j: (0, j))],
        out_specs=pl.BlockSpec((nb, tn), lambda j: (0, j)),
        out_shape=jax.ShapeDtypeStruct((nb, n), F32),
        compiler_params=_cp("parallel"),
        name="ada",
    )(c.T, w, b.reshape(1, n))


def _inproj_body(x_ref, nw_ref, sc_ref, sh_ref, w_ref, cw_ref, o_ref, h_ref, carry_ref, *ubuf_refs,
                 tm, tn, sw, rc, hd, per_b, c0, n_qk, n_conv, q_scale):
    i = pl.program_id(0)
    j = pl.program_id(1)

    @pl.when(j == 0)
    def _():
        h = _rms(x_ref[...], nw_ref[...]) * (1.0 + sc_ref[...]) + sh_ref[...]
        h_ref[...] = h.astype(BF16)

    cj = j - c0
    in_conv = (cj >= 0) & (cj < n_conv)

    @pl.when(jnp.logical_not(in_conv))
    def _():
        o_ref[...] = _dot(h_ref[...], w_ref[...])

    @pl.when(in_conv)
    def _():
        taps = cw_ref.shape[0]

        @pl.when(i % per_b == 0)
        def _():
            carry_ref[cj] = jnp.zeros((8, tn), F32)

        prev = carry_ref[cj]
        scale = jnp.where(cj < n_qk, q_scale, 1.0)
        normalise = cj < 2 * n_qk
        tails = []
        for s, ubuf_ref in enumerate(ubuf_refs):
            cols = slice(s * sw, (s + 1) * sw)
            w = cw_ref[:, cols]
            u = _dot(h_ref[...], w_ref[:, cols])
            ubuf_ref[0:8, :] = prev[:, cols]
            ubuf_ref[8:, :] = u
            tails.append(u[tm - 8:, :])
            for r in range(tm // rc):
                ext = ubuf_ref[r * rc:(r + 1) * rc + 8, :]
                y = ext[8:] * w[taps - 1:taps, :]
                for k in range(1, taps):
                    y = y + pltpu.roll(ext, k, axis=0)[8:] * w[taps - 1 - k:taps - k, :]
                y = y * _sigmoid(y)
                for hh in range(sw // hd):
                    x = y[:, hh * hd:(hh + 1) * hd]
                    inv = lax.rsqrt(jnp.sum(x * x, axis=-1, keepdims=True) + EPS) * scale
                    c_lo = s * sw + hh * hd
                    o_ref[r * rc:(r + 1) * rc, c_lo:c_lo + hd] = x * jnp.where(normalise, inv, 1.0)
        carry_ref[cj] = jnp.concatenate(tails, axis=1)


def _inproj(x2, nw, scale, shift, w16, conv_w, seq, conv_col0, qk_w, hd):
    t, d = x2.shape
    n = w16.shape[1]
    tm = min(seq, 1024)
    tn = _tile(n, 1024)
    per_b = seq // tm
    conv_ch = conv_w.shape[1]
    assert conv_col0 % tn == 0 and qk_w % tn == 0 and conv_ch % tn == 0 and tn % hd == 0
    c0, n_qk, n_conv = conv_col0 // tn, qk_w // tn, conv_ch // tn
    rc = min(tm, 64)
    sw = min(tn, 256)
    body = functools.partial(_inproj_body, tm=tm, tn=tn, sw=sw, rc=rc, hd=hd, per_b=per_b, c0=c0, n_qk=n_qk,
                             n_conv=n_conv, q_scale=hd ** -0.5)
    return pl.pallas_call(
        body,
        grid=(t // tm, n // tn),
        in_specs=[pl.BlockSpec((tm, d), lambda i, j: (i, 0)),
                  pl.BlockSpec((1, d), lambda i, j: (0, 0)),
                  pl.BlockSpec((None, 1, d), lambda i, j: (i // per_b, 0, 0)),
                  pl.BlockSpec((None, 1, d), lambda i, j: (i // per_b, 0, 0)),
                  pl.BlockSpec((d, tn), lambda i, j: (0, j)),
                  pl.BlockSpec((conv_w.shape[0], tn), lambda i, j: (0, jnp.clip(j - c0, 0, n_conv - 1)))],
        out_specs=pl.BlockSpec((tm, tn), lambda i, j: (i, j)),
        out_shape=jax.ShapeDtypeStruct((t, n), F32),
        scratch_shapes=[pltpu.VMEM((tm, d), BF16), pltpu.VMEM((n_conv, 8, tn), F32)]
        + [pltpu.VMEM((tm + 8, sw), F32) for _ in range(tn // sw)],
        compiler_params=_cp("arbitrary", "arbitrary"),
        name="inproj",
    )(x2, nw.reshape(1, d), scale[:, None, :], shift[:, None, :], w16, conv_w)


def _dsa_prep_body(lat_ref, qnw_ref, kvnw_ref, knw_ref, wuq_ref, wuqi_ref, wuk_ref,
                   qabs_ref, qi_ref, ckv_ref, ckvt_ref, ki_ref, wi_ref,
                   *, qr, kvr, di, nh, dh, nhi, q_scale, i_scale, w_scale, wi_off):
    lat = lat_ref[...]
    cq_t = _rms(lat[:, :qr], qnw_ref[...]).T.astype(BF16)
    q_t = _dot(wuq_ref[...], cq_t)
    for h in range(nh):
        q_h = q_t[h * dh:(h + 1) * dh].astype(BF16)
        qabs_ref[h] = (_dot(wuk_ref[h], q_h) * q_scale).astype(BF16)
    qi_t = _dot(wuqi_ref[...], cq_t) * i_scale
    for h in range(nhi):
        qi_ref[h] = qi_t[h * di:(h + 1) * di].astype(BF16)
    ckv = _rms(lat[:, qr:qr + kvr], kvnw_ref[...])
    ckv_ref[...] = ckv.astype(BF16)
    ckvt_ref[0:kvr, :] = ckv.T.astype(BF16)
    ckvt_ref[kvr:, :] = jnp.ones((ONES_ROWS, ckvt_ref.shape[1]), BF16)
    ki_ref[...] = _rms(lat[:, qr + kvr:qr + kvr + di], knw_ref[...]).astype(BF16)
    misc_t = lat[:, qr + kvr + di:].T
    wi_ref[...] = misc_t[wi_off:wi_off + nhi] * w_scale


def _dsa_prep(proj, bsz, seq, qnw, kvnw, knw, wuq_t, wuqi_t, wuk_r, dims):
    qr, kvr, di, nh, dh, nhi, wi_off = dims
    tm = min(seq, 512)
    ns = seq // tm
    lat_w = qr + kvr + di + LANES
    body = functools.partial(
        _dsa_prep_body, qr=qr, kvr=kvr, di=di, nh=nh, dh=dh, nhi=nhi,
        q_scale=dh ** -0.5, i_scale=di ** -0.5, w_scale=nhi ** -0.5, wi_off=wi_off)
    full = lambda shape: pl.BlockSpec(shape, lambda b, s: (0,) * len(shape))
    return pl.pallas_call(
        body,
        grid=(bsz, ns),
        in_specs=[pl.BlockSpec((tm, lat_w), lambda b, s: (b * ns + s, 0)),
                  full((1, qr)), full((1, kvr)), full((1, di)),
                  full((nh * dh, qr)), full((nhi * di, qr)), full((nh, kvr, dh))],
        out_specs=[pl.BlockSpec((None, nh, kvr, tm), lambda b, s: (b, 0, 0, s)),
                   pl.BlockSpec((None, nhi, di, tm), lambda b, s: (b, 0, 0, s)),
                   pl.BlockSpec((None, tm, kvr), lambda b, s: (b, s, 0)),
                   pl.BlockSpec((None, kvr + ONES_ROWS, tm), lambda b, s: (b, 0, s)),
                   pl.BlockSpec((None, tm, di), lambda b, s: (b, s, 0)),
                   pl.BlockSpec((None, nhi, tm), lambda b, s: (b, 0, s))],
        out_shape=[jax.ShapeDtypeStruct((bsz, nh, kvr, seq), BF16),
                   jax.ShapeDtypeStruct((bsz, nhi, di, seq), BF16),
                   jax.ShapeDtypeStruct((bsz, seq, kvr), BF16),
                   jax.ShapeDtypeStruct((bsz, kvr + ONES_ROWS, seq), BF16),
                   jax.ShapeDtypeStruct((bsz, seq, di), BF16),
                   jax.ShapeDtypeStruct((bsz, nhi, seq), F32)],
        compiler_params=_cp("parallel", "parallel"),
        name="dsa_prep",
    )(proj, qnw.reshape(1, qr), kvnw.reshape(1, kvr), knw.reshape(1, di), wuq_t, wuqi_t, wuk_r)


def _dsa_attn_body(ki_ref, ckv_ref, ckvt_ref, qi_ref, qabs_ref, wi_ref, wuv_ref, o_ref,
                   keys_ref, acc_ref, m_ref, ot_ref, thr_ref, jlim_ref,
                   *, tq, tk, nh, nhi, dh, kvr, topk, seq_bits):
    qb = pl.program_id(1)
    nk = ((qb + 1) * tq + tk - 1) // tk
    q_pos = qb * tq + lax.broadcasted_iota(I32, (tk, tq), 1)
    k_iota = lax.broadcasted_iota(I32, (tk, tq), 0)

    def score_tile(kt, _):
        ki_t = ki_ref[pl.ds(pl.multiple_of(kt * tk, tk), tk), :]
        acc = jnp.zeros((tk, tq), F32)
        for h in range(nhi):
            rel = _dot(ki_t, qi_ref[h])
            acc = acc + wi_ref[h:h + 1, :] * jnp.maximum(rel, 0.0)
        bits = lax.bitcast_convert_type(acc, I32)
        key = jnp.where(bits < 0, bits ^ INT_MAX, bits)
        key = jnp.where(kt * tk + k_iota <= q_pos, key, INT_MIN)
        keys_ref[pl.ds(pl.multiple_of(kt * tk, tk), tk), :] = key
        return 0

    lax.fori_loop(0, nk, score_tile, 0)

    def count(pred_fn):
        def tile(kt, c):
            k_t = keys_ref[pl.ds(pl.multiple_of(kt * tk, tk), tk), :]
            hit = jnp.where(pred_fn(k_t, kt), 1, 0).astype(I32)
            return c + jnp.sum(hit.reshape(tk // 8, 8, tq), axis=0)
        c8 = lax.fori_loop(0, nk, tile, jnp.zeros((8, tq), I32))
        return jnp.sum(c8, axis=0, keepdims=True)

    def bit_cond(carry):
        it, _, settled = carry
        return (it < 32) & (jnp.min(settled) == 0)

    def bit_step(carry):
        it, thr, settled = carry
        for _ in range(4):
            cand = thr ^ lax.shift_left(jnp.int32(1), 31 - it)
            cnt = count(lambda k_t, kt: k_t >= cand)
            take = (cnt >= topk) & (settled == 0)
            settled = jnp.where(take & (cnt == topk), 1, settled)
            thr = jnp.where(take, cand, thr)
            it = it + 1
        return it, thr, settled

    _, thr, _ = lax.while_loop(
        bit_cond, bit_step, (jnp.int32(0), jnp.full((1, tq), INT_MIN, I32), jnp.zeros((1, tq), I32)))
    thr = jnp.maximum(thr, INT_MIN + 1)
    cnt_gt = count(lambda k_t, kt: k_t > thr)
    cnt_ge = count(lambda k_t, kt: k_t >= thr)
    thr_ref[...] = thr
    jlim_ref[...] = jnp.full((1, tq), INT_MAX, I32)

    @pl.when(jnp.max(cnt_ge) > topk)
    def _():
        need = topk - cnt_gt

        def pos_step(it, p):
            cand = p + lax.shift_left(jnp.int32(1), seq_bits - 1 - it)
            cnt = count(lambda k_t, kt: (k_t == thr) & (kt * tk + k_iota < cand))
            return jnp.where(cnt < need, cand, p)

        jlim_ref[...] = lax.fori_loop(0, seq_bits, pos_step, jnp.zeros((1, tq), I32))

    m_ref[...] = jnp.full(m_ref.shape, -jnp.inf, F32)
    acc_ref[...] = jnp.zeros(acc_ref.shape, F32)

    @pl.when(nk % 2 == 1)
    def _():
        keys_ref[pl.ds(pl.multiple_of(nk * tk, tk), tk), :] = jnp.full((tk, tq), INT_MIN, I32)

    ta = 2 * tk
    a_iota = lax.broadcasted_iota(I32, (ta, tq), 0)

    def attn_tile(kt, _):
        off = pl.multiple_of(kt * ta, ta)
        k_t = keys_ref[pl.ds(off, ta), :]
        thr_b = thr_ref[...]
        mask = (k_t > thr_b) | ((k_t == thr_b) & (kt * ta + a_iota <= jlim_ref[...]))
        ckv_t = ckv_ref[pl.ds(off, ta), :]
        ckvt_t = ckvt_ref[:, pl.ds(off, ta)]
        logits = _dot(ckv_t, qabs_ref[0])
        for h in range(nh):
            s = jnp.where(mask, logits, -jnp.inf)
            if h + 1 < nh:
                logits = _dot(ckv_t, qabs_ref[h + 1])
            m_old = m_ref[h]
            m_new = jnp.maximum(m_old, jnp.max(s, axis=0, keepdims=True))
            m_use = jnp.where(m_new == -jnp.inf, 0.0, m_new)
            p = jnp.exp(s - m_use)
            alpha = jnp.exp(m_old - m_use)
            acc_ref[h] = alpha * acc_ref[h] + _dot(ckvt_t, p.astype(BF16))
            m_ref[h] = m_new
        return 0

    lax.fori_loop(0, (nk + 1) // 2, attn_tile, 0)

    for h in range(nh):
        acc = acc_ref[h]
        o_lat = (acc[:kvr] / acc[kvr:kvr + 1]).astype(BF16)
        ot_ref[h * dh:(h + 1) * dh, :] = _dot(wuv_ref[h], o_lat)
    o_ref[...] = ot_ref[...].T.astype(BF16)


def _dsa_attn(ki, ckv, ckvt, qi_t, qabs_t, wi_t, wuv_t):
    bsz, seq, di = ki.shape
    kvr = ckv.shape[2]
    nh, dh = wuv_t.shape[0], wuv_t.shape[1]
    nhi = qi_t.shape[1]
    tq = min(seq // 2, 256)
    tk = tq
    assert seq % (2 * tk) == 0
    topk = min(IDX_TOPK, seq // 4)
    body = functools.partial(_dsa_attn_body, tq=tq, tk=tk, nh=nh, nhi=nhi, dh=dh, kvr=kvr, topk=topk,
                             seq_bits=max(1, (seq - 1).bit_length()))
    return pl.pallas_call(
        body,
        grid=(bsz, seq // tq),
        in_specs=[pl.BlockSpec((None, seq, di), lambda b, q: (b, 0, 0)),
                  pl.BlockSpec((None, seq, kvr), lambda b, q: (b, 0, 0)),
                  pl.BlockSpec((None, kvr + ONES_ROWS, seq), lambda b, q: (b, 0, 0)),
                  pl.BlockSpec((None, nhi, di, tq), lambda b, q: (b, 0, 0, q)),
                  pl.BlockSpec((None, nh, kvr, tq), lambda b, q: (b, 0, 0, q)),
                  pl.BlockSpec((None, nhi, tq), lambda b, q: (b, 0, q)),
                  pl.BlockSpec((nh, dh, kvr), lambda b, q: (0, 0, 0))],
        out_specs=pl.BlockSpec((None, tq, nh * dh), lambda b, q: (b, q, 0)),
        out_shape=jax.ShapeDtypeStruct((bsz, seq, nh * dh), BF16),
        scratch_shapes=[pltpu.VMEM((seq, tq), I32),
                        pltpu.VMEM((nh, kvr + ONES_ROWS, tq), F32),
                        pltpu.VMEM((nh, 1, tq), F32),
                        pltpu.VMEM((nh * dh, tq), F32),
                        pltpu.VMEM((1, tq), I32),
                        pltpu.VMEM((1, tq), I32)],
        compiler_params=_cp("parallel", "parallel"),
        name="dsa_attn",
    )(ki, ckv, ckvt, qi_t, qabs_t, wi_t, wuv_t)


GDN_GROUP = 8
GDN_INV_BATCH = 16


def _gdn_gate_body(m_ref, alog_ref, dtb_ref, gc_ref, beta_ref, gct_ref, *, ts, chunk, nhv, ng):
    x = m_ref[...]
    z = x + dtb_ref[...]
    softplus = jnp.maximum(z, 0.0) + jnp.log(1.0 + jnp.exp(-jnp.abs(z)))
    g = -jnp.exp(alog_ref[...]) * softplus
    pos = lax.broadcasted_iota(I32, (ts, LANES), 0) % chunk
    d = 1
    while d < chunk:
        g = g + jnp.where(pos >= d, pltpu.roll(g, d, axis=0), 0.0)
        d *= 2
    gct_ref[...] = g.T
    beta = jax.nn.sigmoid(x)
    for j in range(nhv // ng):
        gc_ref[:, j * LANES:(j + 1) * LANES] = pltpu.roll(g, (LANES - j * ng) % LANES, axis=1)
        beta_ref[:, j * LANES:(j + 1) * LANES] = pltpu.roll(beta, (2 * LANES - nhv - j * ng) % LANES, axis=1)


def _gdn_gates(proj, alog_p, dtb_p, bsz, seq, misc_blk, nhv):
    ts = min(seq, 512)
    ns = seq // ts
    ng = GDN_GROUP
    gw = nhv // ng * LANES
    body = functools.partial(_gdn_gate_body, ts=ts, chunk=GDN_CHUNK, nhv=nhv, ng=ng)
    return pl.pallas_call(
        body,
        grid=(bsz, ns),
        in_specs=[pl.BlockSpec((ts, LANES), lambda b, s: (b * ns + s, misc_blk)),
                  pl.BlockSpec((1, LANES), lambda b, s: (0, 0)),
                  pl.BlockSpec((1, LANES), lambda b, s: (0, 0))],
        out_specs=[pl.BlockSpec((ts, gw), lambda b, s: (b * ns + s, 0)),
                   pl.BlockSpec((ts, gw), lambda b, s: (b * ns + s, 0)),
                   pl.BlockSpec((None, LANES, ts), lambda b, s: (b, 0, s))],
        out_shape=[jax.ShapeDtypeStruct((bsz * seq, gw), F32),
                   jax.ShapeDtypeStruct((bsz * seq, gw), F32),
                   jax.ShapeDtypeStruct((bsz, LANES, seq), F32)],
        compiler_params=_cp("parallel", "parallel"),
        name="gdn_gates",
    )(proj, alog_p, dtb_p)


def _gdn_core_body(q_ref, k_ref, v_ref, gc_ref, beta_ref, gr_ref, z_ref, nw_ref, o_ref, *scratch,
                   ts, chunk, hd, ng, rep):
    st_refs = scratch[:ng]
    gcb_ref, bb_ref, a_ref, u_ref, w_ref, qkm_ref, rhs_ref, qd_ref, ktt_ref, egl_ref = scratch[ng:]
    nc = ts // chunk

    @pl.when(pl.program_id(2) == 0)
    def _():
        for st_ref in st_refs:
            st_ref[...] = jnp.zeros(st_ref.shape, F32)

    for g in range(ng):
        gcb_ref[g] = jnp.broadcast_to(gc_ref[:, g:g + 1], (ts, LANES))
        bb_ref[g] = jnp.broadcast_to(beta_ref[:, g:g + 1], (ts, LANES))

    ri = lax.broadcasted_iota(I32, (chunk, chunk), 0)
    ci = lax.broadcasted_iota(I32, (chunk, chunk), 1)
    incl = ri >= ci
    strict = ri > ci
    eye = jnp.where(ri == ci, 1.0, 0.0).astype(F32)

    def prep(c, _):
        rows = pl.ds(pl.multiple_of(c * chunk, chunk), chunk)
        for qh in range(ng // rep):
            q = q_ref[rows, qh * hd:(qh + 1) * hd]
            k = k_ref[rows, qh * hd:(qh + 1) * hd]
            k16 = k.astype(BF16)
            kk = _dot_nt(k16, k16)
            qk = _dot_nt(q.astype(BF16), k16)
            for vh in range(rep):
                g = qh * rep + vh
                n = c * ng + g
                gcb = gcb_ref[g, rows, :]
                bb = bb_ref[g, rows, :]
                decay = jnp.where(incl, jnp.exp(gcb[:, :chunk] - gr_ref[g, pl.ds(c, 1), :]), 0.0)
                a_ref[n] = jnp.where(strict, kk * bb[:, :chunk] * decay, 0.0)
                qkm_ref[n] = (qk * decay).astype(BF16)
                eg = jnp.exp(gcb)
                rhs_ref[n, :, 0:hd] = (v_ref[rows, g * hd:(g + 1) * hd] * bb).astype(BF16)
                rhs_ref[n, :, hd:2 * hd] = (k * bb * eg).astype(BF16)
                qd_ref[n] = (q * eg).astype(BF16)
                g_last = gcb[chunk - 1:chunk, :]
                ktt_ref[n] = (k * jnp.exp(g_last - gcb)).T.astype(BF16)
                egl_ref[n] = jnp.broadcast_to(jnp.exp(g_last), (8, LANES))
        return 0

    lax.fori_loop(0, nc, prep, 0)

    n_dbl = max(1, (chunk - 1).bit_length()) - 1

    def invert(ib, _):
        ns_ = [ib * GDN_INV_BATCH + j for j in range(GDN_INV_BATCH)]
        pw = [a_ref[n] for n in ns_]
        tm = [eye - a for a in pw]
        pw = [_dot(x, x) for x in [a.astype(BF16) for a in pw]]
        for lvl in range(n_dbl):
            last = lvl == n_dbl - 1
            lhs = [(t if last else jnp.concatenate([t, x], axis=0)).astype(BF16) for t, x in zip(tm, pw)]
            prod = [_dot(y, x.astype(BF16)) for y, x in zip(lhs, pw)]
            tm = [t + y[:chunk] for t, y in zip(tm, prod)]
            pw = [y[chunk:] for y in prod]
        uw = [_dot(t.astype(BF16), rhs_ref[n]) for n, t in zip(ns_, tm)]
        for n, x in zip(ns_, uw):
            u_ref[n] = x[:, :hd]
            w_ref[n] = x[:, hd:].astype(BF16)
        return 0

    lax.fori_loop(0, nc * ng // GDN_INV_BATCH, invert, 0)

    nw = nw_ref[...]

    def step(c, _):
        rows = pl.ds(pl.multiple_of(c * chunk, chunk), chunk)
        heads = range(ng)
        ns_ = [c * ng + g for g in heads]
        state = [st_refs[g][...] for g in heads]
        st16 = [x.astype(BF16) for x in state]
        ws = [_dot(w_ref[n], x) for n, x in zip(ns_, st16)]
        qs = [_dot(qd_ref[n], x) for n, x in zip(ns_, st16)]
        vn16 = [(u_ref[n] - x).astype(BF16) for n, x in zip(ns_, ws)]
        o = [y + _dot(qkm_ref[n], x) for n, x, y in zip(ns_, vn16, qs)]
        ds = [_dot(ktt_ref[n], x) for n, x in zip(ns_, vn16)]
        for g in heads:
            st_refs[g][...] = state[g] * egl_ref[ns_[g]][0:1, :] + ds[g]
            zz = z_ref[rows, g * hd:(g + 1) * hd]
            o_ref[rows, g * hd:(g + 1) * hd] = (_rms(o[g], nw) * (zz * _sigmoid(zz))).astype(o_ref.dtype)
        return 0

    lax.fori_loop(0, nc, step, 0)


def _gdn_core(proj, gc, beta, gct4, norm_w, bsz, seq, nqk, nhv, hd, q_col, z_col):
    assert hd == LANES
    chunk = GDN_CHUNK
    ng = GDN_GROUP
    rep = nhv // nqk
    ts = min(seq, 512)
    ns = seq // ts
    n = (ts // chunk) * ng
    assert ng % rep == 0 and nhv % ng == 0 and n % GDN_INV_BATCH == 0
    qw, vw = (ng // rep) * hd, ng * hd
    k_col, v_col = q_col + nqk * hd, q_col + 2 * nqk * hd
    assert q_col % qw == 0 and k_col % qw == 0 and v_col % vw == 0 and z_col % vw == 0
    body = functools.partial(_gdn_core_body, ts=ts, chunk=chunk, hd=hd, ng=ng, rep=rep)
    row = lambda b, h, s: b * ns + s
    return pl.pallas_call(
        body,
        grid=(bsz, nhv // ng, ns),
        in_specs=[pl.BlockSpec((ts, qw), lambda b, h, s: (row(b, h, s), q_col // qw + h)),
                  pl.BlockSpec((ts, qw), lambda b, h, s: (row(b, h, s), k_col // qw + h)),
                  pl.BlockSpec((ts, vw), lambda b, h, s: (row(b, h, s), v_col // vw + h)),
                  pl.BlockSpec((ts, LANES), lambda b, h, s: (row(b, h, s), h)),
                  pl.BlockSpec((ts, LANES), lambda b, h, s: (row(b, h, s), h)),
                  pl.BlockSpec((None, ng, ts // chunk, chunk), lambda b, h, s: (b, h, s, 0)),
                  pl.BlockSpec((ts, vw), lambda b, h, s: (row(b, h, s), z_col // vw + h)),
                  pl.BlockSpec((1, hd), lambda b, h, s: (0, 0))],
        out_specs=pl.BlockSpec((ts, vw), lambda b, h, s: (row(b, h, s), h)),
        out_shape=jax.ShapeDtypeStruct((bsz * seq, nhv * hd), BF16),
        scratch_shapes=[pltpu.VMEM((hd, hd), F32) for _ in range(ng)] + [
                        pltpu.VMEM((ng, ts, LANES), F32),
                        pltpu.VMEM((ng, ts, LANES), F32),
                        pltpu.VMEM((n, chunk, chunk), F32),
                        pltpu.VMEM((n, chunk, hd), F32),
                        pltpu.VMEM((n, chunk, hd), BF16),
                        pltpu.VMEM((n, chunk, chunk), BF16),
                        pltpu.VMEM((n, chunk, 2 * hd), BF16),
                        pltpu.VMEM((n, chunk, hd), BF16),
                        pltpu.VMEM((n, hd, chunk), BF16),
                        pltpu.VMEM((n, 8, LANES), F32)],
        compiler_params=_cp("parallel", "parallel", "arbitrary"),
        name="gdn_core",
    )(proj, proj, proj, gc, beta, gct4, proj, norm_w.reshape(1, hd))


def _merge_body(a_ref, b_ref, wa_ref, wb_ref, ga_ref, gb_ref, o_ref):
    ya = _dot(a_ref[...], wa_ref[...])
    yb = _dot(b_ref[...], wb_ref[...])
    o_ref[...] = (_sigmoid(ga_ref[...]) * ya + _sigmoid(gb_ref[...]) * yb).astype(o_ref.dtype)


def _merge(o_a, o_b, wa16, wb16, proj, ga_col0, gb_col0):
    t, ka = o_a.shape
    kb = o_b.shape[1]
    d = wa16.shape[1]
    tm = min(t, 512)
    tn = _tile(d, 512)
    ga0, gb0 = ga_col0 // tn, gb_col0 // tn
    return pl.pallas_call(
        _merge_body,
        grid=(t // tm, d // tn),
        in_specs=[pl.BlockSpec((tm, ka), lambda i, j: (i, 0)),
                  pl.BlockSpec((tm, kb), lambda i, j: (i, 0)),
                  pl.BlockSpec((ka, tn), lambda i, j: (0, j)),
                  pl.BlockSpec((kb, tn), lambda i, j: (0, j)),
                  pl.BlockSpec((tm, tn), lambda i, j: (i, ga0 + j)),
                  pl.BlockSpec((tm, tn), lambda i, j: (i, gb0 + j))],
        out_specs=pl.BlockSpec((tm, tn), lambda i, j: (i, j)),
        out_shape=jax.ShapeDtypeStruct((t, d), BF16),
        compiler_params=_cp("parallel", "parallel"),
        name="merge",
    )(o_a, o_b, wa16, wb16, proj, proj)


def _outproj_router_body(m_ref, x_ref, wo_ref, g1_ref, nw_ref, sc_ref, sh_ref, wr_ref, br_ref,
                         x1_ref, h2_ref, te_ref, tg_ref, *, topk):
    x1 = x_ref[...] + g1_ref[...] * _dot(m_ref[...], wo_ref[...])
    x1_ref[...] = x1
    h2 = _rms(x1, nw_ref[...]) * (1.0 + sc_ref[...]) + sh_ref[...]
    h2_ref[...] = h2
    logits = _dot(h2.astype(BF16), wr_ref[...]) + br_ref[...]
    lane = lax.broadcasted_iota(I32, logits.shape, 1)
    te = jnp.zeros(logits.shape, I32)
    tg = jnp.zeros(logits.shape, F32)
    denom = jnp.zeros((logits.shape[0], 1), F32)
    m0 = None
    for kk in range(topk):
        mx = jnp.max(logits, axis=-1, keepdims=True)
        idx = jnp.min(jnp.where(logits == mx, lane, LANES), axis=-1, keepdims=True)
        if kk == 0:
            m0 = mx
        e = jnp.exp(mx - m0)
        denom = denom + e
        te = jnp.where(lane == kk, idx, te)
        tg = jnp.where(lane == kk, e, tg)
        logits = jnp.where(lane == idx, -jnp.inf, logits)
    te_ref[...] = te
    tg_ref[...] = tg / denom


def _outproj_router(merged, x2, wo16, gate1, nw, scale, shift, wr16, br_p, seq):
    t, d = x2.shape
    tm = min(seq, 256)
    per_b = seq // tm
    vec = lambda: pl.BlockSpec((None, 1, d), lambda i: (i // per_b, 0, 0))
    return pl.pallas_call(
        functools.partial(_outproj_router_body, topk=TOP_K),
        grid=(t // tm,),
        in_specs=[pl.BlockSpec((tm, d), lambda i: (i, 0)),
                  pl.BlockSpec((tm, d), lambda i: (i, 0)),
                  pl.BlockSpec((d, d), lambda i: (0, 0)),
                  vec(),
                  pl.BlockSpec((1, d), lambda i: (0, 0)),
                  vec(), vec(),
                  pl.BlockSpec((d, LANES), lambda i: (0, 0)),
                  pl.BlockSpec((1, LANES), lambda i: (0, 0))],
        out_specs=[pl.BlockSpec((tm, d), lambda i: (i, 0)),
                   pl.BlockSpec((tm, d), lambda i: (i, 0)),
                   pl.BlockSpec((tm, LANES), lambda i: (i, 0)),
                   pl.BlockSpec((tm, LANES), lambda i: (i, 0))],
        out_shape=[jax.ShapeDtypeStruct((t, d), F32),
                   jax.ShapeDtypeStruct((t, d), F32),
                   jax.ShapeDtypeStruct((t, LANES), I32),
                   jax.ShapeDtypeStruct((t, LANES), F32)],
        compiler_params=_cp("parallel"),
        name="outproj_router",
    )(merged, x2, wo16, gate1[:, None, :], nw.reshape(1, d), scale[:, None, :], shift[:, None, :], wr16, br_p)


def _moe_rank_body(te_ref, pos_ref, cnt_ref, carry_ref, *, tt, topk):
    @pl.when(pl.program_id(0) == 0)
    def _():
        carry_ref[...] = jnp.zeros(carry_ref.shape, F32)

    te = te_ref[...]
    lane = lax.broadcasted_iota(I32, (tt, LANES), 1)
    onehot = jnp.zeros((tt, LANES), F32)
    for kk in range(topk):
        onehot = onehot + jnp.where(lane == te[:, kk:kk + 1], 1.0, 0.0)
    r = lax.broadcasted_iota(I32, (tt, tt), 0)
    c = lax.broadcasted_iota(I32, (tt, tt), 1)
    below = jnp.where(r > c, 1.0, 0.0).astype(BF16)
    rank = _dot(below, onehot.astype(BF16)) + carry_ref[0:1, :]
    pos = jnp.zeros((tt, LANES), I32)
    for kk in range(topk):
        p = jnp.sum(jnp.where(lane == te[:, kk:kk + 1], rank, 0.0), axis=-1, keepdims=True)
        pos = jnp.where(lane == kk, p.astype(I32), pos)
    pos_ref[...] = pos
    carry_ref[...] = carry_ref[...] + jnp.sum(onehot, axis=0, keepdims=True)
    cnt_ref[...] = carry_ref[...]


def _moe_rank(te):
    t = te.shape[0]
    tt = min(t, 512)
    return pl.pallas_call(
        functools.partial(_moe_rank_body, tt=tt, topk=TOP_K),
        grid=(t // tt,),
        in_specs=[pl.BlockSpec((tt, LANES), lambda i: (i, 0))],
        out_specs=[pl.BlockSpec((tt, LANES), lambda i: (i, 0)),
                   pl.BlockSpec((8, LANES), lambda i: (0, 0))],
        out_shape=[jax.ShapeDtypeStruct((t, LANES), I32),
                   jax.ShapeDtypeStruct((8, LANES), F32)],
        scratch_shapes=[pltpu.VMEM((8, LANES), F32)],
        compiler_params=_cp("arbitrary"),
        name="moe_rank",
    )(te)


def _moe_dispatch_body(dest_ref, h_ref, zero_ref, x_hbm, sem, *, tt, topk):
    del zero_ref
    base = pl.program_id(0) * tt * topk

    def issue(t, _):
        for kk in range(topk):
            dst = x_hbm.at[pl.ds(dest_ref[base + t * topk + kk], 1)]
            pltpu.make_async_copy(h_ref.at[pl.ds(t, 1)], dst, sem).start()
        return 0

    lax.fori_loop(0, tt, issue, 0, unroll=2)
    for _ in range(topk):
        pltpu.make_async_copy(h_ref, x_hbm.at[pl.ds(0, tt)], sem).wait()


def _moe_dispatch(dest_flat, h2, n_rows):
    t, d = h2.shape
    tt = min(t, 128)
    grid_spec = pltpu.PrefetchScalarGridSpec(
        num_scalar_prefetch=1,
        grid=(t // tt,),
        in_specs=[pl.BlockSpec((tt, d), lambda i, dest: (i, 0)),
                  pl.BlockSpec(memory_space=pl.ANY)],
        out_specs=pl.BlockSpec(memory_space=pl.ANY),
        scratch_shapes=[pltpu.SemaphoreType.DMA(())])
    return pl.pallas_call(
        functools.partial(_moe_dispatch_body, tt=tt, topk=TOP_K),
        grid_spec=grid_spec,
        out_shape=jax.ShapeDtypeStruct((n_rows, d), F32),
        input_output_aliases={2: 0},
        compiler_params=_cp("arbitrary"),
        name="moe_dispatch",
    )(dest_flat, h2, jnp.zeros((n_rows, d), F32))


def _moe_gu_body(be_ref, na_ref, x_ref, wg_ref, wu_ref, bg_ref, bu_ref, act_ref, wg16_ref, wu16_ref):
    i = pl.program_id(1)
    prev = be_ref[jnp.maximum(i - 1, 0)]

    @pl.when((i == 0) | (be_ref[i] != prev))
    def _():
        wg16_ref[...] = wg_ref[...].astype(BF16)
        wu16_ref[...] = wu_ref[...].astype(BF16)

    @pl.when(i < na_ref[0])
    def _():
        x = x_ref[...].astype(BF16)
        g = jnp.minimum(_dot(x, wg16_ref[...]) + bg_ref[...], SWIGLU_LIMIT)
        u = jnp.clip(_dot(x, wu16_ref[...]) + bu_ref[...], -SWIGLU_LIMIT, SWIGLU_LIMIT)
        act_ref[...] = ((u + 1.0) * (g * _sigmoid(SWIGLU_ALPHA * g))).astype(act_ref.dtype)

    @pl.when(i >= na_ref[0])
    def _():
        act_ref[...] = jnp.zeros(act_ref.shape, act_ref.dtype)


def _moe_gu(block_e, n_act, x_rows, w_gu, b_gu):
    n_rows, d = x_rows.shape
    ne, _, ff2 = w_gu.shape
    ff = ff2 // 2
    bm = MOE_BM
    tn = _tile(ff, 1024)
    nt = ff // tn
    rowblk = lambda j, i, be, na: (jnp.minimum(i, na[0] - 1), 0)
    grid_spec = pltpu.PrefetchScalarGridSpec(
        num_scalar_prefetch=2,
        grid=(nt, n_rows // bm),
        in_specs=[pl.BlockSpec((bm, d), rowblk),
                  pl.BlockSpec((None, d, tn), lambda j, i, be, na: (be[i], 0, j)),
                  pl.BlockSpec((None, d, tn), lambda j, i, be, na: (be[i], 0, nt + j)),
                  pl.BlockSpec((None, 1, tn), lambda j, i, be, na: (be[i], 0, j)),
                  pl.BlockSpec((None, 1, tn), lambda j, i, be, na: (be[i], 0, nt + j))],
        out_specs=pl.BlockSpec((bm, tn), lambda j, i, be, na: (i, j)),
        scratch_shapes=[pltpu.VMEM((d, tn), BF16), pltpu.VMEM((d, tn), BF16)])
    return pl.pallas_call(
        _moe_gu_body,
        grid_spec=grid_spec,
        out_shape=jax.ShapeDtypeStruct((n_rows, ff), BF16),
        compiler_params=_cp("arbitrary", "arbitrary"),
        name="moe_gu",
    )(block_e, n_act, x_rows, w_gu, w_gu, b_gu.reshape(ne, 1, ff2), b_gu.reshape(ne, 1, ff2))


def _moe_down_body(be_ref, na_ref, a_ref, w_ref, b_ref, y_ref, w16_ref):
    i = pl.program_id(1)
    prev = be_ref[jnp.maximum(i - 1, 0)]

    @pl.when((i == 0) | (be_ref[i] != prev))
    def _():
        w16_ref[...] = w_ref[...].astype(BF16)

    @pl.when(i < na_ref[0])
    def _():
        y_ref[...] = _dot(a_ref[...], w16_ref[...]) + b_ref[...]

    @pl.when(i >= na_ref[0])
    def _():
        y_ref[...] = jnp.zeros(y_ref.shape, y_ref.dtype)


def _moe_down(block_e, n_act, act, w_down, b_down):
    n_rows, ff = act.shape
    ne, _, d = w_down.shape
    bm = MOE_BM
    tn = _tile(d, 2048)
    grid_spec = pltpu.PrefetchScalarGridSpec(
        num_scalar_prefetch=2,
        grid=(d // tn, n_rows // bm),
        in_specs=[pl.BlockSpec((bm, ff), lambda j, i, be, na: (jnp.minimum(i, na[0] - 1), 0)),
                  pl.BlockSpec((None, ff, tn), lambda j, i, be, na: (be[i], 0, j)),
                  pl.BlockSpec((None, 1, tn), lambda j, i, be, na: (be[i], 0, j))],
        out_specs=pl.BlockSpec((bm, tn), lambda j, i, be, na: (i, j)),
        scratch_shapes=[pltpu.VMEM((ff, tn), BF16)])
    return pl.pallas_call(
        _moe_down_body,
        grid_spec=grid_spec,
        out_shape=jax.ShapeDtypeStruct((n_rows, d), F32),
        compiler_params=_cp("arbitrary", "arbitrary"),
        name="moe_down",
    )(block_e, n_act, act, w_down, b_down.reshape(ne, 1, d))


def _moe_combine_body(dest_ref, x1_ref, tg_ref, g2_ref, fw_ref, y_hbm, o_ref, buf_ref, sem, *, tt, topk):
    base = pl.program_id(0) * tt * topk

    def issue(t, _):
        for kk in range(topk):
            src = y_hbm.at[pl.ds(dest_ref[base + t * topk + kk], 1)]
            pltpu.make_async_copy(src, buf_ref.at[kk, pl.ds(t, 1)], sem).start()
        return 0

    lax.fori_loop(0, tt, issue, 0, unroll=2)
    for kk in range(topk):
        pltpu.make_async_copy(y_hbm.at[pl.ds(0, tt)], buf_ref.at[kk], sem).wait()
    tg = tg_ref[...]
    y = tg[:, 0:1] * buf_ref[0]
    for kk in range(1, topk):
        y = y + tg[:, kk:kk + 1] * buf_ref[kk]
    o_ref[...] = _rms(x1_ref[...] + g2_ref[...] * y, fw_ref[...])


def _moe_combine(dest_flat, x1, tg, gate2, final_w, y_rows, seq):
    t, d = x1.shape
    tt = min(seq, 128)
    per_b = seq // tt
    grid_spec = pltpu.PrefetchScalarGridSpec(
        num_scalar_prefetch=1,
        grid=(t // tt,),
        in_specs=[pl.BlockSpec((tt, d), lambda i, dest: (i, 0)),
                  pl.BlockSpec((tt, LANES), lambda i, dest: (i, 0)),
                  pl.BlockSpec((None, 1, d), lambda i, dest: (i // per_b, 0, 0)),
                  pl.BlockSpec((1, d), lambda i, dest: (0, 0)),
                  pl.BlockSpec(memory_space=pl.ANY)],
        out_specs=pl.BlockSpec((tt, d), lambda i, dest: (i, 0)),
        scratch_shapes=[pltpu.VMEM((TOP_K, tt, d), F32), pltpu.SemaphoreType.DMA(())])
    return pl.pallas_call(
        functools.partial(_moe_combine_body, tt=tt, topk=TOP_K),
        grid_spec=grid_spec,
        out_shape=jax.ShapeDtypeStruct((t, d), F32),
        compiler_params=_cp("arbitrary"),
        name="moe_combine",
    )(dest_flat, x1, tg, gate2[:, None, :], final_w.reshape(1, d), y_rows)


def _pad_lanes(v, off=0, fill=0.0):
    out = jnp.full((1, LANES), fill, F32)
    return out.at[0, off:off + v.shape[0]].set(v.astype(F32))


def _layer(x2, c, bsz, seq, p, final_w):
    d = x2.shape[1]
    qr, nh, dh = p["w_uq"].shape
    kvr = p["w_uk"].shape[0]
    nhi, di = p["w_uqi"].shape[1:]
    hd = p["gdn_norm_w"].shape[0]
    nhv = p["a_log"].shape[0]
    v_w = nhv * hd
    qk_w = (p["conv_w"].shape[1] - v_w) // 2
    nqk = qk_w // hd
    ne = p["w_router"].shape[1]

    mod = _ada(c, p["ada_w"], p["ada_b"])
    shift1, scale1, gate1, shift2, scale2, gate2 = jnp.split(mod, 6, axis=-1)

    widths = (qr, kvr, di, nhi, qk_w, qk_w, v_w, nhv, nhv, v_w, d, d)
    offs = [0]
    for wd in widths:
        offs.append(offs[-1] + wd)
    col = lambda k: p["w_in"][:, offs[k]:offs[k + 1]]
    pad = jnp.zeros((d, LANES - 2 * nhv - nhi), F32)
    w_in = jnp.concatenate([col(0), col(1), col(2), col(7), col(8), col(3), pad,
                            col(4), col(5), col(6), col(9), col(10), col(11)], axis=1).astype(BF16)
    lat_w = qr + kvr + di + LANES
    gq_col = lat_w
    z_col = gq_col + 2 * qk_w + v_w
    ga_col = z_col + v_w
    gb_col = ga_col + d

    proj = _inproj(x2, p["norm1_w"], scale1, shift1, w_in, p["conv_w"], seq, gq_col, qk_w, hd)

    wuq_t = p["w_uq"].reshape(qr, nh * dh).T.astype(BF16)
    wuqi_t = p["w_uqi"].reshape(qr, nhi * di).T.astype(BF16)
    wuk_r = jnp.transpose(p["w_uk"], (1, 0, 2)).astype(BF16)
    wuv_t = jnp.transpose(p["w_uv"], (1, 2, 0)).astype(BF16)
    qabs_t, qi_t, ckv, ckvt, ki, wi_t = _dsa_prep(
        proj, bsz, seq, p["q_lat_norm_w"], p["kv_lat_norm_w"], p["idx_k_norm_w"], wuq_t, wuqi_t, wuk_r,
        (qr, kvr, di, nh, dh, nhi, 2 * nhv))
    o_a = _dsa_attn(ki, ckv, ckvt, qi_t, qabs_t, wi_t, wuv_t).reshape(bsz * seq, nh * dh)

    gc, beta, gct = _gdn_gates(proj, _pad_lanes(p["a_log"]), _pad_lanes(p["dt_bias"]), bsz, seq,
                               (lat_w - LANES) // LANES, nhv)
    gct4 = gct.reshape(bsz, LANES, seq // GDN_CHUNK, GDN_CHUNK)
    o_b = _gdn_core(proj, gc, beta, gct4, p["gdn_norm_w"], bsz, seq, nqk, nhv, hd, gq_col, z_col)

    merged = _merge(o_a, o_b, p["w_branch_a"].astype(BF16), p["w_branch_b"].astype(BF16), proj, ga_col, gb_col)

    wr16 = jnp.zeros((d, LANES), F32).at[:, :ne].set(p["w_router"]).astype(BF16)
    br_p = _pad_lanes(p["b_router"], fill=-1e30)
    x1, h2, te, tg = _outproj_router(merged, x2, p["w_out"].astype(BF16), gate1, p["norm2_w"], scale2, shift2,
                                     wr16, br_p, seq)

    t = bsz * seq
    pos, cnt = _moe_rank(te)
    counts = cnt[0, :ne].astype(I32)
    padded = (counts + MOE_BM - 1) // MOE_BM * MOE_BM
    pad_end = jnp.cumsum(padded)
    pad_start = pad_end - padded
    dest = (pad_start[te[:, :TOP_K]] + pos[:, :TOP_K]).reshape(-1).astype(I32)
    n_blocks = -(-(t * TOP_K) // MOE_BM) + ne
    n_act = (pad_end[-1] // MOE_BM).astype(I32).reshape(1)
    blk = jnp.minimum(jnp.arange(n_blocks, dtype=I32), n_act[0] - 1) * MOE_BM
    block_e = jnp.minimum(jnp.sum(pad_end[None, :] <= blk[:, None], axis=1), ne - 1).astype(I32)
    x_rows = _moe_dispatch(dest, h2, n_blocks * MOE_BM)
    act = _moe_gu(block_e, n_act, x_rows, p["w_gu"], p["b_gu"])
    y_rows = _moe_down(block_e, n_act, act, p["w_down"], p["b_down"])
    return _moe_combine(dest, x1, tg, gate2, final_w, y_rows, seq)


@jax.jit
def kernel(x, c, ada_w, ada_b, norm1_w, w_in, q_lat_norm_w, kv_lat_norm_w, idx_k_norm_w, w_uq, w_uqi, w_uk, w_uv, conv_w, a_log, dt_bias, gdn_norm_w, w_branch_a, w_branch_b, w_out, norm2_w, w_router, b_router, w_gu, b_gu, w_down, b_down, final_norm_w):
    bsz, seq, d = x.shape
    stacked = dict(ada_w=ada_w, ada_b=ada_b, norm1_w=norm1_w, w_in=w_in, q_lat_norm_w=q_lat_norm_w,
                   kv_lat_norm_w=kv_lat_norm_w, idx_k_norm_w=idx_k_norm_w, w_uq=w_uq, w_uqi=w_uqi, w_uk=w_uk,
                   w_uv=w_uv, conv_w=conv_w, a_log=a_log, dt_bias=dt_bias, gdn_norm_w=gdn_norm_w,
                   w_branch_a=w_branch_a, w_branch_b=w_branch_b, w_out=w_out, norm2_w=norm2_w,
                   w_router=w_router, b_router=b_router, w_gu=w_gu, b_gu=b_gu, w_down=w_down, b_down=b_down)
    depth = ada_w.shape[0]
    assert depth == 1, "the final norm is fused into the last layer's combine kernel"
    x2 = x.reshape(bsz * seq, d)
    p = {k: v[0] for k, v in stacked.items()}
    out = _layer(x2, c, bsz, seq, p, final_norm_w)
    return out.reshape(bsz, seq, d)
```

```python
import functools

import jax
import jax.numpy as jnp
from jax import lax
from jax.experimental import pallas as pl
from jax.experimental.pallas import tpu as pltpu

F32 = jnp.float32
BF16 = jnp.bfloat16
I32 = jnp.int32

EPS = 1e-6
LANES = 128
VMEM_LIMIT = 56 * 1024 * 1024

IDX_TOPK = 256
GDN_CHUNK = 64
TOP_K = 4
SWIGLU_LIMIT = 7.0
SWIGLU_ALPHA = 1.702
MOE_BM = 512
ONES_ROWS = 16
INT_MIN = -(2 ** 31)
INT_MAX = 2 ** 31 - 1


def _cp(*sem):
    return pltpu.CompilerParams(dimension_semantics=sem, vmem_limit_bytes=VMEM_LIMIT)


def _tile(n, pref):
    if n <= pref:
        return n
    t = pref - pref % LANES
    while n % t:
        t -= LANES
    return t


def _rms(x, w):
    return x * lax.rsqrt(jnp.mean(x * x, axis=-1, keepdims=True) + EPS) * w


def _sigmoid(x):
    return 0.5 * jnp.tanh(0.5 * x) + 0.5


def _dot(a, b):
    return jnp.dot(a, b, preferred_element_type=F32)


def _dot_nt(a, b):
    return lax.dot_general(a, b, (((1,), (1,)), ((), ())), preferred_element_type=F32)


def _dot_tn(a, b):
    return lax.dot_general(a, b, (((0,), (0,)), ((), ())), preferred_element_type=F32)


def _ada_body(ct_ref, w_ref, b_ref, o_ref, *, nb):
    ct = ct_ref[...]
    s = ct * jax.nn.sigmoid(ct)
    w = w_ref[...]
    for b in range(nb):
        o_ref[b:b + 1, :] = jnp.sum(w * s[:, b:b + 1], axis=0, keepdims=True) + b_ref[...]


def _ada(c, w, b):
    nb, d = c.shape
    n = w.shape[1]
    tn = _tile(n, 1024)
    return pl.pallas_call(
        functools.partial(_ada_body, nb=nb),
        grid=(n // tn,),
        in_specs=[pl.BlockSpec((d, nb), lambda j: (0, 0)),
                  pl.BlockSpec((d, tn), lambda j: (0, j)),
                  pl.BlockSpec((1, tn), lambda j: (0, j))],
        out_specs=pl.BlockSpec((nb, tn), lambda j: (0, j)),
        out_shape=jax.ShapeDtypeStruct((nb, n), F32),
        compiler_params=_cp("parallel"),
        name="ada",
    )(c.T, w, b.reshape(1, n))


def _inproj_body(x_ref, nw_ref, sc_ref, sh_ref, w_ref, cw_ref, o_ref, h_ref, carry_ref, *ubuf_refs,
                 tm, tn, sw, rc, hd, per_b, c0, n_qk, n_conv, q_scale):
    i = pl.program_id(0)
    j = pl.program_id(1)

    @pl.when(j == 0)
    def _():
        h = _rms(x_ref[...], nw_ref[...]) * (1.0 + sc_ref[...]) + sh_ref[...]
        h_ref[...] = h.astype(BF16)

    cj = j - c0
    in_conv = (cj >= 0) & (cj < n_conv)

    @pl.when(jnp.logical_not(in_conv))
    def _():
        o_ref[...] = _dot(h_ref[...], w_ref[...])

    @pl.when(in_conv)
    def _():
        taps = cw_ref.shape[0]

        @pl.when(i % per_b == 0)
        def _():
            carry_ref[cj] = jnp.zeros((8, tn), F32)

        prev = carry_ref[cj]
        scale = jnp.where(cj < n_qk, q_scale, 1.0)
        normalise = cj < 2 * n_qk
        tails = []
        for s, ubuf_ref in enumerate(ubuf_refs):
            cols = slice(s * sw, (s + 1) * sw)
            w = cw_ref[:, cols]
            u = _dot(h_ref[...], w_ref[:, cols])
            ubuf_ref[0:8, :] = prev[:, cols]
            ubuf_ref[8:, :] = u
            tails.append(u[tm - 8:, :])
            for r in range(tm // rc):
                ext = ubuf_ref[r * rc:(r + 1) * rc + 8, :]
                y = ext[8:] * w[taps - 1:taps, :]
                for k in range(1, taps):
                    y = y + pltpu.roll(ext, k, axis=0)[8:] * w[taps - 1 - k:taps - k, :]
                y = y * _sigmoid(y)
                for hh in range(sw // hd):
                    x = y[:, hh * hd:(hh + 1) * hd]
                    inv = lax.rsqrt(jnp.sum(x * x, axis=-1, keepdims=True) + EPS) * scale
                    c_lo = s * sw + hh * hd
                    o_ref[r * rc:(r + 1) * rc, c_lo:c_lo + hd] = x * jnp.where(normalise, inv, 1.0)
        carry_ref[cj] = jnp.concatenate(tails, axis=1)


def _inproj(x2, nw, scale, shift, w16, conv_w, seq, conv_col0, qk_w, hd):
    t, d = x2.shape
    n = w16.shape[1]
    tm = min(seq, 1024)
    tn = _tile(n, 1024)
    per_b = seq // tm
    conv_ch = conv_w.shape[1]
    assert conv_col0 % tn == 0 and qk_w % tn == 0 and conv_ch % tn == 0 and tn % hd == 0
    c0, n_qk, n_conv = conv_col0 // tn, qk_w // tn, conv_ch // tn
    rc = min(tm, 64)
    sw = min(tn, 256)
    body = functools.partial(_inproj_body, tm=tm, tn=tn, sw=sw, rc=rc, hd=hd, per_b=per_b, c0=c0, n_qk=n_qk,
                             n_conv=n_conv, q_scale=hd ** -0.5)
    return pl.pallas_call(
        body,
        grid=(t // tm, n // tn),
        in_specs=[pl.BlockSpec((tm, d), lambda i, j: (i, 0)),
                  pl.BlockSpec((1, d), lambda i, j: (0, 0)),
                  pl.BlockSpec((None, 1, d), lambda i, j: (i // per_b, 0, 0)),
                  pl.BlockSpec((None, 1, d), lambda i, j: (i // per_b, 0, 0)),
                  pl.BlockSpec((d, tn), lambda i, j: (0, j)),
                  pl.BlockSpec((conv_w.shape[0], tn), lambda i, j: (0, jnp.clip(j - c0, 0, n_conv - 1)))],
        out_specs=pl.BlockSpec((tm, tn), lambda i, j: (i, j)),
        out_shape=jax.ShapeDtypeStruct((t, n), F32),
        scratch_shapes=[pltpu.VMEM((tm, d), BF16), pltpu.VMEM((n_conv, 8, tn), F32)]
        + [pltpu.VMEM((tm + 8, sw), F32) for _ in range(tn // sw)],
        compiler_params=_cp("arbitrary", "arbitrary"),
        name="inproj",
    )(x2, nw.reshape(1, d), scale[:, None, :], shift[:, None, :], w16, conv_w)


def _dsa_prep_body(lat_ref, qnw_ref, kvnw_ref, knw_ref, wuq_ref, wuqi_ref, wuk_ref,
                   qabs_ref, qi_ref, ckv_ref, ckvt_ref, ki_ref, wi_ref,
                   *, qr, kvr, di, nh, dh, nhi, q_scale, i_scale, w_scale, wi_off):
    lat = lat_ref[...]
    cq_t = _rms(lat[:, :qr], qnw_ref[...]).T.astype(BF16)
    q_t = _dot(wuq_ref[...], cq_t)
    for h in range(nh):
        q_h = q_t[h * dh:(h + 1) * dh].astype(BF16)
        qabs_ref[h] = (_dot(wuk_ref[h], q_h) * q_scale).astype(BF16)
    qi_t = _dot(wuqi_ref[...], cq_t) * i_scale
    for h in range(nhi):
        qi_ref[h] = qi_t[h * di:(h + 1) * di].astype(BF16)
    ckv = _rms(lat[:, qr:qr + kvr], kvnw_ref[...])
    ckv_ref[...] = ckv.astype(BF16)
    ckvt_ref[0:kvr, :] = ckv.T.astype(BF16)
    ckvt_ref[kvr:, :] = jnp.ones((ONES_ROWS, ckvt_ref.shape[1]), BF16)
    ki_ref[...] = _rms(lat[:, qr + kvr:qr + kvr + di], knw_ref[...]).astype(BF16)
    misc_t = lat[:, qr + kvr + di:].T
    wi_ref[...] = misc_t[wi_off:wi_off + nhi] * w_scale


def _dsa_prep(proj, bsz, seq, qnw, kvnw, knw, wuq_t, wuqi_t, wuk_r, dims):
    qr, kvr, di, nh, dh, nhi, wi_off, lat_blk = dims
    tm = min(seq, 512)
    ns = seq // tm
    lat_w = qr + kvr + di + LANES
    body = functools.partial(
        _dsa_prep_body, qr=qr, kvr=kvr, di=di, nh=nh, dh=dh, nhi=nhi,
        q_scale=dh ** -0.5, i_scale=di ** -0.5, w_scale=nhi ** -0.5, wi_off=wi_off)
    full = lambda shape: pl.BlockSpec(shape, lambda b, s: (0,) * len(shape))
    return pl.pallas_call(
        body,
        grid=(bsz, ns),
        in_specs=[pl.BlockSpec((tm, lat_w), lambda b, s: (b * ns + s, lat_blk)),
                  full((1, qr)), full((1, kvr)), full((1, di)),
                  full((nh * dh, qr)), full((nhi * di, qr)), full((nh, kvr, dh))],
        out_specs=[pl.BlockSpec((None, nh, kvr, tm), lambda b, s: (b, 0, 0, s)),
                   pl.BlockSpec((None, nhi, di, tm), lambda b, s: (b, 0, 0, s)),
                   pl.BlockSpec((None, tm, kvr), lambda b, s: (b, s, 0)),
                   pl.BlockSpec((None, kvr + ONES_ROWS, tm), lambda b, s: (b, 0, s)),
                   pl.BlockSpec((None, tm, di), lambda b, s: (b, s, 0)),
                   pl.BlockSpec((None, nhi, tm), lambda b, s: (b, 0, s))],
        out_shape=[jax.ShapeDtypeStruct((bsz, nh, kvr, seq), BF16),
                   jax.ShapeDtypeStruct((bsz, nhi, di, seq), BF16),
                   jax.ShapeDtypeStruct((bsz, seq, kvr), BF16),
                   jax.ShapeDtypeStruct((bsz, kvr + ONES_ROWS, seq), BF16),
                   jax.ShapeDtypeStruct((bsz, seq, di), BF16),
                   jax.ShapeDtypeStruct((bsz, nhi, seq), F32)],
        compiler_params=_cp("parallel", "parallel"),
        name="dsa_prep",
    )(proj, qnw.reshape(1, qr), kvnw.reshape(1, kvr), knw.reshape(1, di), wuq_t, wuqi_t, wuk_r)


def _dsa_attn_body(ki_ref, ckv_ref, ckvt_ref, qi_ref, qabs_ref, wi_ref, wuv_ref, o_ref,
                   keys_ref, k16_ref, acc_ref, m_ref, ot_ref, thr_ref, jlim_ref,
                   *, tq, tk, nh, nhi, dh, kvr, topk, seq_bits):
    qb = pl.program_id(1)
    nk = ((qb + 1) * tq + tk - 1) // tk
    q_pos = qb * tq + lax.broadcasted_iota(I32, (tk, tq), 1)
    k_iota = lax.broadcasted_iota(I32, (tk, tq), 0)

    def score_tile(kt, _):
        ki_t = ki_ref[pl.ds(pl.multiple_of(kt * tk, tk), tk), :]
        acc = jnp.zeros((tk, tq), F32)
        for h in range(nhi):
            rel = _dot(ki_t, qi_ref[h])
            acc = acc + wi_ref[h:h + 1, :] * jnp.maximum(rel, 0.0)
        acc = jnp.where(acc == 0.0, 0.0, acc)
        bits = lax.bitcast_convert_type(acc, I32)
        causal = kt * tk + k_iota <= q_pos
        key = jnp.where(bits < 0, bits ^ INT_MAX, bits)
        keys_ref[pl.ds(pl.multiple_of(kt * tk, tk), tk), :] = jnp.where(causal, key, INT_MIN)
        top = lax.bitcast_convert_type(bits & -65536, F32)
        k16_ref[pl.ds(pl.multiple_of(kt * tk, tk), tk), :] = jnp.where(causal, top, jnp.nan).astype(BF16)
        return 0

    lax.fori_loop(0, nk, score_tile, 0)

    one16 = jnp.ones((tk, tq), BF16)
    zero16 = jnp.zeros((tk, tq), BF16)

    def count16(cand_bf):
        cand_b = jnp.broadcast_to(cand_bf, (tk, tq))

        def tile(kt, c):
            k_t = k16_ref[pl.ds(pl.multiple_of(kt * tk, tk), tk), :]
            hit = jnp.where(k_t >= cand_b, one16, zero16)
            part = hit[0:16]
            for r in range(1, tk // 16):
                part = part + hit[r * 16:(r + 1) * 16]
            return c + part.astype(F32)
        c16 = lax.fori_loop(0, nk, tile, jnp.zeros((16, tq), F32))
        return jnp.sum(c16, axis=0, keepdims=True)

    def hi_step(it, hi):
        cand = hi + lax.shift_left(jnp.int32(1), 15 - it)
        pattern = jnp.where(cand >= 0, cand, cand ^ 0x7FFF) & 0xFFFF
        cand_bf = lax.bitcast_convert_type(lax.shift_left(pattern, 16), F32).astype(BF16)
        return jnp.where(count16(cand_bf) >= topk, cand, hi)

    hi16 = lax.fori_loop(0, 16, hi_step, jnp.full((1, tq), -(2 ** 15), I32))

    def count(pred_fn):
        def tile(kt, c):
            k_t = keys_ref[pl.ds(pl.multiple_of(kt * tk, tk), tk), :]
            hit = jnp.where(pred_fn(k_t, kt), 1, 0).astype(I32)
            return c + jnp.sum(hit.reshape(tk // 8, 8, tq), axis=0)
        c8 = lax.fori_loop(0, nk, tile, jnp.zeros((8, tq), I32))
        return jnp.sum(c8, axis=0, keepdims=True)

    def bit_cond(carry):
        it, _, settled = carry
        return (it < 32) & (jnp.min(settled) == 0)

    def bit_step(carry):
        it, thr, settled = carry
        for _ in range(4):
            cand = thr ^ lax.shift_left(jnp.int32(1), 31 - it)
            cnt = count(lambda k_t, kt: k_t >= cand)
            take = (cnt >= topk) & (settled == 0)
            settled = jnp.where(take & (cnt == topk), 1, settled)
            thr = jnp.where(take, cand, thr)
            it = it + 1
        return it, thr, settled

    _, thr, _ = lax.while_loop(
        bit_cond, bit_step, (jnp.int32(16), lax.shift_left(hi16, 16), jnp.zeros((1, tq), I32)))
    thr = jnp.maximum(thr, INT_MIN + 1)
    cnt_gt = count(lambda k_t, kt: k_t > thr)
    cnt_ge = count(lambda k_t, kt: k_t >= thr)
    thr_ref[...] = thr
    jlim_ref[...] = jnp.full((1, tq), INT_MAX, I32)

    @pl.when(jnp.max(cnt_ge) > topk)
    def _():
        need = topk - cnt_gt

        def pos_step(it, p):
            cand = p + lax.shift_left(jnp.int32(1), seq_bits - 1 - it)
            cnt = count(lambda k_t, kt: (k_t == thr) & (kt * tk + k_iota < cand))
            return jnp.where(cnt < need, cand, p)

        jlim_ref[...] = lax.fori_loop(0, seq_bits, pos_step, jnp.zeros((1, tq), I32))

    m_ref[...] = jnp.full(m_ref.shape, -jnp.inf, F32)
    acc_ref[...] = jnp.zeros(acc_ref.shape, F32)

    @pl.when(nk % 2 == 1)
    def _():
        keys_ref[pl.ds(pl.multiple_of(nk * tk, tk), tk), :] = jnp.full((tk, tq), INT_MIN, I32)

    ta = 2 * tk
    a_iota = lax.broadcasted_iota(I32, (ta, tq), 0)

    def attn_tile(kt, _):
        off = pl.multiple_of(kt * ta, ta)
        k_t = keys_ref[pl.ds(off, ta), :]
        thr_b = thr_ref[...]
        mask = (k_t > thr_b) | ((k_t == thr_b) & (kt * ta + a_iota <= jlim_ref[...]))
        ckv_t = ckv_ref[pl.ds(off, ta), :]
        ckvt_t = ckvt_ref[:, pl.ds(off, ta)]
        logits = _dot(ckv_t, qabs_ref[0])
        for h in range(nh):
            s = jnp.where(mask, logits, -jnp.inf)
            if h + 1 < nh:
                logits = _dot(ckv_t, qabs_ref[h + 1])
            m_old = m_ref[h]
            m_new = jnp.maximum(m_old, jnp.max(s, axis=0, keepdims=True))
            m_use = jnp.where(m_new == -jnp.inf, 0.0, m_new)
            p = jnp.exp(s - m_use)
            alpha = jnp.exp(m_old - m_use)
            acc_ref[h] = alpha * acc_ref[h] + _dot(ckvt_t, p.astype(BF16))
            m_ref[h] = m_new
        return 0

    lax.fori_loop(0, (nk + 1) // 2, attn_tile, 0)

    for h in range(nh):
        acc = acc_ref[h]
        o_lat = (acc[:kvr] / acc[kvr:kvr + 1]).astype(BF16)
        ot_ref[h * dh:(h + 1) * dh, :] = _dot(wuv_ref[h], o_lat)
    o_ref[...] = ot_ref[...].T.astype(BF16)


def _dsa_attn(ki, ckv, ckvt, qi_t, qabs_t, wi_t, wuv_t):
    bsz, seq, di = ki.shape
    kvr = ckv.shape[2]
    nh, dh = wuv_t.shape[0], wuv_t.shape[1]
    nhi = qi_t.shape[1]
    tq = min(seq // 2, 256)
    tk = tq
    assert seq % (2 * tk) == 0
    topk = min(IDX_TOPK, seq // 4)
    body = functools.partial(_dsa_attn_body, tq=tq, tk=tk, nh=nh, nhi=nhi, dh=dh, kvr=kvr, topk=topk,
                             seq_bits=max(1, (seq - 1).bit_length()))
    return pl.pallas_call(
        body,
        grid=(bsz, seq // tq),
        in_specs=[pl.BlockSpec((None, seq, di), lambda b, q: (b, 0, 0)),
                  pl.BlockSpec((None, seq, kvr), lambda b, q: (b, 0, 0)),
                  pl.BlockSpec((None, kvr + ONES_ROWS, seq), lambda b, q: (b, 0, 0)),
                  pl.BlockSpec((None, nhi, di, tq), lambda b, q: (b, 0, 0, q)),
                  pl.BlockSpec((None, nh, kvr, tq), lambda b, q: (b, 0, 0, q)),
                  pl.BlockSpec((None, nhi, tq), lambda b, q: (b, 0, q)),
                  pl.BlockSpec((nh, dh, kvr), lambda b, q: (0, 0, 0))],
        out_specs=pl.BlockSpec((None, tq, nh * dh), lambda b, q: (b, q, 0)),
        out_shape=jax.ShapeDtypeStruct((bsz, seq, nh * dh), BF16),
        scratch_shapes=[pltpu.VMEM((seq, tq), I32),
                        pltpu.VMEM((seq, tq), BF16),
                        pltpu.VMEM((nh, kvr + ONES_ROWS, tq), F32),
                        pltpu.VMEM((nh, 1, tq), F32),
                        pltpu.VMEM((nh * dh, tq), F32),
                        pltpu.VMEM((1, tq), I32),
                        pltpu.VMEM((1, tq), I32)],
        compiler_params=_cp("parallel", "parallel"),
        name="dsa_attn",
    )(ki, ckv, ckvt, qi_t, qabs_t, wi_t, wuv_t)


GDN_GROUP = 16
GDN_INV_BATCH = 16


def _gdn_gate_body(m_ref, alog_ref, dtb_ref, gc_ref, beta_ref, gct_ref, *, ts, chunk, nhv, ng):
    x = m_ref[...]
    z = x + dtb_ref[...]
    softplus = jnp.maximum(z, 0.0) + jnp.log(1.0 + jnp.exp(-jnp.abs(z)))
    g = -jnp.exp(alog_ref[...]) * softplus
    pos = lax.broadcasted_iota(I32, (ts, LANES), 0) % chunk
    d = 1
    while d < chunk:
        g = g + jnp.where(pos >= d, pltpu.roll(g, d, axis=0), 0.0)
        d *= 2
    gct_ref[...] = g.T
    beta = jax.nn.sigmoid(x)
    for j in range(nhv // ng):
        gc_ref[:, j * LANES:(j + 1) * LANES] = pltpu.roll(g, (LANES - j * ng) % LANES, axis=1)
        beta_ref[:, j * LANES:(j + 1) * LANES] = pltpu.roll(beta, (2 * LANES - nhv - j * ng) % LANES, axis=1)


def _gdn_gates(proj, alog_p, dtb_p, bsz, seq, misc_blk, nhv):
    ts = min(seq, 512)
    ns = seq // ts
    ng = GDN_GROUP
    gw = nhv // ng * LANES
    body = functools.partial(_gdn_gate_body, ts=ts, chunk=GDN_CHUNK, nhv=nhv, ng=ng)
    return pl.pallas_call(
        body,
        grid=(bsz, ns),
        in_specs=[pl.BlockSpec((ts, LANES), lambda b, s: (b * ns + s, misc_blk)),
                  pl.BlockSpec((1, LANES), lambda b, s: (0, 0)),
                  pl.BlockSpec((1, LANES), lambda b, s: (0, 0))],
        out_specs=[pl.BlockSpec((ts, gw), lambda b, s: (b * ns + s, 0)),
                   pl.BlockSpec((ts, gw), lambda b, s: (b * ns + s, 0)),
                   pl.BlockSpec((None, LANES, ts), lambda b, s: (b, 0, s))],
        out_shape=[jax.ShapeDtypeStruct((bsz * seq, gw), F32),
                   jax.ShapeDtypeStruct((bsz * seq, gw), F32),
                   jax.ShapeDtypeStruct((bsz, LANES, seq), F32)],
        compiler_params=_cp("parallel", "parallel"),
        name="gdn_gates",
    )(proj, alog_p, dtb_p)


def _gdn_core_body(q_ref, k_ref, v_ref, gc_ref, beta_ref, gr_ref, z_ref, nw_ref, o_ref, *scratch,
                   ts, chunk, hd, ng, rep, gr_steps):
    st_refs = scratch[:ng]
    gcb_ref, bb_ref, a_ref, u_ref, w_ref, qkm_ref, rhs_ref, qd_ref, ktt_ref, egl_ref = scratch[ng:]
    nc = ts // chunk
    gr_base = (pl.program_id(2) % gr_steps) * nc

    @pl.when(pl.program_id(2) == 0)
    def _():
        for st_ref in st_refs:
            st_ref[...] = jnp.zeros(st_ref.shape, F32)

    for g in range(ng):
        gcb_ref[g] = jnp.broadcast_to(gc_ref[:, g:g + 1], (ts, LANES))
        bb_ref[g] = jnp.broadcast_to(beta_ref[:, g:g + 1], (ts, LANES))

    ri = lax.broadcasted_iota(I32, (chunk, chunk), 0)
    ci = lax.broadcasted_iota(I32, (chunk, chunk), 1)
    incl = ri >= ci
    strict = ri > ci
    eye = jnp.where(ri == ci, 1.0, 0.0).astype(F32)

    def prep(c, _):
        rows = pl.ds(pl.multiple_of(c * chunk, chunk), chunk)
        for qh in range(ng // rep):
            q = q_ref[rows, qh * hd:(qh + 1) * hd]
            k = k_ref[rows, qh * hd:(qh + 1) * hd]
            k16 = k.astype(BF16)
            kk = _dot_nt(k16, k16)
            qk = _dot_nt(q.astype(BF16), k16)
            for vh in range(rep):
                g = qh * rep + vh
                n = c * ng + g
                gcb = gcb_ref[g, rows, :]
                bb = bb_ref[g, rows, :]
                g_row = gr_ref[g, pl.ds(gr_base + c, 1), :]
                decay = jnp.where(incl, jnp.exp(gcb[:, :chunk] - g_row), 0.0)
                a_ref[n] = jnp.where(strict, kk * bb[:, :chunk] * decay, 0.0)
                qkm_ref[n] = (qk * decay).astype(BF16)
                eg = jnp.exp(gcb)
                rhs_ref[n, :, 0:hd] = (v_ref[rows, g * hd:(g + 1) * hd] * bb).astype(BF16)
                rhs_ref[n, :, hd:2 * hd] = (k * bb * eg).astype(BF16)
                qd_ref[n] = (q * eg).astype(BF16)
                g_last = gcb[chunk - 1:chunk, :]
                ktt_ref[n] = (k * jnp.exp(g_last - gcb)).T.astype(BF16)
                egl_ref[n] = jnp.broadcast_to(jnp.exp(g_last), (8, LANES))
        return 0

    lax.fori_loop(0, nc, prep, 0)

    n_dbl = max(1, (chunk - 1).bit_length()) - 1

    def invert(ib, _):
        ns_ = [ib * GDN_INV_BATCH + j for j in range(GDN_INV_BATCH)]
        pw = [a_ref[n] for n in ns_]
        tm = [eye - a for a in pw]
        pw = [_dot(x, x) for x in [a.astype(BF16) for a in pw]]
        for lvl in range(n_dbl):
            last = lvl == n_dbl - 1
            lhs = [(t if last else jnp.concatenate([t, x], axis=0)).astype(BF16) for t, x in zip(tm, pw)]
            prod = [_dot(y, x.astype(BF16)) for y, x in zip(lhs, pw)]
            tm = [t + y[:chunk] for t, y in zip(tm, prod)]
            pw = [y[chunk:] for y in prod]
        uw = [_dot(t.astype(BF16), rhs_ref[n]) for n, t in zip(ns_, tm)]
        for n, x in zip(ns_, uw):
            u_ref[n] = x[:, :hd]
            w_ref[n] = x[:, hd:].astype(BF16)
        return 0

    lax.fori_loop(0, nc * ng // GDN_INV_BATCH, invert, 0)

    nw = nw_ref[...]

    def step(c, _):
        rows = pl.ds(pl.multiple_of(c * chunk, chunk), chunk)
        heads = range(ng)
        ns_ = [c * ng + g for g in heads]
        state = [st_refs[g][...] for g in heads]
        st16 = [x.astype(BF16) for x in state]
        ws = [_dot(w_ref[n], x) for n, x in zip(ns_, st16)]
        qs = [_dot(qd_ref[n], x) for n, x in zip(ns_, st16)]
        vn16 = [(u_ref[n] - x).astype(BF16) for n, x in zip(ns_, ws)]
        o = [y + _dot(qkm_ref[n], x) for n, x, y in zip(ns_, vn16, qs)]
        ds = [_dot(ktt_ref[n], x) for n, x in zip(ns_, vn16)]
        for g in heads:
            st_refs[g][...] = state[g] * egl_ref[ns_[g]][0:1, :] + ds[g]
            zz = z_ref[rows, g * hd:(g + 1) * hd]
            o_ref[rows, g * hd:(g + 1) * hd] = (_rms(o[g], nw) * (zz * _sigmoid(zz))).astype(o_ref.dtype)
        return 0

    lax.fori_loop(0, nc, step, 0)


def _gdn_core(proj, gc, beta, gct4, norm_w, bsz, seq, nqk, nhv, hd, q_col, z_col):
    assert hd == LANES
    chunk = GDN_CHUNK
    ng = GDN_GROUP
    rep = nhv // nqk
    ts = min(seq, 256)
    ns = seq // ts
    n = (ts // chunk) * ng
    assert ng % rep == 0 and nhv % ng == 0 and n % GDN_INV_BATCH == 0
    qw, vw = (ng // rep) * hd, ng * hd
    k_col, v_col = q_col + nqk * hd, q_col + 2 * nqk * hd
    assert q_col % qw == 0 and k_col % qw == 0 and v_col % vw == 0 and z_col % vw == 0
    gr_steps = max(1, 8 * chunk // ts)
    assert ns % gr_steps == 0
    body = functools.partial(_gdn_core_body, ts=ts, chunk=chunk, hd=hd, ng=ng, rep=rep, gr_steps=gr_steps)
    row = lambda b, h, s: b * ns + s
    return pl.pallas_call(
        body,
        grid=(bsz, nhv // ng, ns),
        in_specs=[pl.BlockSpec((ts, qw), lambda b, h, s: (row(b, h, s), q_col // qw + h)),
                  pl.BlockSpec((ts, qw), lambda b, h, s: (row(b, h, s), k_col // qw + h)),
                  pl.BlockSpec((ts, vw), lambda b, h, s: (row(b, h, s), v_col // vw + h)),
                  pl.BlockSpec((ts, LANES), lambda b, h, s: (row(b, h, s), h)),
                  pl.BlockSpec((ts, LANES), lambda b, h, s: (row(b, h, s), h)),
                  pl.BlockSpec((None, ng, gr_steps * ts // chunk, chunk), lambda b, h, s: (b, h, s // gr_steps, 0)),
                  pl.BlockSpec((ts, vw), lambda b, h, s: (row(b, h, s), z_col // vw + h)),
                  pl.BlockSpec((1, hd), lambda b, h, s: (0, 0))],
        out_specs=pl.BlockSpec((ts, vw), lambda b, h, s: (row(b, h, s), h)),
        out_shape=jax.ShapeDtypeStruct((bsz * seq, nhv * hd), BF16),
        scratch_shapes=[pltpu.VMEM((hd, hd), F32) for _ in range(ng)] + [
                        pltpu.VMEM((ng, ts, LANES), F32),
                        pltpu.VMEM((ng, ts, LANES), F32),
                        pltpu.VMEM((n, chunk, chunk), F32),
                        pltpu.VMEM((n, chunk, hd), F32),
                        pltpu.VMEM((n, chunk, hd), BF16),
                        pltpu.VMEM((n, chunk, chunk), BF16),
                        pltpu.VMEM((n, chunk, 2 * hd), BF16),
                        pltpu.VMEM((n, chunk, hd), BF16),
                        pltpu.VMEM((n, hd, chunk), BF16),
                        pltpu.VMEM((n, 8, LANES), F32)],
        compiler_params=_cp("parallel", "parallel", "arbitrary"),
        name="gdn_core",
    )(proj, proj, proj, gc, beta, gct4, proj, norm_w.reshape(1, hd))


def _merge_body(a_ref, b_ref, wa_ref, wb_ref, ga_ref, gb_ref, o_ref):
    ya = _dot(a_ref[...], wa_ref[...])
    yb = _dot(b_ref[...], wb_ref[...])
    o_ref[...] = (_sigmoid(ga_ref[...]) * ya + _sigmoid(gb_ref[...]) * yb).astype(o_ref.dtype)


def _merge(o_a, o_b, wa16, wb16, proj, ga_col0, gb_col0):
    t, ka = o_a.shape
    kb = o_b.shape[1]
    d = wa16.shape[1]
    tm = min(t, 512)
    tn = _tile(d, 512)
    ga0, gb0 = ga_col0 // tn, gb_col0 // tn
    return pl.pallas_call(
        _merge_body,
        grid=(t // tm, d // tn),
        in_specs=[pl.BlockSpec((tm, ka), lambda i, j: (i, 0)),
                  pl.BlockSpec((tm, kb), lambda i, j: (i, 0)),
                  pl.BlockSpec((ka, tn), lambda i, j: (0, j)),
                  pl.BlockSpec((kb, tn), lambda i, j: (0, j)),
                  pl.BlockSpec((tm, tn), lambda i, j: (i, ga0 + j)),
                  pl.BlockSpec((tm, tn), lambda i, j: (i, gb0 + j))],
        out_specs=pl.BlockSpec((tm, tn), lambda i, j: (i, j)),
        out_shape=jax.ShapeDtypeStruct((t, d), BF16),
        compiler_params=_cp("parallel", "parallel"),
        name="merge",
    )(o_a, o_b, wa16, wb16, proj, proj)


def _outproj_router_body(m_ref, x_ref, wo_ref, g1_ref, nw_ref, sc_ref, sh_ref, wr_ref, br_ref,
                         x1_ref, h2_ref, te_ref, tg_ref, *, topk):
    x1 = x_ref[...] + g1_ref[...] * _dot(m_ref[...], wo_ref[...])
    x1_ref[...] = x1
    h2 = _rms(x1, nw_ref[...]) * (1.0 + sc_ref[...]) + sh_ref[...]
    h2_ref[...] = h2
    logits = _dot(h2.astype(BF16), wr_ref[...]) + br_ref[...]
    lane = lax.broadcasted_iota(I32, logits.shape, 1)
    te = jnp.zeros(logits.shape, I32)
    tg = jnp.zeros(logits.shape, F32)
    denom = jnp.zeros((logits.shape[0], 1), F32)
    m0 = None
    for kk in range(topk):
        mx = jnp.max(logits, axis=-1, keepdims=True)
        idx = jnp.min(jnp.where(logits == mx, lane, LANES), axis=-1, keepdims=True)
        if kk == 0:
            m0 = mx
        e = jnp.exp(mx - m0)
        denom = denom + e
        te = jnp.where(lane == kk, idx, te)
        tg = jnp.where(lane == kk, e, tg)
        logits = jnp.where(lane == idx, -jnp.inf, logits)
    te_ref[...] = te
    tg_ref[...] = tg / denom


def _outproj_router(merged, x2, wo16, gate1, nw, scale, shift, wr16, br_p, seq):
    t, d = x2.shape
    tm = min(seq, 256)
    per_b = seq // tm
    vec = lambda: pl.BlockSpec((None, 1, d), lambda i: (i // per_b, 0, 0))
    return pl.pallas_call(
        functools.partial(_outproj_router_body, topk=TOP_K),
        grid=(t // tm,),
        in_specs=[pl.BlockSpec((tm, d), lambda i: (i, 0)),
                  pl.BlockSpec((tm, d), lambda i: (i, 0)),
                  pl.BlockSpec((d, d), lambda i: (0, 0)),
                  vec(),
                  pl.BlockSpec((1, d), lambda i: (0, 0)),
                  vec(), vec(),
                  pl.BlockSpec((d, LANES), lambda i: (0, 0)),
                  pl.BlockSpec((1, LANES), lambda i: (0, 0))],
        out_specs=[pl.BlockSpec((tm, d), lambda i: (i, 0)),
                   pl.BlockSpec((tm, d), lambda i: (i, 0)),
                   pl.BlockSpec((tm, LANES), lambda i: (i, 0)),
                   pl.BlockSpec((tm, LANES), lambda i: (i, 0))],
        out_shape=[jax.ShapeDtypeStruct((t, d), F32),
                   jax.ShapeDtypeStruct((t, d), F32),
                   jax.ShapeDtypeStruct((t, LANES), I32),
                   jax.ShapeDtypeStruct((t, LANES), F32)],
        compiler_params=_cp("parallel"),
        name="outproj_router",
    )(merged, x2, wo16, gate1[:, None, :], nw.reshape(1, d), scale[:, None, :], shift[:, None, :], wr16, br_p)


def _moe_rank_body(te_ref, pos_ref, cnt_ref, carry_ref, *, tt, topk):
    @pl.when(pl.program_id(0) == 0)
    def _():
        carry_ref[...] = jnp.zeros(carry_ref.shape, F32)

    te = te_ref[...]
    lane = lax.broadcasted_iota(I32, (tt, LANES), 1)
    onehot = jnp.zeros((tt, LANES), F32)
    for kk in range(topk):
        onehot = onehot + jnp.where(lane == te[:, kk:kk + 1], 1.0, 0.0)
    r = lax.broadcasted_iota(I32, (tt, tt), 0)
    c = lax.broadcasted_iota(I32, (tt, tt), 1)
    below = jnp.where(r > c, 1.0, 0.0).astype(BF16)
    rank = _dot(below, onehot.astype(BF16)) + carry_ref[0:1, :]
    pos = jnp.zeros((tt, LANES), I32)
    for kk in range(topk):
        p = jnp.sum(jnp.where(lane == te[:, kk:kk + 1], rank, 0.0), axis=-1, keepdims=True)
        pos = jnp.where(lane == kk, p.astype(I32), pos)
    pos_ref[...] = pos
    carry_ref[...] = carry_ref[...] + jnp.sum(onehot, axis=0, keepdims=True)
    cnt_ref[...] = carry_ref[...]


def _moe_rank(te):
    t = te.shape[0]
    tt = min(t, 512)
    return pl.pallas_call(
        functools.partial(_moe_rank_body, tt=tt, topk=TOP_K),
        grid=(t // tt,),
        in_specs=[pl.BlockSpec((tt, LANES), lambda i: (i, 0))],
        out_specs=[pl.BlockSpec((tt, LANES), lambda i: (i, 0)),
                   pl.BlockSpec((8, LANES), lambda i: (0, 0))],
        out_shape=[jax.ShapeDtypeStruct((t, LANES), I32),
                   jax.ShapeDtypeStruct((8, LANES), F32)],
        scratch_shapes=[pltpu.VMEM((8, LANES), F32)],
        compiler_params=_cp("arbitrary"),
        name="moe_rank",
    )(te)


def _moe_dispatch_body(dest_ref, h_ref, zero_ref, x_hbm, sem, *, tt, topk):
    del zero_ref
    base = pl.program_id(0) * tt * topk

    def issue(t, _):
        for kk in range(topk):
            dst = x_hbm.at[pl.ds(dest_ref[base + t * topk + kk], 1)]
            pltpu.make_async_copy(h_ref.at[pl.ds(t, 1)], dst, sem).start()
        return 0

    lax.fori_loop(0, tt, issue, 0, unroll=2)
    for _ in range(topk):
        pltpu.make_async_copy(h_ref, x_hbm.at[pl.ds(0, tt)], sem).wait()


def _moe_dispatch(dest_flat, h2, n_rows):
    t, d = h2.shape
    tt = min(t, 128)
    grid_spec = pltpu.PrefetchScalarGridSpec(
        num_scalar_prefetch=1,
        grid=(t // tt,),
        in_specs=[pl.BlockSpec((tt, d), lambda i, dest: (i, 0)),
                  pl.BlockSpec(memory_space=pl.ANY)],
        out_specs=pl.BlockSpec(memory_space=pl.ANY),
        scratch_shapes=[pltpu.SemaphoreType.DMA(())])
    return pl.pallas_call(
        functools.partial(_moe_dispatch_body, tt=tt, topk=TOP_K),
        grid_spec=grid_spec,
        out_shape=jax.ShapeDtypeStruct((n_rows, d), F32),
        input_output_aliases={2: 0},
        compiler_params=_cp("arbitrary"),
        name="moe_dispatch",
    )(dest_flat, h2, jnp.zeros((n_rows, d), F32))


def _moe_gu_body(be_ref, na_ref, x_ref, wg_ref, wu_ref, bg_ref, bu_ref, act_ref, wg16_ref, wu16_ref):
    i = pl.program_id(1)
    prev = be_ref[jnp.maximum(i - 1, 0)]

    @pl.when((i == 0) | (be_ref[i] != prev))
    def _():
        wg16_ref[...] = wg_ref[...].astype(BF16)
        wu16_ref[...] = wu_ref[...].astype(BF16)

    @pl.when(i < na_ref[0])
    def _():
        x = x_ref[...].astype(BF16)
        g = jnp.minimum(_dot(x, wg16_ref[...]) + bg_ref[...], SWIGLU_LIMIT)
        u = jnp.clip(_dot(x, wu16_ref[...]) + bu_ref[...], -SWIGLU_LIMIT, SWIGLU_LIMIT)
        act_ref[...] = ((u + 1.0) * (g * _sigmoid(SWIGLU_ALPHA * g))).astype(act_ref.dtype)

    @pl.when(i >= na_ref[0])
    def _():
        act_ref[...] = jnp.zeros(act_ref.shape, act_ref.dtype)


def _moe_gu(block_e, n_act, x_rows, w_gu, b_gu):
    n_rows, d = x_rows.shape
    ne, _, ff2 = w_gu.shape
    ff = ff2 // 2
    bm = MOE_BM
    tn = _tile(ff, 1024)
    nt = ff // tn
    rowblk = lambda j, i, be, na: (jnp.minimum(i, na[0] - 1), 0)
    grid_spec = pltpu.PrefetchScalarGridSpec(
        num_scalar_prefetch=2,
        grid=(nt, n_rows // bm),
        in_specs=[pl.BlockSpec((bm, d), rowblk),
                  pl.BlockSpec((None, d, tn), lambda j, i, be, na: (be[i], 0, j)),
                  pl.BlockSpec((None, d, tn), lambda j, i, be, na: (be[i], 0, nt + j)),
                  pl.BlockSpec((None, 1, tn), lambda j, i, be, na: (be[i], 0, j)),
                  pl.BlockSpec((None, 1, tn), lambda j, i, be, na: (be[i], 0, nt + j))],
        out_specs=pl.BlockSpec((bm, tn), lambda j, i, be, na: (i, j)),
        scratch_shapes=[pltpu.VMEM((d, tn), BF16), pltpu.VMEM((d, tn), BF16)])
    return pl.pallas_call(
        _moe_gu_body,
        grid_spec=grid_spec,
        out_shape=jax.ShapeDtypeStruct((n_rows, ff), BF16),
        compiler_params=_cp("arbitrary", "arbitrary"),
        name="moe_gu",
    )(block_e, n_act, x_rows, w_gu, w_gu, b_gu.reshape(ne, 1, ff2), b_gu.reshape(ne, 1, ff2))


def _moe_down_body(be_ref, na_ref, a_ref, w_ref, b_ref, y_ref, w16_ref):
    i = pl.program_id(1)
    prev = be_ref[jnp.maximum(i - 1, 0)]

    @pl.when((i == 0) | (be_ref[i] != prev))
    def _():
        w16_ref[...] = w_ref[...].astype(BF16)

    @pl.when(i < na_ref[0])
    def _():
        y_ref[...] = _dot(a_ref[...], w16_ref[...]) + b_ref[...]

    @pl.when(i >= na_ref[0])
    def _():
        y_ref[...] = jnp.zeros(y_ref.shape, y_ref.dtype)


def _moe_down(block_e, n_act, act, w_down, b_down):
    n_rows, ff = act.shape
    ne, _, d = w_down.shape
    bm = MOE_BM
    tn = _tile(d, 2048)
    grid_spec = pltpu.PrefetchScalarGridSpec(
        num_scalar_prefetch=2,
        grid=(d // tn, n_rows // bm),
        in_specs=[pl.BlockSpec((bm, ff), lambda j, i, be, na: (jnp.minimum(i, na[0] - 1), 0)),
                  pl.BlockSpec((None, ff, tn), lambda j, i, be, na: (be[i], 0, j)),
                  pl.BlockSpec((None, 1, tn), lambda j, i, be, na: (be[i], 0, j))],
        out_specs=pl.BlockSpec((bm, tn), lambda j, i, be, na: (i, j)),
        scratch_shapes=[pltpu.VMEM((ff, tn), BF16)])
    return pl.pallas_call(
        _moe_down_body,
        grid_spec=grid_spec,
        out_shape=jax.ShapeDtypeStruct((n_rows, d), F32),
        compiler_params=_cp("arbitrary", "arbitrary"),
        name="moe_down",
    )(block_e, n_act, act, w_down, b_down.reshape(ne, 1, d))


def _moe_combine_body(dest_ref, x1_ref, tg_ref, g2_ref, fw_ref, y_hbm, o_ref, buf_ref, sem, *, tt, topk):
    base = pl.program_id(0) * tt * topk

    def issue(t, _):
        for kk in range(topk):
            src = y_hbm.at[pl.ds(dest_ref[base + t * topk + kk], 1)]
            pltpu.make_async_copy(src, buf_ref.at[kk, pl.ds(t, 1)], sem).start()
        return 0

    lax.fori_loop(0, tt, issue, 0, unroll=2)
    for kk in range(topk):
        pltpu.make_async_copy(y_hbm.at[pl.ds(0, tt)], buf_ref.at[kk], sem).wait()
    tg = tg_ref[...]
    y = tg[:, 0:1] * buf_ref[0]
    for kk in range(1, topk):
        y = y + tg[:, kk:kk + 1] * buf_ref[kk]
    o_ref[...] = _rms(x1_ref[...] + g2_ref[...] * y, fw_ref[...])


def _moe_combine(dest_flat, x1, tg, gate2, final_w, y_rows, seq):
    t, d = x1.shape
    tt = min(seq, 128)
    per_b = seq // tt
    grid_spec = pltpu.PrefetchScalarGridSpec(
        num_scalar_prefetch=1,
        grid=(t // tt,),
        in_specs=[pl.BlockSpec((tt, d), lambda i, dest: (i, 0)),
                  pl.BlockSpec((tt, LANES), lambda i, dest: (i, 0)),
                  pl.BlockSpec((None, 1, d), lambda i, dest: (i // per_b, 0, 0)),
                  pl.BlockSpec((1, d), lambda i, dest: (0, 0)),
                  pl.BlockSpec(memory_space=pl.ANY)],
        out_specs=pl.BlockSpec((tt, d), lambda i, dest: (i, 0)),
        scratch_shapes=[pltpu.VMEM((TOP_K, tt, d), F32), pltpu.SemaphoreType.DMA(())])
    return pl.pallas_call(
        functools.partial(_moe_combine_body, tt=tt, topk=TOP_K),
        grid_spec=grid_spec,
        out_shape=jax.ShapeDtypeStruct((t, d), F32),
        compiler_params=_cp("arbitrary"),
        name="moe_combine",
    )(dest_flat, x1, tg, gate2[:, None, :], final_w.reshape(1, d), y_rows)


def _pad_lanes(v, off=0, fill=0.0):
    out = jnp.full((1, LANES), fill, F32)
    return out.at[0, off:off + v.shape[0]].set(v.astype(F32))


def _layer(x2, c, bsz, seq, p, final_w):
    d = x2.shape[1]
    qr, nh, dh = p["w_uq"].shape
    kvr = p["w_uk"].shape[0]
    nhi, di = p["w_uqi"].shape[1:]
    hd = p["gdn_norm_w"].shape[0]
    nhv = p["a_log"].shape[0]
    v_w = nhv * hd
    qk_w = (p["conv_w"].shape[1] - v_w) // 2
    nqk = qk_w // hd
    ne = p["w_router"].shape[1]

    mod = _ada(c, p["ada_w"], p["ada_b"])
    shift1, scale1, gate1, shift2, scale2, gate2 = jnp.split(mod, 6, axis=-1)

    widths = (qr, kvr, di, nhi, qk_w, qk_w, v_w, nhv, nhv, v_w, d, d)
    offs = [0]
    for wd in widths:
        offs.append(offs[-1] + wd)
    col = lambda k: p["w_in"][:, offs[k]:offs[k + 1]]
    pad = jnp.zeros((d, LANES - 2 * nhv - nhi), F32)
    w_in = jnp.concatenate([col(4), col(5), col(6), col(9), col(10), col(11),
                            col(0), col(1), col(2), col(7), col(8), col(3), pad], axis=1).astype(BF16)
    lat_w = qr + kvr + di + LANES
    gq_col = 0
    z_col = gq_col + 2 * qk_w + v_w
    ga_col = z_col + v_w
    gb_col = ga_col + d
    lat_col = gb_col + d
    assert lat_col % lat_w == 0

    proj = _inproj(x2, p["norm1_w"], scale1, shift1, w_in, p["conv_w"], seq, gq_col, qk_w, hd)

    wuq_t = p["w_uq"].reshape(qr, nh * dh).T.astype(BF16)
    wuqi_t = p["w_uqi"].reshape(qr, nhi * di).T.astype(BF16)
    wuk_r = jnp.transpose(p["w_uk"], (1, 0, 2)).astype(BF16)
    wuv_t = jnp.transpose(p["w_uv"], (1, 2, 0)).astype(BF16)
    qabs_t, qi_t, ckv, ckvt, ki, wi_t = _dsa_prep(
        proj, bsz, seq, p["q_lat_norm_w"], p["kv_lat_norm_w"], p["idx_k_norm_w"], wuq_t, wuqi_t, wuk_r,
        (qr, kvr, di, nh, dh, nhi, 2 * nhv, lat_col // lat_w))
    o_a = _dsa_attn(ki, ckv, ckvt, qi_t, qabs_t, wi_t, wuv_t).reshape(bsz * seq, nh * dh)

    gc, beta, gct = _gdn_gates(proj, _pad_lanes(p["a_log"]), _pad_lanes(p["dt_bias"]), bsz, seq,
                               (lat_col + lat_w - LANES) // LANES, nhv)
    gct4 = gct.reshape(bsz, LANES, seq // GDN_CHUNK, GDN_CHUNK)
    o_b = _gdn_core(proj, gc, beta, gct4, p["gdn_norm_w"], bsz, seq, nqk, nhv, hd, gq_col, z_col)

    merged = _merge(o_a, o_b, p["w_branch_a"].astype(BF16), p["w_branch_b"].astype(BF16), proj, ga_col, gb_col)

    wr16 = jnp.zeros((d, LANES), F32).at[:, :ne].set(p["w_router"]).astype(BF16)
    br_p = _pad_lanes(p["b_router"], fill=-1e30)
    x1, h2, te, tg = _outproj_router(merged, x2, p["w_out"].astype(BF16), gate1, p["norm2_w"], scale2, shift2,
                                     wr16, br_p, seq)

    t = bsz * seq
    pos, cnt = _moe_rank(te)
    counts = cnt[0, :ne].astype(I32)
    padded = (counts + MOE_BM - 1) // MOE_BM * MOE_BM
    pad_end = jnp.cumsum(padded)
    pad_start = pad_end - padded
    dest = (pad_start[te[:, :TOP_K]] + pos[:, :TOP_K]).reshape(-1).astype(I32)
    n_blocks = -(-(t * TOP_K) // MOE_BM) + ne
    n_act = (pad_end[-1] // MOE_BM).astype(I32).reshape(1)
    blk = jnp.minimum(jnp.arange(n_blocks, dtype=I32), n_act[0] - 1) * MOE_BM
    block_e = jnp.minimum(jnp.sum(pad_end[None, :] <= blk[:, None], axis=1), ne - 1).astype(I32)
    x_rows = _moe_dispatch(dest, h2, n_blocks * MOE_BM)
    act = _moe_gu(block_e, n_act, x_rows, p["w_gu"], p["b_gu"])
    y_rows = _moe_down(block_e, n_act, act, p["w_down"], p["b_down"])
    return _moe_combine(dest, x1, tg, gate2, final_w, y_rows, seq)


@jax.jit
def kernel(x, c, ada_w, ada_b, norm1_w, w_in, q_lat_norm_w, kv_lat_norm_w, idx_k_norm_w, w_uq, w_uqi, w_uk, w_uv, conv_w, a_log, dt_bias, gdn_norm_w, w_branch_a, w_branch_b, w_out, norm2_w, w_router, b_router, w_gu, b_gu, w_down, b_down, final_norm_w):
    bsz, seq, d = x.shape
    stacked = dict(ada_w=ada_w, ada_b=ada_b, norm1_w=norm1_w, w_in=w_in, q_lat_norm_w=q_lat_norm_w,
                   kv_lat_norm_w=kv_lat_norm_w, idx_k_norm_w=idx_k_norm_w, w_uq=w_uq, w_uqi=w_uqi, w_uk=w_uk,
                   w_uv=w_uv, conv_w=conv_w, a_log=a_log, dt_bias=dt_bias, gdn_norm_w=gdn_norm_w,
                   w_branch_a=w_branch_a, w_branch_b=w_branch_b, w_out=w_out, norm2_w=norm2_w,
                   w_router=w_router, b_router=b_router, w_gu=w_gu, b_gu=b_gu, w_down=w_down, b_down=b_down)
    depth = ada_w.shape[0]
    assert depth == 1, "the final norm is fused into the last layer's combine kernel"
    x2 = x.reshape(bsz * seq, d)
    p = {k: v[0] for k, v in stacked.items()}
    out = _layer(x2, c, bsz, seq, p, final_norm_w)
    return out.reshape(bsz, seq, d)
```

```python
import functools

import jax
import jax.numpy as jnp
from jax import lax
from jax.experimental import pallas as pl
from jax.experimental.pallas import tpu as pltpu

F32 = jnp.float32
BF16 = jnp.bfloat16
I32 = jnp.int32

EPS = 1e-6
LANES = 128
VMEM_LIMIT = 56 * 1024 * 1024

IDX_TOPK = 256
GDN_CHUNK = 64
TOP_K = 4
SWIGLU_LIMIT = 7.0
SWIGLU_ALPHA = 1.702
MOE_BM = 512
ONES_ROWS = 16
LOG2E = 1.4426950408889634
INT_MIN = -(2 ** 31)
INT_MAX = 2 ** 31 - 1


def _cp(*sem):
    return pltpu.CompilerParams(dimension_semantics=sem, vmem_limit_bytes=VMEM_LIMIT)


def _tile(n, pref):
    if n <= pref:
        return n
    t = pref - pref % LANES
    while n % t:
        t -= LANES
    return t


def _rms(x, w):
    return x * lax.rsqrt(jnp.mean(x * x, axis=-1, keepdims=True) + EPS) * w


def _sigmoid(x):
    return 0.5 * jnp.tanh(0.5 * x) + 0.5


def _dot(a, b):
    return jnp.dot(a, b, preferred_element_type=F32)


def _dot_nt(a, b):
    return lax.dot_general(a, b, (((1,), (1,)), ((), ())), preferred_element_type=F32)


def _dot_tn(a, b):
    return lax.dot_general(a, b, (((0,), (0,)), ((), ())), preferred_element_type=F32)


def _ada_body(ct_ref, w_ref, b_ref, o_ref, *, nb):
    ct = ct_ref[...]
    s = ct * jax.nn.sigmoid(ct)
    w = w_ref[...]
    for b in range(nb):
        o_ref[b:b + 1, :] = jnp.sum(w * s[:, b:b + 1], axis=0, keepdims=True) + b_ref[...]


def _ada(c, w, b):
    nb, d = c.shape
    n = w.shape[1]
    tn = _tile(n, 1024)
    return pl.pallas_call(
        functools.partial(_ada_body, nb=nb),
        grid=(n // tn,),
        in_specs=[pl.BlockSpec((d, nb), lambda j: (0, 0)),
                  pl.BlockSpec((d, tn), lambda j: (0, j)),
                  pl.BlockSpec((1, tn), lambda j: (0, j))],
        out_specs=pl.BlockSpec((nb, tn), lambda j: (0, j)),
        out_shape=jax.ShapeDtypeStruct((nb, n), F32),
        compiler_params=_cp("parallel"),
        name="ada",
    )(c.T, w, b.reshape(1, n))


def _inproj_body(x_ref, nw_ref, sc_ref, sh_ref, w_ref, cw_ref, o_ref, h_ref, carry_ref, *ubuf_refs,
                 tm, tn, sw, rc, hd, per_b, c0, n_qk, n_conv, q_scale):
    i = pl.program_id(0)
    j = pl.program_id(1)

    @pl.when(j == 0)
    def _():
        h = _rms(x_ref[...], nw_ref[...]) * (1.0 + sc_ref[...]) + sh_ref[...]
        h_ref[...] = h.astype(BF16)

    cj = j - c0
    in_conv = (cj >= 0) & (cj < n_conv)

    @pl.when(jnp.logical_not(in_conv))
    def _():
        o_ref[...] = _dot(h_ref[...], w_ref[...])

    @pl.when(in_conv)
    def _():
        taps = cw_ref.shape[0]

        @pl.when(i % per_b == 0)
        def _():
            carry_ref[cj] = jnp.zeros((8, tn), F32)

        prev = carry_ref[cj]
        scale = jnp.where(cj < n_qk, q_scale, 1.0)
        normalise = cj < 2 * n_qk
        tails = []
        for s, ubuf_ref in enumerate(ubuf_refs):
            cols = slice(s * sw, (s + 1) * sw)
            w = cw_ref[:, cols]
            u = _dot(h_ref[...], w_ref[:, cols])
            ubuf_ref[0:8, :] = prev[:, cols]
            ubuf_ref[8:, :] = u
            tails.append(u[tm - 8:, :])
            for r in range(tm // rc):
                ext = ubuf_ref[r * rc:(r + 1) * rc + 8, :]
                y = ext[8:] * w[taps - 1:taps, :]
                for k in range(1, taps):
                    y = y + pltpu.roll(ext, k, axis=0)[8:] * w[taps - 1 - k:taps - k, :]
                y = y * _sigmoid(y)
                for hh in range(sw // hd):
                    x = y[:, hh * hd:(hh + 1) * hd]
                    inv = lax.rsqrt(jnp.sum(x * x, axis=-1, keepdims=True) + EPS) * scale
                    c_lo = s * sw + hh * hd
                    o_ref[r * rc:(r + 1) * rc, c_lo:c_lo + hd] = x * jnp.where(normalise, inv, 1.0)
        carry_ref[cj] = jnp.concatenate(tails, axis=1)


def _inproj(x2, nw, scale, shift, w16, conv_w, seq, conv_col0, qk_w, hd):
    t, d = x2.shape
    n = w16.shape[1]
    tm = min(seq, 1024)
    tn = _tile(n, 1024)
    per_b = seq // tm
    conv_ch = conv_w.shape[1]
    assert conv_col0 % tn == 0 and qk_w % tn == 0 and conv_ch % tn == 0 and tn % hd == 0
    c0, n_qk, n_conv = conv_col0 // tn, qk_w // tn, conv_ch // tn
    rc = min(tm, 64)
    sw = min(tn, 256)
    body = functools.partial(_inproj_body, tm=tm, tn=tn, sw=sw, rc=rc, hd=hd, per_b=per_b, c0=c0, n_qk=n_qk,
                             n_conv=n_conv, q_scale=hd ** -0.5)
    return pl.pallas_call(
        body,
        grid=(t // tm, n // tn),
        in_specs=[pl.BlockSpec((tm, d), lambda i, j: (i, 0)),
                  pl.BlockSpec((1, d), lambda i, j: (0, 0)),
                  pl.BlockSpec((None, 1, d), lambda i, j: (i // per_b, 0, 0)),
                  pl.BlockSpec((None, 1, d), lambda i, j: (i // per_b, 0, 0)),
                  pl.BlockSpec((d, tn), lambda i, j: (0, j)),
                  pl.BlockSpec((conv_w.shape[0], tn), lambda i, j: (0, jnp.clip(j - c0, 0, n_conv - 1)))],
        out_specs=pl.BlockSpec((tm, tn), lambda i, j: (i, j)),
        out_shape=jax.ShapeDtypeStruct((t, n), F32),
        scratch_shapes=[pltpu.VMEM((tm, d), BF16), pltpu.VMEM((n_conv, 8, tn), F32)]
        + [pltpu.VMEM((tm + 8, sw), F32) for _ in range(tn // sw)],
        compiler_params=_cp("arbitrary", "arbitrary"),
        name="inproj",
    )(x2, nw.reshape(1, d), scale[:, None, :], shift[:, None, :], w16, conv_w)


def _dsa_prep_body(lat_ref, qnw_ref, kvnw_ref, knw_ref, wuq_ref, wuqi_ref, wuk_ref,
                   qabs_ref, qi_ref, ckv_ref, ckvt_ref, ki_ref, wi_ref,
                   *, qr, kvr, di, nh, dh, nhi, q_scale, i_scale, w_scale, wi_off):
    lat = lat_ref[...]
    cq_t = _rms(lat[:, :qr], qnw_ref[...]).T.astype(BF16)
    q_t = _dot(wuq_ref[...], cq_t)
    for h in range(nh):
        q_h = q_t[h * dh:(h + 1) * dh].astype(BF16)
        qabs_ref[h] = (_dot(wuk_ref[h], q_h) * q_scale).astype(BF16)
    qi_t = _dot(wuqi_ref[...], cq_t) * i_scale
    for h in range(nhi):
        qi_ref[h] = qi_t[h * di:(h + 1) * di].astype(BF16)
    ckv = _rms(lat[:, qr:qr + kvr], kvnw_ref[...])
    ckv_ref[...] = ckv.astype(BF16)
    ckvt_ref[0:kvr, :] = ckv.T.astype(BF16)
    ckvt_ref[kvr:, :] = jnp.ones((ONES_ROWS, ckvt_ref.shape[1]), BF16)
    ki_ref[...] = _rms(lat[:, qr + kvr:qr + kvr + di], knw_ref[...]).astype(BF16)
    misc_t = lat[:, qr + kvr + di:].T
    wi_ref[...] = misc_t[wi_off:wi_off + nhi] * w_scale


def _dsa_prep(proj, bsz, seq, qnw, kvnw, knw, wuq_t, wuqi_t, wuk_r, dims):
    qr, kvr, di, nh, dh, nhi, wi_off, lat_blk = dims
    tm = min(seq, 512)
    ns = seq // tm
    lat_w = qr + kvr + di + LANES
    body = functools.partial(
        _dsa_prep_body, qr=qr, kvr=kvr, di=di, nh=nh, dh=dh, nhi=nhi,
        q_scale=dh ** -0.5 * LOG2E, i_scale=di ** -0.5, w_scale=nhi ** -0.5, wi_off=wi_off)
    full = lambda shape: pl.BlockSpec(shape, lambda b, s: (0,) * len(shape))
    return pl.pallas_call(
        body,
        grid=(bsz, ns),
        in_specs=[pl.BlockSpec((tm, lat_w), lambda b, s: (b * ns + s, lat_blk)),
                  full((1, qr)), full((1, kvr)), full((1, di)),
                  full((nh * dh, qr)), full((nhi * di, qr)), full((nh, kvr, dh))],
        out_specs=[pl.BlockSpec((None, nh, kvr, tm), lambda b, s: (b, 0, 0, s)),
                   pl.BlockSpec((None, nhi, di, tm), lambda b, s: (b, 0, 0, s)),
                   pl.BlockSpec((None, tm, kvr), lambda b, s: (b, s, 0)),
                   pl.BlockSpec((None, kvr + ONES_ROWS, tm), lambda b, s: (b, 0, s)),
                   pl.BlockSpec((None, tm, di), lambda b, s: (b, s, 0)),
                   pl.BlockSpec((None, nhi, tm), lambda b, s: (b, 0, s))],
        out_shape=[jax.ShapeDtypeStruct((bsz, nh, kvr, seq), BF16),
                   jax.ShapeDtypeStruct((bsz, nhi, di, seq), BF16),
                   jax.ShapeDtypeStruct((bsz, seq, kvr), BF16),
                   jax.ShapeDtypeStruct((bsz, kvr + ONES_ROWS, seq), BF16),
                   jax.ShapeDtypeStruct((bsz, seq, di), BF16),
                   jax.ShapeDtypeStruct((bsz, nhi, seq), F32)],
        compiler_params=_cp("parallel", "parallel"),
        name="dsa_prep",
    )(proj, qnw.reshape(1, qr), kvnw.reshape(1, kvr), knw.reshape(1, di), wuq_t, wuqi_t, wuk_r)


def _dsa_attn_body(ki_ref, ckv_ref, ckvt_ref, qi_ref, qabs_ref, wi_ref, wuv_ref, o_ref,
                   keys_ref, k16_ref, acc_ref, m_ref, ot_ref, thr_ref, jlim_ref,
                   *, tq, tk, nh, nhi, dh, kvr, topk, seq_bits):
    qb = pl.program_id(1)
    nk = ((qb + 1) * tq + tk - 1) // tk
    q_pos = qb * tq + lax.broadcasted_iota(I32, (tk, tq), 1)
    k_iota = lax.broadcasted_iota(I32, (tk, tq), 0)

    def score_tile(kt, _):
        ki_t = ki_ref[pl.ds(pl.multiple_of(kt * tk, tk), tk), :]
        acc = jnp.zeros((tk, tq), F32)
        for h in range(nhi):
            rel = _dot(ki_t, qi_ref[h])
            acc = acc + wi_ref[h:h + 1, :] * jnp.maximum(rel, 0.0)
        acc = jnp.where(acc == 0.0, 0.0, acc)
        bits = lax.bitcast_convert_type(acc, I32)
        causal = kt * tk + k_iota <= q_pos
        key = jnp.where(bits < 0, bits ^ INT_MAX, bits)
        keys_ref[pl.ds(pl.multiple_of(kt * tk, tk), tk), :] = jnp.where(causal, key, INT_MIN)
        top = lax.bitcast_convert_type(bits & -65536, F32)
        k16_ref[pl.ds(pl.multiple_of(kt * tk, tk), tk), :] = jnp.where(causal, top, jnp.nan).astype(BF16)
        return 0

    lax.fori_loop(0, nk, score_tile, 0)

    one16 = jnp.ones((tk, tq), BF16)
    zero16 = jnp.zeros((tk, tq), BF16)

    def count16(cand_bf):
        cand_b = jnp.broadcast_to(cand_bf, (tk, tq))

        def tile(kt, c):
            k_t = k16_ref[pl.ds(pl.multiple_of(kt * tk, tk), tk), :]
            hit = jnp.where(k_t >= cand_b, one16, zero16)
            part = hit[0:16]
            for r in range(1, tk // 16):
                part = part + hit[r * 16:(r + 1) * 16]
            return c + part.astype(F32)
        c16 = lax.fori_loop(0, nk, tile, jnp.zeros((16, tq), F32))
        return jnp.sum(c16, axis=0, keepdims=True)

    def hi_step(it, hi):
        cand = hi + lax.shift_left(jnp.int32(1), 15 - it)
        pattern = jnp.where(cand >= 0, cand, cand ^ 0x7FFF) & 0xFFFF
        cand_bf = lax.bitcast_convert_type(lax.shift_left(pattern, 16), F32).astype(BF16)
        return jnp.where(count16(cand_bf) >= topk, cand, hi)

    hi16 = lax.fori_loop(0, 16, hi_step, jnp.full((1, tq), -(2 ** 15), I32))

    def count(pred_fn):
        def tile(kt, c):
            k_t = keys_ref[pl.ds(pl.multiple_of(kt * tk, tk), tk), :]
            hit = jnp.where(pred_fn(k_t, kt), 1, 0).astype(I32)
            return c + jnp.sum(hit.reshape(tk // 8, 8, tq), axis=0)
        c8 = lax.fori_loop(0, nk, tile, jnp.zeros((8, tq), I32))
        return jnp.sum(c8, axis=0, keepdims=True)

    def bit_cond(carry):
        it, _, settled = carry
        return (it < 32) & (jnp.min(settled) == 0)

    def bit_step(carry):
        it, thr, settled = carry
        for _ in range(4):
            cand = thr ^ lax.shift_left(jnp.int32(1), 31 - it)
            cnt = count(lambda k_t, kt: k_t >= cand)
            take = (cnt >= topk) & (settled == 0)
            settled = jnp.where(take & (cnt == topk), 1, settled)
            thr = jnp.where(take, cand, thr)
            it = it + 1
        return it, thr, settled

    _, thr, _ = lax.while_loop(
        bit_cond, bit_step, (jnp.int32(16), lax.shift_left(hi16, 16), jnp.zeros((1, tq), I32)))
    thr = jnp.maximum(thr, INT_MIN + 1)
    cnt_gt = count(lambda k_t, kt: k_t > thr)
    cnt_ge = count(lambda k_t, kt: k_t >= thr)
    thr_ref[...] = thr
    jlim_ref[...] = jnp.full((1, tq), INT_MAX, I32)

    @pl.when(jnp.max(cnt_ge) > topk)
    def _():
        need = topk - cnt_gt

        def pos_step(it, p):
            cand = p + lax.shift_left(jnp.int32(1), seq_bits - 1 - it)
            cnt = count(lambda k_t, kt: (k_t == thr) & (kt * tk + k_iota < cand))
            return jnp.where(cnt < need, cand, p)

        jlim_ref[...] = lax.fori_loop(0, seq_bits, pos_step, jnp.zeros((1, tq), I32))

    m_ref[...] = jnp.full(m_ref.shape, -jnp.inf, F32)
    acc_ref[...] = jnp.zeros(acc_ref.shape, F32)

    @pl.when(nk % 2 == 1)
    def _():
        keys_ref[pl.ds(pl.multiple_of(nk * tk, tk), tk), :] = jnp.full((tk, tq), INT_MIN, I32)

    ta = 2 * tk
    a_iota = lax.broadcasted_iota(I32, (ta, tq), 0)

    def attn_tile(kt, _):
        off = pl.multiple_of(kt * ta, ta)
        k_t = keys_ref[pl.ds(off, ta), :]
        thr_b = thr_ref[...]
        mask = (k_t > thr_b) | ((k_t == thr_b) & (kt * ta + a_iota <= jlim_ref[...]))
        bias = jnp.where(mask, 0.0, -jnp.inf)
        ckv_t = ckv_ref[pl.ds(off, ta), :]
        ckvt_t = ckvt_ref[:, pl.ds(off, ta)]
        logits = _dot(ckv_t, qabs_ref[0])
        for h in range(nh):
            s = logits + bias
            if h + 1 < nh:
                logits = _dot(ckv_t, qabs_ref[h + 1])
            m_old = m_ref[h]
            m_new = jnp.maximum(m_old, jnp.max(s, axis=0, keepdims=True))
            m_use = jnp.where(m_new == -jnp.inf, 0.0, m_new)
            p = jnp.exp2(s - m_use)
            alpha = jnp.exp2(m_old - m_use)
            acc_ref[h] = alpha * acc_ref[h] + _dot(ckvt_t, p.astype(BF16))
            m_ref[h] = m_new
        return 0

    lax.fori_loop(0, (nk + 1) // 2, attn_tile, 0)

    for h in range(nh):
        acc = acc_ref[h]
        o_lat = (acc[:kvr] / acc[kvr:kvr + 1]).astype(BF16)
        ot_ref[h * dh:(h + 1) * dh, :] = _dot(wuv_ref[h], o_lat)
    o_ref[...] = ot_ref[...].T.astype(BF16)


def _dsa_attn(ki, ckv, ckvt, qi_t, qabs_t, wi_t, wuv_t):
    bsz, seq, di = ki.shape
    kvr = ckv.shape[2]
    nh, dh = wuv_t.shape[0], wuv_t.shape[1]
    nhi = qi_t.shape[1]
    tq = min(seq // 2, 256)
    tk = tq
    assert seq % (2 * tk) == 0
    topk = min(IDX_TOPK, seq // 4)
    body = functools.partial(_dsa_attn_body, tq=tq, tk=tk, nh=nh, nhi=nhi, dh=dh, kvr=kvr, topk=topk,
                             seq_bits=max(1, (seq - 1).bit_length()))
    return pl.pallas_call(
        body,
        grid=(bsz, seq // tq),
        in_specs=[pl.BlockSpec((None, seq, di), lambda b, q: (b, 0, 0)),
                  pl.BlockSpec((None, seq, kvr), lambda b, q: (b, 0, 0)),
                  pl.BlockSpec((None, kvr + ONES_ROWS, seq), lambda b, q: (b, 0, 0)),
                  pl.BlockSpec((None, nhi, di, tq), lambda b, q: (b, 0, 0, q)),
                  pl.BlockSpec((None, nh, kvr, tq), lambda b, q: (b, 0, 0, q)),
                  pl.BlockSpec((None, nhi, tq), lambda b, q: (b, 0, q)),
                  pl.BlockSpec((nh, dh, kvr), lambda b, q: (0, 0, 0))],
        out_specs=pl.BlockSpec((None, tq, nh * dh), lambda b, q: (b, q, 0)),
        out_shape=jax.ShapeDtypeStruct((bsz, seq, nh * dh), BF16),
        scratch_shapes=[pltpu.VMEM((seq, tq), I32),
                        pltpu.VMEM((seq, tq), BF16),
                        pltpu.VMEM((nh, kvr + ONES_ROWS, tq), F32),
                        pltpu.VMEM((nh, 1, tq), F32),
                        pltpu.VMEM((nh * dh, tq), F32),
                        pltpu.VMEM((1, tq), I32),
                        pltpu.VMEM((1, tq), I32)],
        compiler_params=_cp("parallel", "parallel"),
        name="dsa_attn",
    )(ki, ckv, ckvt, qi_t, qabs_t, wi_t, wuv_t)


GDN_GROUP = 16
GDN_INV_BATCH = 16


def _gdn_gate_body(m_ref, alog_ref, dtb_ref, gc_ref, beta_ref, gct_ref, *, ts, chunk, nhv, ng):
    x = m_ref[...]
    z = x + dtb_ref[...]
    softplus = jnp.maximum(z, 0.0) + jnp.log(1.0 + jnp.exp(-jnp.abs(z)))
    g = -jnp.exp(alog_ref[...]) * softplus
    pos = lax.broadcasted_iota(I32, (ts, LANES), 0) % chunk
    d = 1
    while d < chunk:
        g = g + jnp.where(pos >= d, pltpu.roll(g, d, axis=0), 0.0)
        d *= 2
    gct_ref[...] = g.T
    beta = jax.nn.sigmoid(x)
    for j in range(nhv // ng):
        gc_ref[:, j * LANES:(j + 1) * LANES] = pltpu.roll(g, (LANES - j * ng) % LANES, axis=1)
        beta_ref[:, j * LANES:(j + 1) * LANES] = pltpu.roll(beta, (2 * LANES - nhv - j * ng) % LANES, axis=1)


def _gdn_gates(proj, alog_p, dtb_p, bsz, seq, misc_blk, nhv):
    ts = min(seq, 512)
    ns = seq // ts
    ng = GDN_GROUP
    gw = nhv // ng * LANES
    body = functools.partial(_gdn_gate_body, ts=ts, chunk=GDN_CHUNK, nhv=nhv, ng=ng)
    return pl.pallas_call(
        body,
        grid=(bsz, ns),
        in_specs=[pl.BlockSpec((ts, LANES), lambda b, s: (b * ns + s, misc_blk)),
                  pl.BlockSpec((1, LANES), lambda b, s: (0, 0)),
                  pl.BlockSpec((1, LANES), lambda b, s: (0, 0))],
        out_specs=[pl.BlockSpec((ts, gw), lambda b, s: (b * ns + s, 0)),
                   pl.BlockSpec((ts, gw), lambda b, s: (b * ns + s, 0)),
                   pl.BlockSpec((None, LANES, ts), lambda b, s: (b, 0, s))],
        out_shape=[jax.ShapeDtypeStruct((bsz * seq, gw), F32),
                   jax.ShapeDtypeStruct((bsz * seq, gw), F32),
                   jax.ShapeDtypeStruct((bsz, LANES, seq), F32)],
        compiler_params=_cp("parallel", "parallel"),
        name="gdn_gates",
    )(proj, alog_p, dtb_p)


def _gdn_core_body(q_ref, k_ref, v_ref, gc_ref, beta_ref, gr_ref, z_ref, nw_ref, o_ref, *scratch,
                   ts, chunk, hd, ng, rep, gr_steps):
    st_refs = scratch[:ng]
    gcb_ref, bb_ref, a_ref, u_ref, w_ref, qkm_ref, rhs_ref, qd_ref, ktt_ref, egl_ref = scratch[ng:]
    nc = ts // chunk
    gr_base = (pl.program_id(2) % gr_steps) * nc

    @pl.when(pl.program_id(2) == 0)
    def _():
        for st_ref in st_refs:
            st_ref[...] = jnp.zeros(st_ref.shape, F32)

    for g in range(ng):
        gcb_ref[g] = jnp.broadcast_to(gc_ref[:, g:g + 1], (ts, LANES))
        bb_ref[g] = jnp.broadcast_to(beta_ref[:, g:g + 1], (ts, LANES))

    ri = lax.broadcasted_iota(I32, (chunk, chunk), 0)
    ci = lax.broadcasted_iota(I32, (chunk, chunk), 1)
    incl = ri >= ci
    strict = ri > ci
    eye = jnp.where(ri == ci, 1.0, 0.0).astype(F32)

    def prep(c, _):
        rows = pl.ds(pl.multiple_of(c * chunk, chunk), chunk)
        for qh in range(ng // rep):
            q = q_ref[rows, qh * hd:(qh + 1) * hd]
            k = k_ref[rows, qh * hd:(qh + 1) * hd]
            k16 = k.astype(BF16)
            kk = _dot_nt(k16, k16)
            qk = _dot_nt(q.astype(BF16), k16)
            for vh in range(rep):
                g = qh * rep + vh
                n = c * ng + g
                gcb = gcb_ref[g, rows, :]
                bb = bb_ref[g, rows, :]
                g_row = gr_ref[g, pl.ds(gr_base + c, 1), :]
                decay = jnp.where(incl, jnp.exp(gcb[:, :chunk] - g_row), 0.0)
                a_ref[n] = jnp.where(strict, kk * bb[:, :chunk] * decay, 0.0)
                qkm_ref[n] = (qk * decay).astype(BF16)
                eg = jnp.exp(gcb)
                rhs_ref[n, :, 0:hd] = (v_ref[rows, g * hd:(g + 1) * hd] * bb).astype(BF16)
                rhs_ref[n, :, hd:2 * hd] = (k * bb * eg).astype(BF16)
                qd_ref[n] = (q * eg).astype(BF16)
                g_last = gcb[chunk - 1:chunk, :]
                ktt_ref[n] = (k * jnp.exp(g_last - gcb)).T.astype(BF16)
                egl_ref[n] = jnp.broadcast_to(jnp.exp(g_last), (8, LANES))
        return 0

    lax.fori_loop(0, nc, prep, 0)

    n_dbl = max(1, (chunk - 1).bit_length()) - 1

    def invert(ib, _):
        ns_ = [ib * GDN_INV_BATCH + j for j in range(GDN_INV_BATCH)]
        pw = [a_ref[n] for n in ns_]
        tm = [eye - a for a in pw]
        pw = [_dot(x, x) for x in [a.astype(BF16) for a in pw]]
        for lvl in range(n_dbl):
            last = lvl == n_dbl - 1
            lhs = [(t if last else jnp.concatenate([t, x], axis=0)).astype(BF16) for t, x in zip(tm, pw)]
            prod = [_dot(y, x.astype(BF16)) for y, x in zip(lhs, pw)]
            tm = [t + y[:chunk] for t, y in zip(tm, prod)]
            pw = [y[chunk:] for y in prod]
        uw = [_dot(t.astype(BF16), rhs_ref[n]) for n, t in zip(ns_, tm)]
        for n, x in zip(ns_, uw):
            u_ref[n] = x[:, :hd]
            w_ref[n] = x[:, hd:].astype(BF16)
        return 0

    lax.fori_loop(0, nc * ng // GDN_INV_BATCH, invert, 0)

    nw = nw_ref[...]

    def step(c, _):
        rows = pl.ds(pl.multiple_of(c * chunk, chunk), chunk)
        heads = range(ng)
        ns_ = [c * ng + g for g in heads]
        state = [st_refs[g][...] for g in heads]
        st16 = [x.astype(BF16) for x in state]
        ws = [_dot(w_ref[n], x) for n, x in zip(ns_, st16)]
        qs = [_dot(qd_ref[n], x) for n, x in zip(ns_, st16)]
        vn16 = [(u_ref[n] - x).astype(BF16) for n, x in zip(ns_, ws)]
        o = [y + _dot(qkm_ref[n], x) for n, x, y in zip(ns_, vn16, qs)]
        ds = [_dot(ktt_ref[n], x) for n, x in zip(ns_, vn16)]
        for g in heads:
            st_refs[g][...] = state[g] * egl_ref[ns_[g]][0:1, :] + ds[g]
            zz = z_ref[rows, g * hd:(g + 1) * hd]
            o_ref[rows, g * hd:(g + 1) * hd] = (_rms(o[g], nw) * (zz * _sigmoid(zz))).astype(o_ref.dtype)
        return 0

    lax.fori_loop(0, nc, step, 0)


def _gdn_core(proj, gc, beta, gct4, norm_w, bsz, seq, nqk, nhv, hd, q_col, z_col):
    assert hd == LANES
    chunk = GDN_CHUNK
    ng = GDN_GROUP
    rep = nhv // nqk
    ts = min(seq, 256)
    ns = seq // ts
    n = (ts // chunk) * ng
    assert ng % rep == 0 and nhv % ng == 0 and n % GDN_INV_BATCH == 0
    qw, vw = (ng // rep) * hd, ng * hd
    k_col, v_col = q_col + nqk * hd, q_col + 2 * nqk * hd
    assert q_col % qw == 0 and k_col % qw == 0 and v_col % vw == 0 and z_col % vw == 0
    gr_steps = max(1, 8 * chunk // ts)
    assert ns % gr_steps == 0
    body = functools.partial(_gdn_core_body, ts=ts, chunk=chunk, hd=hd, ng=ng, rep=rep, gr_steps=gr_steps)
    row = lambda b, h, s: b * ns + s
    return pl.pallas_call(
        body,
        grid=(bsz, nhv // ng, ns),
        in_specs=[pl.BlockSpec((ts, qw), lambda b, h, s: (row(b, h, s), q_col // qw + h)),
                  pl.BlockSpec((ts, qw), lambda b, h, s: (row(b, h, s), k_col // qw + h)),
                  pl.BlockSpec((ts, vw), lambda b, h, s: (row(b, h, s), v_col // vw + h)),
                  pl.BlockSpec((ts, LANES), lambda b, h, s: (row(b, h, s), h)),
                  pl.BlockSpec((ts, LANES), lambda b, h, s: (row(b, h, s), h)),
                  pl.BlockSpec((None, ng, gr_steps * ts // chunk, chunk), lambda b, h, s: (b, h, s // gr_steps, 0)),
                  pl.BlockSpec((ts, vw), lambda b, h, s: (row(b, h, s), z_col // vw + h)),
                  pl.BlockSpec((1, hd), lambda b, h, s: (0, 0))],
        out_specs=pl.BlockSpec((ts, vw), lambda b, h, s: (row(b, h, s), h)),
        out_shape=jax.ShapeDtypeStruct((bsz * seq, nhv * hd), BF16),
        scratch_shapes=[pltpu.VMEM((hd, hd), F32) for _ in range(ng)] + [
                        pltpu.VMEM((ng, ts, LANES), F32),
                        pltpu.VMEM((ng, ts, LANES), F32),
                        pltpu.VMEM((n, chunk, chunk), F32),
                        pltpu.VMEM((n, chunk, hd), F32),
                        pltpu.VMEM((n, chunk, hd), BF16),
                        pltpu.VMEM((n, chunk, chunk), BF16),
                        pltpu.VMEM((n, chunk, 2 * hd), BF16),
                        pltpu.VMEM((n, chunk, hd), BF16),
                        pltpu.VMEM((n, hd, chunk), BF16),
                        pltpu.VMEM((n, 8, LANES), F32)],
        compiler_params=_cp("parallel", "parallel", "arbitrary"),
        name="gdn_core",
    )(proj, proj, proj, gc, beta, gct4, proj, norm_w.reshape(1, hd))


def _merge_body(a_ref, b_ref, wa_ref, wb_ref, ga_ref, gb_ref, o_ref):
    ya = _dot(a_ref[...], wa_ref[...])
    yb = _dot(b_ref[...], wb_ref[...])
    o_ref[...] = (_sigmoid(ga_ref[...]) * ya + _sigmoid(gb_ref[...]) * yb).astype(o_ref.dtype)


def _merge(o_a, o_b, wa16, wb16, proj, ga_col0, gb_col0):
    t, ka = o_a.shape
    kb = o_b.shape[1]
    d = wa16.shape[1]
    tm = min(t, 512)
    tn = _tile(d, 512)
    ga0, gb0 = ga_col0 // tn, gb_col0 // tn
    return pl.pallas_call(
        _merge_body,
        grid=(t // tm, d // tn),
        in_specs=[pl.BlockSpec((tm, ka), lambda i, j: (i, 0)),
                  pl.BlockSpec((tm, kb), lambda i, j: (i, 0)),
                  pl.BlockSpec((ka, tn), lambda i, j: (0, j)),
                  pl.BlockSpec((kb, tn), lambda i, j: (0, j)),
                  pl.BlockSpec((tm, tn), lambda i, j: (i, ga0 + j)),
                  pl.BlockSpec((tm, tn), lambda i, j: (i, gb0 + j))],
        out_specs=pl.BlockSpec((tm, tn), lambda i, j: (i, j)),
        out_shape=jax.ShapeDtypeStruct((t, d), BF16),
        compiler_params=_cp("parallel", "parallel"),
        name="merge",
    )(o_a, o_b, wa16, wb16, proj, proj)


def _outproj_router_body(m_ref, x_ref, wo_ref, g1_ref, nw_ref, sc_ref, sh_ref, wr_ref, br_ref,
                         x1_ref, h2_ref, te_ref, tg_ref, *, topk):
    x1 = x_ref[...] + g1_ref[...] * _dot(m_ref[...], wo_ref[...])
    x1_ref[...] = x1
    h2 = _rms(x1, nw_ref[...]) * (1.0 + sc_ref[...]) + sh_ref[...]
    h2_ref[...] = h2
    logits = _dot(h2.astype(BF16), wr_ref[...]) + br_ref[...]
    lane = lax.broadcasted_iota(I32, logits.shape, 1)
    te = jnp.zeros(logits.shape, I32)
    tg = jnp.zeros(logits.shape, F32)
    denom = jnp.zeros((logits.shape[0], 1), F32)
    m0 = None
    for kk in range(topk):
        mx = jnp.max(logits, axis=-1, keepdims=True)
        idx = jnp.min(jnp.where(logits == mx, lane, LANES), axis=-1, keepdims=True)
        if kk == 0:
            m0 = mx
        e = jnp.exp(mx - m0)
        denom = denom + e
        te = jnp.where(lane == kk, idx, te)
        tg = jnp.where(lane == kk, e, tg)
        logits = jnp.where(lane == idx, -jnp.inf, logits)
    te_ref[...] = te
    tg_ref[...] = tg / denom


def _outproj_router(merged, x2, wo16, gate1, nw, scale, shift, wr16, br_p, seq):
    t, d = x2.shape
    tm = min(seq, 256)
    per_b = seq // tm
    vec = lambda: pl.BlockSpec((None, 1, d), lambda i: (i // per_b, 0, 0))
    return pl.pallas_call(
        functools.partial(_outproj_router_body, topk=TOP_K),
        grid=(t // tm,),
        in_specs=[pl.BlockSpec((tm, d), lambda i: (i, 0)),
                  pl.BlockSpec((tm, d), lambda i: (i, 0)),
                  pl.BlockSpec((d, d), lambda i: (0, 0)),
                  vec(),
                  pl.BlockSpec((1, d), lambda i: (0, 0)),
                  vec(), vec(),
                  pl.BlockSpec((d, LANES), lambda i: (0, 0)),
                  pl.BlockSpec((1, LANES), lambda i: (0, 0))],
        out_specs=[pl.BlockSpec((tm, d), lambda i: (i, 0)),
                   pl.BlockSpec((tm, d), lambda i: (i, 0)),
                   pl.BlockSpec((tm, LANES), lambda i: (i, 0)),
                   pl.BlockSpec((tm, LANES), lambda i: (i, 0))],
        out_shape=[jax.ShapeDtypeStruct((t, d), F32),
                   jax.ShapeDtypeStruct((t, d), F32),
                   jax.ShapeDtypeStruct((t, LANES), I32),
                   jax.ShapeDtypeStruct((t, LANES), F32)],
        compiler_params=_cp("parallel"),
        name="outproj_router",
    )(merged, x2, wo16, gate1[:, None, :], nw.reshape(1, d), scale[:, None, :], shift[:, None, :], wr16, br_p)


def _moe_rank_body(te_ref, pos_ref, cnt_ref, carry_ref, *, tt, topk):
    @pl.when(pl.program_id(0) == 0)
    def _():
        carry_ref[...] = jnp.zeros(carry_ref.shape, F32)

    te = te_ref[...]
    lane = lax.broadcasted_iota(I32, (tt, LANES), 1)
    onehot = jnp.zeros((tt, LANES), F32)
    for kk in range(topk):
        onehot = onehot + jnp.where(lane == te[:, kk:kk + 1], 1.0, 0.0)
    r = lax.broadcasted_iota(I32, (tt, tt), 0)
    c = lax.broadcasted_iota(I32, (tt, tt), 1)
    below = jnp.where(r > c, 1.0, 0.0).astype(BF16)
    rank = _dot(below, onehot.astype(BF16)) + carry_ref[0:1, :]
    pos = jnp.zeros((tt, LANES), I32)
    for kk in range(topk):
        p = jnp.sum(jnp.where(lane == te[:, kk:kk + 1], rank, 0.0), axis=-1, keepdims=True)
        pos = jnp.where(lane == kk, p.astype(I32), pos)
    pos_ref[...] = pos
    carry_ref[...] = carry_ref[...] + jnp.sum(onehot, axis=0, keepdims=True)
    cnt_ref[...] = carry_ref[...]


def _moe_rank(te):
    t = te.shape[0]
    tt = min(t, 512)
    return pl.pallas_call(
        functools.partial(_moe_rank_body, tt=tt, topk=TOP_K),
        grid=(t // tt,),
        in_specs=[pl.BlockSpec((tt, LANES), lambda i: (i, 0))],
        out_specs=[pl.BlockSpec((tt, LANES), lambda i: (i, 0)),
                   pl.BlockSpec((8, LANES), lambda i: (0, 0))],
        out_shape=[jax.ShapeDtypeStruct((t, LANES), I32),
                   jax.ShapeDtypeStruct((8, LANES), F32)],
        scratch_shapes=[pltpu.VMEM((8, LANES), F32)],
        compiler_params=_cp("arbitrary"),
        name="moe_rank",
    )(te)


def _moe_dispatch_body(dest_ref, fill_ref, h_ref, x_hbm, zero_ref, sem, fill_sem, *, tt, topk, bm, n_blocks):
    @pl.when(pl.program_id(0) == 0)
    def _():
        zero_ref[...] = jnp.zeros(zero_ref.shape, zero_ref.dtype)

        def fill_copy(blk):
            return pltpu.make_async_copy(zero_ref, x_hbm.at[pl.ds(pl.multiple_of(blk * bm, bm), bm)], fill_sem)

        def start(blk, _):
            @pl.when(fill_ref[blk] != 0)
            def _():
                fill_copy(blk).start()
            return 0

        def drain(blk, _):
            @pl.when(fill_ref[blk] != 0)
            def _():
                fill_copy(blk).wait()
            return 0

        lax.fori_loop(0, n_blocks, start, 0)
        lax.fori_loop(0, n_blocks, drain, 0)

    base = pl.program_id(0) * tt * topk

    def issue(t, _):
        for kk in range(topk):
            dst = x_hbm.at[pl.ds(dest_ref[base + t * topk + kk], 1)]
            pltpu.make_async_copy(h_ref.at[pl.ds(t, 1)], dst, sem).start()
        return 0

    lax.fori_loop(0, tt, issue, 0, unroll=2)
    for _ in range(topk):
        pltpu.make_async_copy(h_ref, x_hbm.at[pl.ds(0, tt)], sem).wait()


def _moe_dispatch(dest_flat, fill_flags, h2, n_rows):
    t, d = h2.shape
    tt = min(t, 128)
    bm = MOE_BM
    grid_spec = pltpu.PrefetchScalarGridSpec(
        num_scalar_prefetch=2,
        grid=(t // tt,),
        in_specs=[pl.BlockSpec((tt, d), lambda i, dest, fill: (i, 0))],
        out_specs=pl.BlockSpec(memory_space=pl.ANY),
        scratch_shapes=[pltpu.VMEM((bm, d), F32), pltpu.SemaphoreType.DMA(()), pltpu.SemaphoreType.DMA(())])
    return pl.pallas_call(
        functools.partial(_moe_dispatch_body, tt=tt, topk=TOP_K, bm=bm, n_blocks=n_rows // bm),
        grid_spec=grid_spec,
        out_shape=jax.ShapeDtypeStruct((n_rows, d), F32),
        compiler_params=_cp("arbitrary"),
        name="moe_dispatch",
    )(dest_flat, fill_flags, h2)


def _moe_gu_body(be_ref, na_ref, x_ref, wg_ref, wu_ref, bg_ref, bu_ref, act_ref, wg16_ref, wu16_ref):
    i = pl.program_id(1)
    prev = be_ref[jnp.maximum(i - 1, 0)]

    @pl.when((i == 0) | (be_ref[i] != prev))
    def _():
        wg16_ref[...] = wg_ref[...].astype(BF16)
        wu16_ref[...] = wu_ref[...].astype(BF16)

    @pl.when(i < na_ref[0])
    def _():
        x = x_ref[...].astype(BF16)
        g = jnp.minimum(_dot(x, wg16_ref[...]) + bg_ref[...], SWIGLU_LIMIT)
        u = jnp.clip(_dot(x, wu16_ref[...]) + bu_ref[...], -SWIGLU_LIMIT, SWIGLU_LIMIT)
        act_ref[...] = ((u + 1.0) * (g * _sigmoid(SWIGLU_ALPHA * g))).astype(act_ref.dtype)

    @pl.when(i >= na_ref[0])
    def _():
        act_ref[...] = jnp.zeros(act_ref.shape, act_ref.dtype)


def _moe_gu(block_e, n_act, x_rows, w_gu, b_gu):
    n_rows, d = x_rows.shape
    ne, _, ff2 = w_gu.shape
    ff = ff2 // 2
    bm = MOE_BM
    tn = _tile(ff, 1024)
    nt = ff // tn
    rowblk = lambda j, i, be, na: (jnp.minimum(i, na[0] - 1), 0)
    grid_spec = pltpu.PrefetchScalarGridSpec(
        num_scalar_prefetch=2,
        grid=(nt, n_rows // bm),
        in_specs=[pl.BlockSpec((bm, d), rowblk),
                  pl.BlockSpec((None, d, tn), lambda j, i, be, na: (be[i], 0, j)),
                  pl.BlockSpec((None, d, tn), lambda j, i, be, na: (be[i], 0, nt + j)),
                  pl.BlockSpec((None, 1, tn), lambda j, i, be, na: (be[i], 0, j)),
                  pl.BlockSpec((None, 1, tn), lambda j, i, be, na: (be[i], 0, nt + j))],
        out_specs=pl.BlockSpec((bm, tn), lambda j, i, be, na: (i, j)),
        scratch_shapes=[pltpu.VMEM((d, tn), BF16), pltpu.VMEM((d, tn), BF16)])
    return pl.pallas_call(
        _moe_gu_body,
        grid_spec=grid_spec,
        out_shape=jax.ShapeDtypeStruct((n_rows, ff), BF16),
        compiler_params=_cp("arbitrary", "arbitrary"),
        name="moe_gu",
    )(block_e, n_act, x_rows, w_gu, w_gu, b_gu.reshape(ne, 1, ff2), b_gu.reshape(ne, 1, ff2))


def _moe_down_body(be_ref, na_ref, a_ref, w_ref, b_ref, y_ref, w16_ref):
    i = pl.program_id(1)
    prev = be_ref[jnp.maximum(i - 1, 0)]

    @pl.when((i == 0) | (be_ref[i] != prev))
    def _():
        w16_ref[...] = w_ref[...].astype(BF16)

    @pl.when(i < na_ref[0])
    def _():
        y_ref[...] = _dot(a_ref[...], w16_ref[...]) + b_ref[...]

    @pl.when(i >= na_ref[0])
    def _():
        y_ref[...] = jnp.zeros(y_ref.shape, y_ref.dtype)


def _moe_down(block_e, n_act, act, w_down, b_down):
    n_rows, ff = act.shape
    ne, _, d = w_down.shape
    bm = MOE_BM
    tn = _tile(d, 2048)
    grid_spec = pltpu.PrefetchScalarGridSpec(
        num_scalar_prefetch=2,
        grid=(d // tn, n_rows // bm),
        in_specs=[pl.BlockSpec((bm, ff), lambda j, i, be, na: (jnp.minimum(i, na[0] - 1), 0)),
                  pl.BlockSpec((None, ff, tn), lambda j, i, be, na: (be[i], 0, j)),
                  pl.BlockSpec((None, 1, tn), lambda j, i, be, na: (be[i], 0, j))],
        out_specs=pl.BlockSpec((bm, tn), lambda j, i, be, na: (i, j)),
        scratch_shapes=[pltpu.VMEM((ff, tn), BF16)])
    return pl.pallas_call(
        _moe_down_body,
        grid_spec=grid_spec,
        out_shape=jax.ShapeDtypeStruct((n_rows, d), F32),
        compiler_params=_cp("arbitrary", "arbitrary"),
        name="moe_down",
    )(block_e, n_act, act, w_down, b_down.reshape(ne, 1, d))


def _moe_combine_body(dest_ref, x1_ref, tg_ref, g2_ref, fw_ref, y_hbm, o_ref, buf_ref, sem, *, tt, topk):
    base = pl.program_id(0) * tt * topk

    def issue(t, _):
        for kk in range(topk):
            src = y_hbm.at[pl.ds(dest_ref[base + t * topk + kk], 1)]
            pltpu.make_async_copy(src, buf_ref.at[kk, pl.ds(t, 1)], sem).start()
        return 0

    lax.fori_loop(0, tt, issue, 0, unroll=2)
    for kk in range(topk):
        pltpu.make_async_copy(y_hbm.at[pl.ds(0, tt)], buf_ref.at[kk], sem).wait()
    tg = tg_ref[...]
    y = tg[:, 0:1] * buf_ref[0]
    for kk in range(1, topk):
        y = y + tg[:, kk:kk + 1] * buf_ref[kk]
    o_ref[...] = _rms(x1_ref[...] + g2_ref[...] * y, fw_ref[...])


def _moe_combine(dest_flat, x1, tg, gate2, final_w, y_rows, seq):
    t, d = x1.shape
    tt = min(seq, 128)
    per_b = seq // tt
    grid_spec = pltpu.PrefetchScalarGridSpec(
        num_scalar_prefetch=1,
        grid=(t // tt,),
        in_specs=[pl.BlockSpec((tt, d), lambda i, dest: (i, 0)),
                  pl.BlockSpec((tt, LANES), lambda i, dest: (i, 0)),
                  pl.BlockSpec((None, 1, d), lambda i, dest: (i // per_b, 0, 0)),
                  pl.BlockSpec((1, d), lambda i, dest: (0, 0)),
                  pl.BlockSpec(memory_space=pl.ANY)],
        out_specs=pl.BlockSpec((tt, d), lambda i, dest: (i, 0)),
        scratch_shapes=[pltpu.VMEM((TOP_K, tt, d), F32), pltpu.SemaphoreType.DMA(())])
    return pl.pallas_call(
        functools.partial(_moe_combine_body, tt=tt, topk=TOP_K),
        grid_spec=grid_spec,
        out_shape=jax.ShapeDtypeStruct((t, d), F32),
        compiler_params=_cp("arbitrary"),
        name="moe_combine",
    )(dest_flat, x1, tg, gate2[:, None, :], final_w.reshape(1, d), y_rows)


def _pad_lanes(v, off=0, fill=0.0):
    out = jnp.full((1, LANES), fill, F32)
    return out.at[0, off:off + v.shape[0]].set(v.astype(F32))


def _layer(x2, c, bsz, seq, p, final_w):
    d = x2.shape[1]
    qr, nh, dh = p["w_uq"].shape
    kvr = p["w_uk"].shape[0]
    nhi, di = p["w_uqi"].shape[1:]
    hd = p["gdn_norm_w"].shape[0]
    nhv = p["a_log"].shape[0]
    v_w = nhv * hd
    qk_w = (p["conv_w"].shape[1] - v_w) // 2
    nqk = qk_w // hd
    ne = p["w_router"].shape[1]

    mod = _ada(c, p["ada_w"], p["ada_b"])
    shift1, scale1, gate1, shift2, scale2, gate2 = jnp.split(mod, 6, axis=-1)

    widths = (qr, kvr, di, nhi, qk_w, qk_w, v_w, nhv, nhv, v_w, d, d)
    offs = [0]
    for wd in widths:
        offs.append(offs[-1] + wd)
    col = lambda k: p["w_in"][:, offs[k]:offs[k + 1]]
    pad = jnp.zeros((d, LANES - 2 * nhv - nhi), F32)
    w_in = jnp.concatenate([col(4), col(5), col(6), col(9), col(10), col(11),
                            col(0), col(1), col(2), col(7), col(8), col(3), pad], axis=1).astype(BF16)
    lat_w = qr + kvr + di + LANES
    gq_col = 0
    z_col = gq_col + 2 * qk_w + v_w
    ga_col = z_col + v_w
    gb_col = ga_col + d
    lat_col = gb_col + d
    assert lat_col % lat_w == 0

    proj = _inproj(x2, p["norm1_w"], scale1, shift1, w_in, p["conv_w"], seq, gq_col, qk_w, hd)

    wuq_t = p["w_uq"].reshape(qr, nh * dh).T.astype(BF16)
    wuqi_t = p["w_uqi"].reshape(qr, nhi * di).T.astype(BF16)
    wuk_r = jnp.transpose(p["w_uk"], (1, 0, 2)).astype(BF16)
    wuv_t = jnp.transpose(p["w_uv"], (1, 2, 0)).astype(BF16)
    qabs_t, qi_t, ckv, ckvt, ki, wi_t = _dsa_prep(
        proj, bsz, seq, p["q_lat_norm_w"], p["kv_lat_norm_w"], p["idx_k_norm_w"], wuq_t, wuqi_t, wuk_r,
        (qr, kvr, di, nh, dh, nhi, 2 * nhv, lat_col // lat_w))
    o_a = _dsa_attn(ki, ckv, ckvt, qi_t, qabs_t, wi_t, wuv_t).reshape(bsz * seq, nh * dh)

    gc, beta, gct = _gdn_gates(proj, _pad_lanes(p["a_log"]), _pad_lanes(p["dt_bias"]), bsz, seq,
                               (lat_col + lat_w - LANES) // LANES, nhv)
    gct4 = gct.reshape(bsz, LANES, seq // GDN_CHUNK, GDN_CHUNK)
    o_b = _gdn_core(proj, gc, beta, gct4, p["gdn_norm_w"], bsz, seq, nqk, nhv, hd, gq_col, z_col)

    merged = _merge(o_a, o_b, p["w_branch_a"].astype(BF16), p["w_branch_b"].astype(BF16), proj, ga_col, gb_col)

    wr16 = jnp.zeros((d, LANES), F32).at[:, :ne].set(p["w_router"]).astype(BF16)
    br_p = _pad_lanes(p["b_router"], fill=-1e30)
    x1, h2, te, tg = _outproj_router(merged, x2, p["w_out"].astype(BF16), gate1, p["norm2_w"], scale2, shift2,
                                     wr16, br_p, seq)

    t = bsz * seq
    pos, cnt = _moe_rank(te)
    counts = cnt[0, :ne].astype(I32)
    padded = (counts + MOE_BM - 1) // MOE_BM * MOE_BM
    pad_end = jnp.cumsum(padded)
    pad_start = pad_end - padded
    dest = (pad_start[te[:, :TOP_K]] + pos[:, :TOP_K]).reshape(-1).astype(I32)
    n_blocks = -(-(t * TOP_K) // MOE_BM) + ne
    n_act = (pad_end[-1] // MOE_BM).astype(I32).reshape(1)
    blk = jnp.minimum(jnp.arange(n_blocks, dtype=I32), n_act[0] - 1) * MOE_BM
    block_e = jnp.minimum(jnp.sum(pad_end[None, :] <= blk[:, None], axis=1), ne - 1).astype(I32)
    blk_end = (jnp.arange(n_blocks, dtype=I32) + 1) * MOE_BM
    has_pad = jnp.any((blk_end[:, None] == pad_end[None, :]) & (padded[None, :] > 0), axis=1)
    fill_flags = (has_pad | (blk_end > pad_end[-1])).astype(I32)
    x_rows = _moe_dispatch(dest, fill_flags, h2, n_blocks * MOE_BM)
    act = _moe_gu(block_e, n_act, x_rows, p["w_gu"], p["b_gu"])
    y_rows = _moe_down(block_e, n_act, act, p["w_down"], p["b_down"])
    return _moe_combine(dest, x1, tg, gate2, final_w, y_rows, seq)


@jax.jit
def kernel(x, c, ada_w, ada_b, norm1_w, w_in, q_lat_norm_w, kv_lat_norm_w, idx_k_norm_w, w_uq, w_uqi, w_uk, w_uv, conv_w, a_log, dt_bias, gdn_norm_w, w_branch_a, w_branch_b, w_out, norm2_w, w_router, b_router, w_gu, b_gu, w_down, b_down, final_norm_w):
    bsz, seq, d = x.shape
    stacked = dict(ada_w=ada_w, ada_b=ada_b, norm1_w=norm1_w, w_in=w_in, q_lat_norm_w=q_lat_norm_w,
                   kv_lat_norm_w=kv_lat_norm_w, idx_k_norm_w=idx_k_norm_w, w_uq=w_uq, w_uqi=w_uqi, w_uk=w_uk,
                   w_uv=w_uv, conv_w=conv_w, a_log=a_log, dt_bias=dt_bias, gdn_norm_w=gdn_norm_w,
                   w_branch_a=w_branch_a, w_branch_b=w_branch_b, w_out=w_out, norm2_w=norm2_w,
                   w_router=w_router, b_router=b_router, w_gu=w_gu, b_gu=b_gu, w_down=w_down, b_down=b_down)
    depth = ada_w.shape[0]
    assert depth == 1, "the final norm is fused into the last layer's combine kernel"
    x2 = x.reshape(bsz * seq, d)
    p = {k: v[0] for k, v in stacked.items()}
    out = _layer(x2, c, bsz, seq, p, final_norm_w)
    return out.reshape(bsz, seq, d)
```

```python
import functools

import jax
import jax.numpy as jnp
from jax import lax
from jax.experimental import pallas as pl
from jax.experimental.pallas import tpu as pltpu

F32 = jnp.float32
BF16 = jnp.bfloat16
I32 = jnp.int32

EPS = 1e-6
LANES = 128
VMEM_LIMIT = 56 * 1024 * 1024

IDX_TOPK = 256
GDN_CHUNK = 64
TOP_K = 4
SWIGLU_LIMIT = 7.0
SWIGLU_ALPHA = 1.702
MOE_BM = 512
ONES_ROWS = 16
LOG2E = 1.4426950408889634
INT_MIN = -(2 ** 31)
INT_MAX = 2 ** 31 - 1


def _cp(*sem):
    return pltpu.CompilerParams(dimension_semantics=sem, vmem_limit_bytes=VMEM_LIMIT)


def _tile(n, pref):
    if n <= pref:
        return n
    t = pref - pref % LANES
    while n % t:
        t -= LANES
    return t


def _rms(x, w):
    return x * lax.rsqrt(jnp.mean(x * x, axis=-1, keepdims=True) + EPS) * w


def _sigmoid(x):
    return 0.5 * jnp.tanh(0.5 * x) + 0.5


def _dot(a, b):
    return jnp.dot(a, b, preferred_element_type=F32)


def _dot_nt(a, b):
    return lax.dot_general(a, b, (((1,), (1,)), ((), ())), preferred_element_type=F32)


def _dot_tn(a, b):
    return lax.dot_general(a, b, (((0,), (0,)), ((), ())), preferred_element_type=F32)


def _ada_body(ct_ref, w_ref, b_ref, o_ref, *, nb):
    ct = ct_ref[...]
    s = ct * jax.nn.sigmoid(ct)
    w = w_ref[...]
    for b in range(nb):
        o_ref[b:b + 1, :] = jnp.sum(w * s[:, b:b + 1], axis=0, keepdims=True) + b_ref[...]


def _ada(c, w, b):
    nb, d = c.shape
    n = w.shape[1]
    tn = _tile(n, 1024)
    return pl.pallas_call(
        functools.partial(_ada_body, nb=nb),
        grid=(n // tn,),
        in_specs=[pl.BlockSpec((d, nb), lambda j: (0, 0)),
                  pl.BlockSpec((d, tn), lambda j: (0, j)),
                  pl.BlockSpec((1, tn), lambda j: (0, j))],
        out_specs=pl.BlockSpec((nb, tn), lambda j: (0, j)),
        out_shape=jax.ShapeDtypeStruct((nb, n), F32),
        compiler_params=_cp("parallel"),
        name="ada",
    )(c.T, w, b.reshape(1, n))


def _inproj_body(x_ref, nw_ref, sc_ref, sh_ref, w_ref, cw_ref, o_ref, h_ref, carry_ref, *ubuf_refs,
                 tm, tn, sw, rc, hd, per_b, c0, n_qk, n_conv, q_scale):
    i = pl.program_id(0)
    j = pl.program_id(1)

    @pl.when(j == 0)
    def _():
        h = _rms(x_ref[...], nw_ref[...]) * (1.0 + sc_ref[...]) + sh_ref[...]
        h_ref[...] = h.astype(BF16)

    cj = j - c0
    in_conv = (cj >= 0) & (cj < n_conv)

    @pl.when(jnp.logical_not(in_conv))
    def _():
        o_ref[...] = _dot(h_ref[...], w_ref[...])

    @pl.when(in_conv)
    def _():
        taps = cw_ref.shape[0]

        @pl.when(i % per_b == 0)
        def _():
            carry_ref[cj] = jnp.zeros((8, tn), F32)

        prev = carry_ref[cj]
        scale = jnp.where(cj < n_qk, q_scale, 1.0)

        def conv_tile(normalise):
            tails = []
            for s, ubuf_ref in enumerate(ubuf_refs):
                cols = slice(s * sw, (s + 1) * sw)
                w = cw_ref[:, cols]
                u = _dot(h_ref[...], w_ref[:, cols])
                ubuf_ref[0:8, :] = prev[:, cols]
                ubuf_ref[8:, :] = u
                tails.append(u[tm - 8:, :])
                for r in range(tm // rc):
                    ext = ubuf_ref[r * rc:(r + 1) * rc + 8, :]
                    y = ext[8:] * w[taps - 1:taps, :]
                    for k in range(1, taps):
                        y = y + pltpu.roll(ext, k, axis=0)[8:] * w[taps - 1 - k:taps - k, :]
                    y = y * _sigmoid(y)
                    if not normalise:
                        o_ref[r * rc:(r + 1) * rc, cols] = y
                        continue
                    for hh in range(sw // hd):
                        x = y[:, hh * hd:(hh + 1) * hd]
                        inv = lax.rsqrt(jnp.sum(x * x, axis=-1, keepdims=True) + EPS) * scale
                        c_lo = s * sw + hh * hd
                        o_ref[r * rc:(r + 1) * rc, c_lo:c_lo + hd] = x * inv
            carry_ref[cj] = jnp.concatenate(tails, axis=1)

        @pl.when(cj < 2 * n_qk)
        def _():
            conv_tile(True)

        @pl.when(cj >= 2 * n_qk)
        def _():
            conv_tile(False)


def _inproj(x2, nw, scale, shift, w16, conv_w, seq, conv_col0, qk_w, hd):
    t, d = x2.shape
    n = w16.shape[1]
    tm = min(seq, 1024)
    tn = _tile(n, 1024)
    per_b = seq // tm
    conv_ch = conv_w.shape[1]
    assert conv_col0 % tn == 0 and qk_w % tn == 0 and conv_ch % tn == 0 and tn % hd == 0
    c0, n_qk, n_conv = conv_col0 // tn, qk_w // tn, conv_ch // tn
    rc = min(tm, 64)
    sw = min(tn, 256)
    body = functools.partial(_inproj_body, tm=tm, tn=tn, sw=sw, rc=rc, hd=hd, per_b=per_b, c0=c0, n_qk=n_qk,
                             n_conv=n_conv, q_scale=hd ** -0.5)
    return pl.pallas_call(
        body,
        grid=(t // tm, n // tn),
        in_specs=[pl.BlockSpec((tm, d), lambda i, j: (i, 0)),
                  pl.BlockSpec((1, d), lambda i, j: (0, 0)),
                  pl.BlockSpec((None, 1, d), lambda i, j: (i // per_b, 0, 0)),
                  pl.BlockSpec((None, 1, d), lambda i, j: (i // per_b, 0, 0)),
                  pl.BlockSpec((d, tn), lambda i, j: (0, j)),
                  pl.BlockSpec((conv_w.shape[0], tn), lambda i, j: (0, jnp.clip(j - c0, 0, n_conv - 1)))],
        out_specs=pl.BlockSpec((tm, tn), lambda i, j: (i, j)),
        out_shape=jax.ShapeDtypeStruct((t, n), F32),
        scratch_shapes=[pltpu.VMEM((tm, d), BF16), pltpu.VMEM((n_conv, 8, tn), F32)]
        + [pltpu.VMEM((tm + 8, sw), F32) for _ in range(tn // sw)],
        compiler_params=_cp("arbitrary", "arbitrary"),
        name="inproj",
    )(x2, nw.reshape(1, d), scale[:, None, :], shift[:, None, :], w16, conv_w)


def _dsa_prep_body(lat_ref, qnw_ref, kvnw_ref, knw_ref, wuq_ref, wuqi_ref, wuk_ref,
                   qabs_ref, qi_ref, ckv_ref, ckvt_ref, ki_ref, wi_ref,
                   *, qr, kvr, di, nh, dh, nhi, q_scale, i_scale, w_scale, wi_off):
    lat = lat_ref[...]
    cq_t = _rms(lat[:, :qr], qnw_ref[...]).T.astype(BF16)
    q_t = _dot(wuq_ref[...], cq_t)
    for h in range(nh):
        q_h = q_t[h * dh:(h + 1) * dh].astype(BF16)
        qabs_ref[h] = (_dot(wuk_ref[h], q_h) * q_scale).astype(BF16)
    qi_t = _dot(wuqi_ref[...], cq_t) * i_scale
    for h in range(nhi):
        qi_ref[h] = qi_t[h * di:(h + 1) * di].astype(BF16)
    ckv = _rms(lat[:, qr:qr + kvr], kvnw_ref[...])
    ckv_ref[...] = ckv.astype(BF16)
    ckvt_ref[0:kvr, :] = ckv.T.astype(BF16)
    ckvt_ref[kvr:, :] = jnp.ones((ONES_ROWS, ckvt_ref.shape[1]), BF16)
    ki_ref[...] = _rms(lat[:, qr + kvr:qr + kvr + di], knw_ref[...]).astype(BF16)
    misc_t = lat[:, qr + kvr + di:].T
    wi_ref[...] = misc_t[wi_off:wi_off + nhi] * w_scale


def _dsa_prep(proj, bsz, seq, qnw, kvnw, knw, wuq_t, wuqi_t, wuk_r, dims):
    qr, kvr, di, nh, dh, nhi, wi_off, lat_blk = dims
    tm = min(seq, 512)
    ns = seq // tm
    lat_w = qr + kvr + di + LANES
    body = functools.partial(
        _dsa_prep_body, qr=qr, kvr=kvr, di=di, nh=nh, dh=dh, nhi=nhi,
        q_scale=dh ** -0.5 * LOG2E, i_scale=di ** -0.5, w_scale=nhi ** -0.5, wi_off=wi_off)
    full = lambda shape: pl.BlockSpec(shape, lambda b, s: (0,) * len(shape))
    return pl.pallas_call(
        body,
        grid=(bsz, ns),
        in_specs=[pl.BlockSpec((tm, lat_w), lambda b, s: (b * ns + s, lat_blk)),
                  full((1, qr)), full((1, kvr)), full((1, di)),
                  full((nh * dh, qr)), full((nhi * di, qr)), full((nh, kvr, dh))],
        out_specs=[pl.BlockSpec((None, nh, kvr, tm), lambda b, s: (b, 0, 0, s)),
                   pl.BlockSpec((None, nhi, di, tm), lambda b, s: (b, 0, 0, s)),
                   pl.BlockSpec((None, tm, kvr), lambda b, s: (b, s, 0)),
                   pl.BlockSpec((None, kvr + ONES_ROWS, tm), lambda b, s: (b, 0, s)),
                   pl.BlockSpec((None, tm, di), lambda b, s: (b, s, 0)),
                   pl.BlockSpec((None, nhi, tm), lambda b, s: (b, 0, s))],
        out_shape=[jax.ShapeDtypeStruct((bsz, nh, kvr, seq), BF16),
                   jax.ShapeDtypeStruct((bsz, nhi, di, seq), BF16),
                   jax.ShapeDtypeStruct((bsz, seq, kvr), BF16),
                   jax.ShapeDtypeStruct((bsz, kvr + ONES_ROWS, seq), BF16),
                   jax.ShapeDtypeStruct((bsz, seq, di), BF16),
                   jax.ShapeDtypeStruct((bsz, nhi, seq), F32)],
        compiler_params=_cp("parallel", "parallel"),
        name="dsa_prep",
    )(proj, qnw.reshape(1, qr), kvnw.reshape(1, kvr), knw.reshape(1, di), wuq_t, wuqi_t, wuk_r)


def _dsa_attn_body(ki_ref, ckv_ref, ckvt_ref, qi_ref, qabs_ref, wi_ref, wuv_ref, o_ref,
                   keys_ref, k16_ref, acc_ref, m_ref, ot_ref, thr_ref, jlim_ref,
                   *, tq, tk, nh, nhi, dh, kvr, topk, seq_bits):
    qb = pl.program_id(1)
    nk = ((qb + 1) * tq + tk - 1) // tk
    q_pos = qb * tq + lax.broadcasted_iota(I32, (tk, tq), 1)
    k_iota = lax.broadcasted_iota(I32, (tk, tq), 0)

    def score_tile(kt, _):
        ki_t = ki_ref[pl.ds(pl.multiple_of(kt * tk, tk), tk), :]
        acc = jnp.zeros((tk, tq), F32)
        for h in range(nhi):
            rel = _dot(ki_t, qi_ref[h])
            acc = acc + wi_ref[h:h + 1, :] * jnp.maximum(rel, 0.0)
        acc = jnp.where(acc == 0.0, 0.0, acc)
        bits = lax.bitcast_convert_type(acc, I32)
        causal = kt * tk + k_iota <= q_pos
        key = jnp.where(bits < 0, bits ^ INT_MAX, bits)
        keys_ref[pl.ds(pl.multiple_of(kt * tk, tk), tk), :] = jnp.where(causal, key, INT_MIN)
        top = lax.bitcast_convert_type(bits & -65536, F32)
        k16_ref[pl.ds(pl.multiple_of(kt * tk, tk), tk), :] = jnp.where(causal, top, jnp.nan).astype(BF16)
        return 0

    lax.fori_loop(0, nk, score_tile, 0)

    one16 = jnp.ones((tk, tq), BF16)
    zero16 = jnp.zeros((tk, tq), BF16)

    def count16(cand_bf):
        cand_b = jnp.broadcast_to(cand_bf, (tk, tq))

        def tile(kt, c):
            k_t = k16_ref[pl.ds(pl.multiple_of(kt * tk, tk), tk), :]
            hit = jnp.where(k_t >= cand_b, one16, zero16)
            part = hit[0:16]
            for r in range(1, tk // 16):
                part = part + hit[r * 16:(r + 1) * 16]
            return c + part.astype(F32)
        c16 = lax.fori_loop(0, nk, tile, jnp.zeros((16, tq), F32))
        return jnp.sum(c16, axis=0, keepdims=True)

    def hi_step(it, hi):
        cand = hi + lax.shift_left(jnp.int32(1), 15 - it)
        pattern = jnp.where(cand >= 0, cand, cand ^ 0x7FFF) & 0xFFFF
        cand_bf = lax.bitcast_convert_type(lax.shift_left(pattern, 16), F32).astype(BF16)
        return jnp.where(count16(cand_bf) >= topk, cand, hi)

    hi16 = lax.fori_loop(0, 16, hi_step, jnp.full((1, tq), -(2 ** 15), I32))

    def count(pred_fn):
        def tile(kt, c):
            k_t = keys_ref[pl.ds(pl.multiple_of(kt * tk, tk), tk), :]
            hit = jnp.where(pred_fn(k_t, kt), 1, 0).astype(I32)
            return c + jnp.sum(hit.reshape(tk // 8, 8, tq), axis=0)
        c8 = lax.fori_loop(0, nk, tile, jnp.zeros((8, tq), I32))
        return jnp.sum(c8, axis=0, keepdims=True)

    def bit_cond(carry):
        it, _, settled = carry
        return (it < 32) & (jnp.min(settled) == 0)

    def bit_step(carry):
        it, thr, settled = carry
        for _ in range(4):
            cand = thr ^ lax.shift_left(jnp.int32(1), 31 - it)
            cnt = count(lambda k_t, kt: k_t >= cand)
            take = (cnt >= topk) & (settled == 0)
            settled = jnp.where(take & (cnt == topk), 1, settled)
            thr = jnp.where(take, cand, thr)
            it = it + 1
        return it, thr, settled

    _, thr, _ = lax.while_loop(
        bit_cond, bit_step, (jnp.int32(16), lax.shift_left(hi16, 16), jnp.zeros((1, tq), I32)))
    thr = jnp.maximum(thr, INT_MIN + 1)
    cnt_gt = count(lambda k_t, kt: k_t > thr)
    cnt_ge = count(lambda k_t, kt: k_t >= thr)
    thr_ref[...] = thr
    jlim_ref[...] = jnp.full((1, tq), INT_MAX, I32)

    @pl.when(jnp.max(cnt_ge) > topk)
    def _():
        need = topk - cnt_gt

        def pos_step(it, p):
            cand = p + lax.shift_left(jnp.int32(1), seq_bits - 1 - it)
            cnt = count(lambda k_t, kt: (k_t == thr) & (kt * tk + k_iota < cand))
            return jnp.where(cnt < need, cand, p)

        jlim_ref[...] = lax.fori_loop(0, seq_bits, pos_step, jnp.zeros((1, tq), I32))

    m_ref[...] = jnp.full(m_ref.shape, -jnp.inf, F32)
    acc_ref[...] = jnp.zeros(acc_ref.shape, F32)

    @pl.when(nk % 2 == 1)
    def _():
        keys_ref[pl.ds(pl.multiple_of(nk * tk, tk), tk), :] = jnp.full((tk, tq), INT_MIN, I32)

    ta = 2 * tk
    a_iota = lax.broadcasted_iota(I32, (ta, tq), 0)

    def attn_tile(kt, _):
        off = pl.multiple_of(kt * ta, ta)
        k_t = keys_ref[pl.ds(off, ta), :]
        thr_b = thr_ref[...]
        mask = (k_t > thr_b) | ((k_t == thr_b) & (kt * ta + a_iota <= jlim_ref[...]))
        bias = jnp.where(mask, 0.0, -jnp.inf)
        ckv_t = ckv_ref[pl.ds(off, ta), :]
        ckvt_t = ckvt_ref[:, pl.ds(off, ta)]
        logits = _dot(ckv_t, qabs_ref[0])
        for h in range(nh):
            s = logits + bias
            if h + 1 < nh:
                logits = _dot(ckv_t, qabs_ref[h + 1])
            m_old = m_ref[h]
            m_new = jnp.maximum(m_old, jnp.max(s, axis=0, keepdims=True))
            m_use = jnp.where(m_new == -jnp.inf, 0.0, m_new)
            p = jnp.exp2(s - m_use)
            alpha = jnp.exp2(m_old - m_use)
            acc_ref[h] = alpha * acc_ref[h] + _dot(ckvt_t, p.astype(BF16))
            m_ref[h] = m_new
        return 0

    lax.fori_loop(0, (nk + 1) // 2, attn_tile, 0)

    for h in range(nh):
        acc = acc_ref[h]
        o_lat = (acc[:kvr] / acc[kvr:kvr + 1]).astype(BF16)
        ot_ref[h * dh:(h + 1) * dh, :] = _dot(wuv_ref[h], o_lat)
    o_ref[...] = ot_ref[...].T.astype(BF16)


def _dsa_attn(ki, ckv, ckvt, qi_t, qabs_t, wi_t, wuv_t):
    bsz, seq, di = ki.shape
    kvr = ckv.shape[2]
    nh, dh = wuv_t.shape[0], wuv_t.shape[1]
    nhi = qi_t.shape[1]
    tq = min(seq // 2, 256)
    tk = tq
    assert seq % (2 * tk) == 0
    topk = min(IDX_TOPK, seq // 4)
    body = functools.partial(_dsa_attn_body, tq=tq, tk=tk, nh=nh, nhi=nhi, dh=dh, kvr=kvr, topk=topk,
                             seq_bits=max(1, (seq - 1).bit_length()))
    return pl.pallas_call(
        body,
        grid=(bsz, seq // tq),
        in_specs=[pl.BlockSpec((None, seq, di), lambda b, q: (b, 0, 0)),
                  pl.BlockSpec((None, seq, kvr), lambda b, q: (b, 0, 0)),
                  pl.BlockSpec((None, kvr + ONES_ROWS, seq), lambda b, q: (b, 0, 0)),
                  pl.BlockSpec((None, nhi, di, tq), lambda b, q: (b, 0, 0, q)),
                  pl.BlockSpec((None, nh, kvr, tq), lambda b, q: (b, 0, 0, q)),
                  pl.BlockSpec((None, nhi, tq), lambda b, q: (b, 0, q)),
                  pl.BlockSpec((nh, dh, kvr), lambda b, q: (0, 0, 0))],
        out_specs=pl.BlockSpec((None, tq, nh * dh), lambda b, q: (b, q, 0)),
        out_shape=jax.ShapeDtypeStruct((bsz, seq, nh * dh), BF16),
        scratch_shapes=[pltpu.VMEM((seq, tq), I32),
                        pltpu.VMEM((seq, tq), BF16),
                        pltpu.VMEM((nh, kvr + ONES_ROWS, tq), F32),
                        pltpu.VMEM((nh, 1, tq), F32),
                        pltpu.VMEM((nh * dh, tq), F32),
                        pltpu.VMEM((1, tq), I32),
                        pltpu.VMEM((1, tq), I32)],
        compiler_params=_cp("parallel", "parallel"),
        name="dsa_attn",
    )(ki, ckv, ckvt, qi_t, qabs_t, wi_t, wuv_t)


GDN_GROUP = 16
GDN_INV_BATCH = 16


def _gdn_gate_body(m_ref, alog_ref, dtb_ref, gc_ref, beta_ref, gct_ref, *, ts, chunk, nhv, ng):
    x = m_ref[...]
    z = x + dtb_ref[...]
    softplus = jnp.maximum(z, 0.0) + jnp.log(1.0 + jnp.exp(-jnp.abs(z)))
    g = -jnp.exp(alog_ref[...]) * softplus
    pos = lax.broadcasted_iota(I32, (ts, LANES), 0) % chunk
    d = 1
    while d < chunk:
        g = g + jnp.where(pos >= d, pltpu.roll(g, d, axis=0), 0.0)
        d *= 2
    gct_ref[...] = g.T
    beta = jax.nn.sigmoid(x)
    for j in range(nhv // ng):
        gc_ref[:, j * LANES:(j + 1) * LANES] = pltpu.roll(g, (LANES - j * ng) % LANES, axis=1)
        beta_ref[:, j * LANES:(j + 1) * LANES] = pltpu.roll(beta, (2 * LANES - nhv - j * ng) % LANES, axis=1)


def _gdn_gates(proj, alog_p, dtb_p, bsz, seq, misc_blk, nhv):
    ts = min(seq, 512)
    ns = seq // ts
    ng = GDN_GROUP
    gw = nhv // ng * LANES
    body = functools.partial(_gdn_gate_body, ts=ts, chunk=GDN_CHUNK, nhv=nhv, ng=ng)
    return pl.pallas_call(
        body,
        grid=(bsz, ns),
        in_specs=[pl.BlockSpec((ts, LANES), lambda b, s: (b * ns + s, misc_blk)),
                  pl.BlockSpec((1, LANES), lambda b, s: (0, 0)),
                  pl.BlockSpec((1, LANES), lambda b, s: (0, 0))],
        out_specs=[pl.BlockSpec((ts, gw), lambda b, s: (b * ns + s, 0)),
                   pl.BlockSpec((ts, gw), lambda b, s: (b * ns + s, 0)),
                   pl.BlockSpec((None, LANES, ts), lambda b, s: (b, 0, s))],
        out_shape=[jax.ShapeDtypeStruct((bsz * seq, gw), F32),
                   jax.ShapeDtypeStruct((bsz * seq, gw), F32),
                   jax.ShapeDtypeStruct((bsz, LANES, seq), F32)],
        compiler_params=_cp("parallel", "parallel"),
        name="gdn_gates",
    )(proj, alog_p, dtb_p)


def _gdn_core_body(q_ref, k_ref, v_ref, gc_ref, beta_ref, gr_ref, z_ref, nw_ref, o_ref, *scratch,
                   ts, chunk, hd, ng, rep, gr_steps):
    st_refs = scratch[:ng]
    gcb_ref, bb_ref, a_ref, u_ref, w_ref, qkm_ref, rhs_ref, qd_ref, ktt_ref, egl_ref = scratch[ng:]
    nc = ts // chunk
    gr_base = (pl.program_id(2) % gr_steps) * nc

    @pl.when(pl.program_id(2) == 0)
    def _():
        for st_ref in st_refs:
            st_ref[...] = jnp.zeros(st_ref.shape, F32)

    for g in range(ng):
        gcb_ref[g] = jnp.broadcast_to(gc_ref[:, g:g + 1], (ts, LANES))
        bb_ref[g] = jnp.broadcast_to(beta_ref[:, g:g + 1], (ts, LANES))

    ri = lax.broadcasted_iota(I32, (chunk, chunk), 0)
    ci = lax.broadcasted_iota(I32, (chunk, chunk), 1)
    incl = ri >= ci
    strict = ri > ci
    eye = jnp.where(ri == ci, 1.0, 0.0).astype(F32)

    def prep(c, _):
        rows = pl.ds(pl.multiple_of(c * chunk, chunk), chunk)
        for qh in range(ng // rep):
            q = q_ref[rows, qh * hd:(qh + 1) * hd]
            k = k_ref[rows, qh * hd:(qh + 1) * hd]
            k16 = k.astype(BF16)
            kk = _dot_nt(k16, k16)
            qk = _dot_nt(q.astype(BF16), k16)
            for vh in range(rep):
                g = qh * rep + vh
                n = c * ng + g
                gcb = gcb_ref[g, rows, :]
                bb = bb_ref[g, rows, :]
                g_row = gr_ref[g, pl.ds(gr_base + c, 1), :]
                decay = jnp.where(incl, jnp.exp(gcb[:, :chunk] - g_row), 0.0)
                a_ref[n] = jnp.where(strict, kk * bb[:, :chunk] * decay, 0.0)
                qkm_ref[n] = (qk * decay).astype(BF16)
                eg = jnp.exp(gcb)
                rhs_ref[n, :, 0:hd] = (v_ref[rows, g * hd:(g + 1) * hd] * bb).astype(BF16)
                rhs_ref[n, :, hd:2 * hd] = (k * bb * eg).astype(BF16)
                qd_ref[n] = (q * eg).astype(BF16)
                g_last = gcb[chunk - 1:chunk, :]
                ktt_ref[n] = (k * jnp.exp(g_last - gcb)).T.astype(BF16)
                egl_ref[n] = jnp.broadcast_to(jnp.exp(g_last), (8, LANES))
        return 0

    lax.fori_loop(0, nc, prep, 0)

    n_dbl = max(1, (chunk - 1).bit_length()) - 1

    def invert(ib, _):
        ns_ = [ib * GDN_INV_BATCH + j for j in range(GDN_INV_BATCH)]
        pw = [a_ref[n] for n in ns_]
        tm = [eye - a for a in pw]
        pw = [_dot(x, x) for x in [a.astype(BF16) for a in pw]]
        for lvl in range(n_dbl):
            last = lvl == n_dbl - 1
            lhs = [(t if last else jnp.concatenate([t, x], axis=0)).astype(BF16) for t, x in zip(tm, pw)]
            prod = [_dot(y, x.astype(BF16)) for y, x in zip(lhs, pw)]
            tm = [t + y[:chunk] for t, y in zip(tm, prod)]
            pw = [y[chunk:] for y in prod]
        uw = [_dot(t.astype(BF16), rhs_ref[n]) for n, t in zip(ns_, tm)]
        for n, x in zip(ns_, uw):
            u_ref[n] = x[:, :hd]
            w_ref[n] = x[:, hd:].astype(BF16)
        return 0

    lax.fori_loop(0, nc * ng // GDN_INV_BATCH, invert, 0)

    nw = nw_ref[...]

    def step(c, _):
        rows = pl.ds(pl.multiple_of(c * chunk, chunk), chunk)
        heads = range(ng)
        ns_ = [c * ng + g for g in heads]
        state = [st_refs[g][...] for g in heads]
        st16 = [x.astype(BF16) for x in state]
        ws = [_dot(w_ref[n], x) for n, x in zip(ns_, st16)]
        qs = [_dot(qd_ref[n], x) for n, x in zip(ns_, st16)]
        vn16 = [(u_ref[n] - x).astype(BF16) for n, x in zip(ns_, ws)]
        o = [y + _dot(qkm_ref[n], x) for n, x, y in zip(ns_, vn16, qs)]
        ds = [_dot(ktt_ref[n], x) for n, x in zip(ns_, vn16)]
        for g in heads:
            st_refs[g][...] = state[g] * egl_ref[ns_[g]][0:1, :] + ds[g]
            zz = z_ref[rows, g * hd:(g + 1) * hd]
            o_ref[rows, g * hd:(g + 1) * hd] = (_rms(o[g], nw) * (zz * _sigmoid(zz))).astype(o_ref.dtype)
        return 0

    lax.fori_loop(0, nc, step, 0)


def _gdn_core(proj, gc, beta, gct4, norm_w, bsz, seq, nqk, nhv, hd, q_col, z_col):
    assert hd == LANES
    chunk = GDN_CHUNK
    ng = GDN_GROUP
    rep = nhv // nqk
    ts = min(seq, 256)
    ns = seq // ts
    n = (ts // chunk) * ng
    assert ng % rep == 0 and nhv % ng == 0 and n % GDN_INV_BATCH == 0
    qw, vw = (ng // rep) * hd, ng * hd
    k_col, v_col = q_col + nqk * hd, q_col + 2 * nqk * hd
    assert q_col % qw == 0 and k_col % qw == 0 and v_col % vw == 0 and z_col % vw == 0
    gr_steps = max(1, 8 * chunk // ts)
    assert ns % gr_steps == 0
    body = functools.partial(_gdn_core_body, ts=ts, chunk=chunk, hd=hd, ng=ng, rep=rep, gr_steps=gr_steps)
    row = lambda b, h, s: b * ns + s
    return pl.pallas_call(
        body,
        grid=(bsz, nhv // ng, ns),
        in_specs=[pl.BlockSpec((ts, qw), lambda b, h, s: (row(b, h, s), q_col // qw + h)),
                  pl.BlockSpec((ts, qw), lambda b, h, s: (row(b, h, s), k_col // qw + h)),
                  pl.BlockSpec((ts, vw), lambda b, h, s: (row(b, h, s), v_col // vw + h)),
                  pl.BlockSpec((ts, LANES), lambda b, h, s: (row(b, h, s), h)),
                  pl.BlockSpec((ts, LANES), lambda b, h, s: (row(b, h, s), h)),
                  pl.BlockSpec((None, ng, gr_steps * ts // chunk, chunk), lambda b, h, s: (b, h, s // gr_steps, 0)),
                  pl.BlockSpec((ts, vw), lambda b, h, s: (row(b, h, s), z_col // vw + h)),
                  pl.BlockSpec((1, hd), lambda b, h, s: (0, 0))],
        out_specs=pl.BlockSpec((ts, vw), lambda b, h, s: (row(b, h, s), h)),
        out_shape=jax.ShapeDtypeStruct((bsz * seq, nhv * hd), BF16),
        scratch_shapes=[pltpu.VMEM((hd, hd), F32) for _ in range(ng)] + [
                        pltpu.VMEM((ng, ts, LANES), F32),
                        pltpu.VMEM((ng, ts, LANES), F32),
                        pltpu.VMEM((n, chunk, chunk), F32),
                        pltpu.VMEM((n, chunk, hd), F32),
                        pltpu.VMEM((n, chunk, hd), BF16),
                        pltpu.VMEM((n, chunk, chunk), BF16),
                        pltpu.VMEM((n, chunk, 2 * hd), BF16),
                        pltpu.VMEM((n, chunk, hd), BF16),
                        pltpu.VMEM((n, hd, chunk), BF16),
                        pltpu.VMEM((n, 8, LANES), F32)],
        compiler_params=_cp("parallel", "parallel", "arbitrary"),
        name="gdn_core",
    )(proj, proj, proj, gc, beta, gct4, proj, norm_w.reshape(1, hd))


def _merge_body(a_ref, b_ref, wa_ref, wb_ref, ga_ref, gb_ref, o_ref):
    ya = _dot(a_ref[...], wa_ref[...])
    yb = _dot(b_ref[...], wb_ref[...])
    o_ref[...] = (_sigmoid(ga_ref[...]) * ya + _sigmoid(gb_ref[...]) * yb).astype(o_ref.dtype)


def _merge(o_a, o_b, wa16, wb16, proj, ga_col0, gb_col0):
    t, ka = o_a.shape
    kb = o_b.shape[1]
    d = wa16.shape[1]
    tm = min(t, 512)
    tn = _tile(d, 512)
    ga0, gb0 = ga_col0 // tn, gb_col0 // tn
    return pl.pallas_call(
        _merge_body,
        grid=(t // tm, d // tn),
        in_specs=[pl.BlockSpec((tm, ka), lambda i, j: (i, 0)),
                  pl.BlockSpec((tm, kb), lambda i, j: (i, 0)),
                  pl.BlockSpec((ka, tn), lambda i, j: (0, j)),
                  pl.BlockSpec((kb, tn), lambda i, j: (0, j)),
                  pl.BlockSpec((tm, tn), lambda i, j: (i, ga0 + j)),
                  pl.BlockSpec((tm, tn), lambda i, j: (i, gb0 + j))],
        out_specs=pl.BlockSpec((tm, tn), lambda i, j: (i, j)),
        out_shape=jax.ShapeDtypeStruct((t, d), BF16),
        compiler_params=_cp("parallel", "parallel"),
        name="merge",
    )(o_a, o_b, wa16, wb16, proj, proj)


def _outproj_router_body(m_ref, x_ref, wo_ref, g1_ref, nw_ref, sc_ref, sh_ref, wr_ref, br_ref,
                         x1_ref, h2_ref, te_ref, tg_ref, *, topk):
    x1 = x_ref[...] + g1_ref[...] * _dot(m_ref[...], wo_ref[...])
    x1_ref[...] = x1
    h2 = _rms(x1, nw_ref[...]) * (1.0 + sc_ref[...]) + sh_ref[...]
    h2_ref[...] = h2
    logits = _dot(h2.astype(BF16), wr_ref[...]) + br_ref[...]
    lane = lax.broadcasted_iota(I32, logits.shape, 1)
    te = jnp.zeros(logits.shape, I32)
    tg = jnp.zeros(logits.shape, F32)
    denom = jnp.zeros((logits.shape[0], 1), F32)
    m0 = None
    for kk in range(topk):
        mx = jnp.max(logits, axis=-1, keepdims=True)
        idx = jnp.min(jnp.where(logits == mx, lane, LANES), axis=-1, keepdims=True)
        if kk == 0:
            m0 = mx
        e = jnp.exp(mx - m0)
        denom = denom + e
        te = jnp.where(lane == kk, idx, te)
        tg = jnp.where(lane == kk, e, tg)
        logits = jnp.where(lane == idx, -jnp.inf, logits)
    te_ref[...] = te
    tg_ref[...] = tg / denom


def _outproj_router(merged, x2, wo16, gate1, nw, scale, shift, wr16, br_p, seq):
    t, d = x2.shape
    tm = min(seq, 256)
    per_b = seq // tm
    vec = lambda: pl.BlockSpec((None, 1, d), lambda i: (i // per_b, 0, 0))
    return pl.pallas_call(
        functools.partial(_outproj_router_body, topk=TOP_K),
        grid=(t // tm,),
        in_specs=[pl.BlockSpec((tm, d), lambda i: (i, 0)),
                  pl.BlockSpec((tm, d), lambda i: (i, 0)),
                  pl.BlockSpec((d, d), lambda i: (0, 0)),
                  vec(),
                  pl.BlockSpec((1, d), lambda i: (0, 0)),
                  vec(), vec(),
                  pl.BlockSpec((d, LANES), lambda i: (0, 0)),
                  pl.BlockSpec((1, LANES), lambda i: (0, 0))],
        out_specs=[pl.BlockSpec((tm, d), lambda i: (i, 0)),
                   pl.BlockSpec((tm, d), lambda i: (i, 0)),
                   pl.BlockSpec((tm, LANES), lambda i: (i, 0)),
                   pl.BlockSpec((tm, LANES), lambda i: (i, 0))],
        out_shape=[jax.ShapeDtypeStruct((t, d), F32),
                   jax.ShapeDtypeStruct((t, d), F32),
                   jax.ShapeDtypeStruct((t, LANES), I32),
                   jax.ShapeDtypeStruct((t, LANES), F32)],
        compiler_params=_cp("parallel"),
        name="outproj_router",
    )(merged, x2, wo16, gate1[:, None, :], nw.reshape(1, d), scale[:, None, :], shift[:, None, :], wr16, br_p)


def _moe_rank_body(te_ref, pos_ref, cnt_ref, carry_ref, *, tt, topk):
    @pl.when(pl.program_id(0) == 0)
    def _():
        carry_ref[...] = jnp.zeros(carry_ref.shape, F32)

    te = te_ref[...]
    lane = lax.broadcasted_iota(I32, (tt, LANES), 1)
    onehot = jnp.zeros((tt, LANES), F32)
    for kk in range(topk):
        onehot = onehot + jnp.where(lane == te[:, kk:kk + 1], 1.0, 0.0)
    r = lax.broadcasted_iota(I32, (tt, tt), 0)
    c = lax.broadcasted_iota(I32, (tt, tt), 1)
    below = jnp.where(r > c, 1.0, 0.0).astype(BF16)
    rank = _dot(below, onehot.astype(BF16)) + carry_ref[0:1, :]
    pos = jnp.zeros((tt, LANES), I32)
    for kk in range(topk):
        p = jnp.sum(jnp.where(lane == te[:, kk:kk + 1], rank, 0.0), axis=-1, keepdims=True)
        pos = jnp.where(lane == kk, p.astype(I32), pos)
    pos_ref[...] = pos
    carry_ref[...] = carry_ref[...] + jnp.sum(onehot, axis=0, keepdims=True)
    cnt_ref[...] = carry_ref[...]


def _moe_rank(te):
    t = te.shape[0]
    tt = min(t, 512)
    return pl.pallas_call(
        functools.partial(_moe_rank_body, tt=tt, topk=TOP_K),
        grid=(t // tt,),
        in_specs=[pl.BlockSpec((tt, LANES), lambda i: (i, 0))],
        out_specs=[pl.BlockSpec((tt, LANES), lambda i: (i, 0)),
                   pl.BlockSpec((8, LANES), lambda i: (0, 0))],
        out_shape=[jax.ShapeDtypeStruct((t, LANES), I32),
                   jax.ShapeDtypeStruct((8, LANES), F32)],
        scratch_shapes=[pltpu.VMEM((8, LANES), F32)],
        compiler_params=_cp("arbitrary"),
        name="moe_rank",
    )(te)


def _moe_dispatch_body(dest_ref, fill_ref, h_ref, x_hbm, zero_ref, sem, fill_sem, *, tt, topk, bm, n_blocks):
    @pl.when(pl.program_id(0) == 0)
    def _():
        zero_ref[...] = jnp.zeros(zero_ref.shape, zero_ref.dtype)

        def fill_copy(blk):
            return pltpu.make_async_copy(zero_ref, x_hbm.at[pl.ds(pl.multiple_of(blk * bm, bm), bm)], fill_sem)

        def start(blk, _):
            @pl.when(fill_ref[blk] != 0)
            def _():
                fill_copy(blk).start()
            return 0

        def drain(blk, _):
            @pl.when(fill_ref[blk] != 0)
            def _():
                fill_copy(blk).wait()
            return 0

        lax.fori_loop(0, n_blocks, start, 0)
        lax.fori_loop(0, n_blocks, drain, 0)

    base = pl.program_id(0) * tt * topk

    def issue(t, _):
        for kk in range(topk):
            dst = x_hbm.at[pl.ds(dest_ref[base + t * topk + kk], 1)]
            pltpu.make_async_copy(h_ref.at[pl.ds(t, 1)], dst, sem).start()
        return 0

    lax.fori_loop(0, tt, issue, 0, unroll=2)
    for _ in range(topk):
        pltpu.make_async_copy(h_ref, x_hbm.at[pl.ds(0, tt)], sem).wait()


def _moe_dispatch(dest_flat, fill_flags, h2, n_rows):
    t, d = h2.shape
    tt = min(t, 128)
    bm = MOE_BM
    grid_spec = pltpu.PrefetchScalarGridSpec(
        num_scalar_prefetch=2,
        grid=(t // tt,),
        in_specs=[pl.BlockSpec((tt, d), lambda i, dest, fill: (i, 0))],
        out_specs=pl.BlockSpec(memory_space=pl.ANY),
        scratch_shapes=[pltpu.VMEM((bm, d), F32), pltpu.SemaphoreType.DMA(()), pltpu.SemaphoreType.DMA(())])
    return pl.pallas_call(
        functools.partial(_moe_dispatch_body, tt=tt, topk=TOP_K, bm=bm, n_blocks=n_rows // bm),
        grid_spec=grid_spec,
        out_shape=jax.ShapeDtypeStruct((n_rows, d), F32),
        compiler_params=_cp("arbitrary"),
        name="moe_dispatch",
    )(dest_flat, fill_flags, h2)


def _moe_gu_body(be_ref, na_ref, x_ref, wg_ref, wu_ref, bg_ref, bu_ref, act_ref, wg16_ref, wu16_ref):
    i = pl.program_id(1)
    prev = be_ref[jnp.maximum(i - 1, 0)]

    @pl.when((i == 0) | (be_ref[i] != prev))
    def _():
        wg16_ref[...] = wg_ref[...].astype(BF16)
        wu16_ref[...] = wu_ref[...].astype(BF16)

    @pl.when(i < na_ref[0])
    def _():
        x = x_ref[...].astype(BF16)
        g = jnp.minimum(_dot(x, wg16_ref[...]) + bg_ref[...], SWIGLU_LIMIT)
        u = jnp.clip(_dot(x, wu16_ref[...]) + bu_ref[...], -SWIGLU_LIMIT, SWIGLU_LIMIT)
        act_ref[...] = ((u + 1.0) * (g * _sigmoid(SWIGLU_ALPHA * g))).astype(act_ref.dtype)

    @pl.when(i >= na_ref[0])
    def _():
        act_ref[...] = jnp.zeros(act_ref.shape, act_ref.dtype)


def _moe_gu(block_e, n_act, x_rows, w_gu, b_gu):
    n_rows, d = x_rows.shape
    ne, _, ff2 = w_gu.shape
    ff = ff2 // 2
    bm = MOE_BM
    tn = _tile(ff, 1024)
    nt = ff // tn
    rowblk = lambda j, i, be, na: (jnp.minimum(i, na[0] - 1), 0)
    grid_spec = pltpu.PrefetchScalarGridSpec(
        num_scalar_prefetch=2,
        grid=(nt, n_rows // bm),
        in_specs=[pl.BlockSpec((bm, d), rowblk),
                  pl.BlockSpec((None, d, tn), lambda j, i, be, na: (be[i], 0, j)),
                  pl.BlockSpec((None, d, tn), lambda j, i, be, na: (be[i], 0, nt + j)),
                  pl.BlockSpec((None, 1, tn), lambda j, i, be, na: (be[i], 0, j)),
                  pl.BlockSpec((None, 1, tn), lambda j, i, be, na: (be[i], 0, nt + j))],
        out_specs=pl.BlockSpec((bm, tn), lambda j, i, be, na: (i, j)),
        scratch_shapes=[pltpu.VMEM((d, tn), BF16), pltpu.VMEM((d, tn), BF16)])
    return pl.pallas_call(
        _moe_gu_body,
        grid_spec=grid_spec,
        out_shape=jax.ShapeDtypeStruct((n_rows, ff), BF16),
        compiler_params=_cp("arbitrary", "arbitrary"),
        name="moe_gu",
    )(block_e, n_act, x_rows, w_gu, w_gu, b_gu.reshape(ne, 1, ff2), b_gu.reshape(ne, 1, ff2))


def _moe_down_body(be_ref, na_ref, a_ref, w_ref, b_ref, y_ref, w16_ref):
    i = pl.program_id(1)
    prev = be_ref[jnp.maximum(i - 1, 0)]

    @pl.when((i == 0) | (be_ref[i] != prev))
    def _():
        w16_ref[...] = w_ref[...].astype(BF16)

    @pl.when(i < na_ref[0])
    def _():
        y_ref[...] = _dot(a_ref[...], w16_ref[...]) + b_ref[...]

    @pl.when(i >= na_ref[0])
    def _():
        y_ref[...] = jnp.zeros(y_ref.shape, y_ref.dtype)


def _moe_down(block_e, n_act, act, w_down, b_down):
    n_rows, ff = act.shape
    ne, _, d = w_down.shape
    bm = MOE_BM
    tn = _tile(d, 2048)
    grid_spec = pltpu.PrefetchScalarGridSpec(
        num_scalar_prefetch=2,
        grid=(d // tn, n_rows // bm),
        in_specs=[pl.BlockSpec((bm, ff), lambda j, i, be, na: (jnp.minimum(i, na[0] - 1), 0)),
                  pl.BlockSpec((None, ff, tn), lambda j, i, be, na: (be[i], 0, j)),
                  pl.BlockSpec((None, 1, tn), lambda j, i, be, na: (be[i], 0, j))],
        out_specs=pl.BlockSpec((bm, tn), lambda j, i, be, na: (i, j)),
        scratch_shapes=[pltpu.VMEM((ff, tn), BF16)])
    return pl.pallas_call(
        _moe_down_body,
        grid_spec=grid_spec,
        out_shape=jax.ShapeDtypeStruct((n_rows, d), F32),
        compiler_params=_cp("arbitrary", "arbitrary"),
        name="moe_down",
    )(block_e, n_act, act, w_down, b_down.reshape(ne, 1, d))


def _moe_combine_body(dest_ref, x1_ref, tg_ref, g2_ref, fw_ref, y_hbm, o_ref, buf_ref, sem, *, tt, topk):
    i = pl.program_id(0)
    slot = i % 2

    def gather(step, sl):
        base = step * tt * topk

        def issue(t, _):
            for kk in range(topk):
                src = y_hbm.at[pl.ds(dest_ref[base + t * topk + kk], 1)]
                pltpu.make_async_copy(src, buf_ref.at[sl, kk, pl.ds(t, 1)], sem.at[sl]).start()
            return 0

        lax.fori_loop(0, tt, issue, 0, unroll=2)

    @pl.when(i == 0)
    def _():
        gather(0, 0)

    @pl.when(i + 1 < pl.num_programs(0))
    def _():
        gather(i + 1, 1 - slot)

    for kk in range(topk):
        pltpu.make_async_copy(y_hbm.at[pl.ds(0, tt)], buf_ref.at[slot, kk], sem.at[slot]).wait()
    tg = tg_ref[...]
    y = tg[:, 0:1] * buf_ref[slot, 0]
    for kk in range(1, topk):
        y = y + tg[:, kk:kk + 1] * buf_ref[slot, kk]
    o_ref[...] = _rms(x1_ref[...] + g2_ref[...] * y, fw_ref[...])


def _moe_combine(dest_flat, x1, tg, gate2, final_w, y_rows, seq):
    t, d = x1.shape
    tt = min(seq, 128)
    per_b = seq // tt
    grid_spec = pltpu.PrefetchScalarGridSpec(
        num_scalar_prefetch=1,
        grid=(t // tt,),
        in_specs=[pl.BlockSpec((tt, d), lambda i, dest: (i, 0)),
                  pl.BlockSpec((tt, LANES), lambda i, dest: (i, 0)),
                  pl.BlockSpec((None, 1, d), lambda i, dest: (i // per_b, 0, 0)),
                  pl.BlockSpec((1, d), lambda i, dest: (0, 0)),
                  pl.BlockSpec(memory_space=pl.ANY)],
        out_specs=pl.BlockSpec((tt, d), lambda i, dest: (i, 0)),
        scratch_shapes=[pltpu.VMEM((2, TOP_K, tt, d), F32), pltpu.SemaphoreType.DMA((2,))])
    return pl.pallas_call(
        functools.partial(_moe_combine_body, tt=tt, topk=TOP_K),
        grid_spec=grid_spec,
        out_shape=jax.ShapeDtypeStruct((t, d), F32),
        compiler_params=_cp("arbitrary"),
        name="moe_combine",
    )(dest_flat, x1, tg, gate2[:, None, :], final_w.reshape(1, d), y_rows)


def _pad_lanes(v, off=0, fill=0.0):
    out = jnp.full((1, LANES), fill, F32)
    return out.at[0, off:off + v.shape[0]].set(v.astype(F32))


def _layer(x2, c, bsz, seq, p, final_w):
    d = x2.shape[1]
    qr, nh, dh = p["w_uq"].shape
    kvr = p["w_uk"].shape[0]
    nhi, di = p["w_uqi"].shape[1:]
    hd = p["gdn_norm_w"].shape[0]
    nhv = p["a_log"].shape[0]
    v_w = nhv * hd
    qk_w = (p["conv_w"].shape[1] - v_w) // 2
    nqk = qk_w // hd
    ne = p["w_router"].shape[1]

    mod = _ada(c, p["ada_w"], p["ada_b"])
    shift1, scale1, gate1, shift2, scale2, gate2 = jnp.split(mod, 6, axis=-1)

    widths = (qr, kvr, di, nhi, qk_w, qk_w, v_w, nhv, nhv, v_w, d, d)
    offs = [0]
    for wd in widths:
        offs.append(offs[-1] + wd)
    col = lambda k: p["w_in"][:, offs[k]:offs[k + 1]]
    pad = jnp.zeros((d, LANES - 2 * nhv - nhi), F32)
    w_in = jnp.concatenate([col(4), col(5), col(6), col(9), col(10), col(11),
                            col(0), col(1), col(2), col(7), col(8), col(3), pad], axis=1).astype(BF16)
    lat_w = qr + kvr + di + LANES
    gq_col = 0
    z_col = gq_col + 2 * qk_w + v_w
    ga_col = z_col + v_w
    gb_col = ga_col + d
    lat_col = gb_col + d
    assert lat_col % lat_w == 0

    proj = _inproj(x2, p["norm1_w"], scale1, shift1, w_in, p["conv_w"], seq, gq_col, qk_w, hd)

    wuq_t = p["w_uq"].reshape(qr, nh * dh).T.astype(BF16)
    wuqi_t = p["w_uqi"].reshape(qr, nhi * di).T.astype(BF16)
    wuk_r = jnp.transpose(p["w_uk"], (1, 0, 2)).astype(BF16)
    wuv_t = jnp.transpose(p["w_uv"], (1, 2, 0)).astype(BF16)
    qabs_t, qi_t, ckv, ckvt, ki, wi_t = _dsa_prep(
        proj, bsz, seq, p["q_lat_norm_w"], p["kv_lat_norm_w"], p["idx_k_norm_w"], wuq_t, wuqi_t, wuk_r,
        (qr, kvr, di, nh, dh, nhi, 2 * nhv, lat_col // lat_w))
    o_a = _dsa_attn(ki, ckv, ckvt, qi_t, qabs_t, wi_t, wuv_t).reshape(bsz * seq, nh * dh)

    gc, beta, gct = _gdn_gates(proj, _pad_lanes(p["a_log"]), _pad_lanes(p["dt_bias"]), bsz, seq,
                               (lat_col + lat_w - LANES) // LANES, nhv)
    gct4 = gct.reshape(bsz, LANES, seq // GDN_CHUNK, GDN_CHUNK)
    o_b = _gdn_core(proj, gc, beta, gct4, p["gdn_norm_w"], bsz, seq, nqk, nhv, hd, gq_col, z_col)

    merged = _merge(o_a, o_b, p["w_branch_a"].astype(BF16), p["w_branch_b"].astype(BF16), proj, ga_col, gb_col)

    wr16 = jnp.zeros((d, LANES), F32).at[:, :ne].set(p["w_router"]).astype(BF16)
    br_p = _pad_lanes(p["b_router"], fill=-1e30)
    x1, h2, te, tg = _outproj_router(merged, x2, p["w_out"].astype(BF16), gate1, p["norm2_w"], scale2, shift2,
                                     wr16, br_p, seq)

    t = bsz * seq
    pos, cnt = _moe_rank(te)
    counts = cnt[0, :ne].astype(I32)
    padded = (counts + MOE_BM - 1) // MOE_BM * MOE_BM
    pad_end = jnp.cumsum(padded)
    pad_start = pad_end - padded
    dest = (pad_start[te[:, :TOP_K]] + pos[:, :TOP_K]).reshape(-1).astype(I32)
    n_blocks = -(-(t * TOP_K) // MOE_BM) + ne
    n_act = (pad_end[-1] // MOE_BM).astype(I32).reshape(1)
    blk = jnp.minimum(jnp.arange(n_blocks, dtype=I32), n_act[0] - 1) * MOE_BM
    block_e = jnp.minimum(jnp.sum(pad_end[None, :] <= blk[:, None], axis=1), ne - 1).astype(I32)
    blk_end = (jnp.arange(n_blocks, dtype=I32) + 1) * MOE_BM
    has_pad = jnp.any((blk_end[:, None] == pad_end[None, :]) & (padded[None, :] > 0), axis=1)
    fill_flags = (has_pad | (blk_end > pad_end[-1])).astype(I32)
    x_rows = _moe_dispatch(dest, fill_flags, h2, n_blocks * MOE_BM)
    act = _moe_gu(block_e, n_act, x_rows, p["w_gu"], p["b_gu"])
    y_rows = _moe_down(block_e, n_act, act, p["w_down"], p["b_down"])
    return _moe_combine(dest, x1, tg, gate2, final_w, y_rows, seq)


@jax.jit
def kernel(x, c, ada_w, ada_b, norm1_w, w_in, q_lat_norm_w, kv_lat_norm_w, idx_k_norm_w, w_uq, w_uqi, w_uk, w_uv, conv_w, a_log, dt_bias, gdn_norm_w, w_branch_a, w_branch_b, w_out, norm2_w, w_router, b_router, w_gu, b_gu, w_down, b_down, final_norm_w):
    bsz, seq, d = x.shape
    stacked = dict(ada_w=ada_w, ada_b=ada_b, norm1_w=norm1_w, w_in=w_in, q_lat_norm_w=q_lat_norm_w,
                   kv_lat_norm_w=kv_lat_norm_w, idx_k_norm_w=idx_k_norm_w, w_uq=w_uq, w_uqi=w_uqi, w_uk=w_uk,
                   w_uv=w_uv, conv_w=conv_w, a_log=a_log, dt_bias=dt_bias, gdn_norm_w=gdn_norm_w,
                   w_branch_a=w_branch_a, w_branch_b=w_branch_b, w_out=w_out, norm2_w=norm2_w,
                   w_router=w_router, b_router=b_router, w_gu=w_gu, b_gu=b_gu, w_down=w_down, b_down=b_down)
    depth = ada_w.shape[0]
    assert depth == 1, "the final norm is fused into the last layer's combine kernel"
    x2 = x.reshape(bsz * seq, d)
    p = {k: v[0] for k, v in stacked.items()}
    out = _layer(x2, c, bsz, seq, p, final_norm_w)
    return out.reshape(bsz, seq, d)
```

```python
import functools

import jax
import jax.numpy as jnp
from jax import lax
from jax.experimental import pallas as pl
from jax.experimental.pallas import tpu as pltpu

F32 = jnp.float32
BF16 = jnp.bfloat16
I32 = jnp.int32

EPS = 1e-6
LANES = 128
VMEM_LIMIT = 56 * 1024 * 1024

IDX_TOPK = 256
GDN_CHUNK = 64
TOP_K = 4
SWIGLU_LIMIT = 7.0
SWIGLU_ALPHA = 1.702
MOE_BM = 512
ATTN_TILES = 2
ONES_ROWS = 16
LOG2E = 1.4426950408889634
INT_MIN = -(2 ** 31)
INT_MAX = 2 ** 31 - 1


def _cp(*sem):
    return pltpu.CompilerParams(dimension_semantics=sem, vmem_limit_bytes=VMEM_LIMIT)


def _tile(n, pref):
    if n <= pref:
        return n
    t = pref - pref % LANES
    while n % t:
        t -= LANES
    return t


def _rms(x, w):
    return x * lax.rsqrt(jnp.mean(x * x, axis=-1, keepdims=True) + EPS) * w


def _sigmoid(x):
    return 0.5 * jnp.tanh(0.5 * x) + 0.5


def _dot(a, b):
    return jnp.dot(a, b, preferred_element_type=F32)


def _dot_nt(a, b):
    return lax.dot_general(a, b, (((1,), (1,)), ((), ())), preferred_element_type=F32)


def _dot_tn(a, b):
    return lax.dot_general(a, b, (((0,), (0,)), ((), ())), preferred_element_type=F32)


def _ada_body(ct_ref, w_ref, b_ref, o_ref, *, nb):
    ct = ct_ref[...]
    s = ct * jax.nn.sigmoid(ct)
    w = w_ref[...]
    for b in range(nb):
        o_ref[b:b + 1, :] = jnp.sum(w * s[:, b:b + 1], axis=0, keepdims=True) + b_ref[...]


def _ada(c, w, b):
    nb, d = c.shape
    n = w.shape[1]
    tn = _tile(n, 1024)
    return pl.pallas_call(
        functools.partial(_ada_body, nb=nb),
        grid=(n // tn,),
        in_specs=[pl.BlockSpec((d, nb), lambda j: (0, 0)),
                  pl.BlockSpec((d, tn), lambda j: (0, j)),
                  pl.BlockSpec((1, tn), lambda j: (0, j))],
        out_specs=pl.BlockSpec((nb, tn), lambda j: (0, j)),
        out_shape=jax.ShapeDtypeStruct((nb, n), F32),
        compiler_params=_cp("parallel"),
        name="ada",
    )(c.T, w, b.reshape(1, n))


def _inproj_body(x_ref, nw_ref, sc_ref, sh_ref, w_ref, cw_ref, o_ref, h_ref, carry_ref, *ubuf_refs,
                 tm, tn, sw, rc, hd, per_b, c0, n_qk, n_conv, q_scale):
    i = pl.program_id(0)
    j = pl.program_id(1)

    @pl.when(j == 0)
    def _():
        h = _rms(x_ref[...], nw_ref[...]) * (1.0 + sc_ref[...]) + sh_ref[...]
        h_ref[...] = h.astype(BF16)

    cj = j - c0
    in_conv = (cj >= 0) & (cj < n_conv)

    @pl.when(jnp.logical_not(in_conv))
    def _():
        o_ref[...] = _dot(h_ref[...], w_ref[...])

    @pl.when(in_conv)
    def _():
        taps = cw_ref.shape[0]

        @pl.when(i % per_b == 0)
        def _():
            carry_ref[cj] = jnp.zeros((8, tn), F32)

        prev = carry_ref[cj]
        scale = jnp.where(cj < n_qk, q_scale, 1.0)

        def conv_tile(normalise):
            tails = []
            for s, ubuf_ref in enumerate(ubuf_refs):
                cols = slice(s * sw, (s + 1) * sw)
                w = cw_ref[:, cols]
                u = _dot(h_ref[...], w_ref[:, cols])
                ubuf_ref[0:8, :] = prev[:, cols]
                ubuf_ref[8:, :] = u
                tails.append(u[tm - 8:, :])
                for r in range(tm // rc):
                    ext = ubuf_ref[r * rc:(r + 1) * rc + 8, :]
                    y = ext[8:] * w[taps - 1:taps, :]
                    for k in range(1, taps):
                        y = y + pltpu.roll(ext, k, axis=0)[8:] * w[taps - 1 - k:taps - k, :]
                    y = y * _sigmoid(y)
                    if not normalise:
                        o_ref[r * rc:(r + 1) * rc, cols] = y
                        continue
                    for hh in range(sw // hd):
                        x = y[:, hh * hd:(hh + 1) * hd]
                        inv = lax.rsqrt(jnp.sum(x * x, axis=-1, keepdims=True) + EPS) * scale
                        c_lo = s * sw + hh * hd
                        o_ref[r * rc:(r + 1) * rc, c_lo:c_lo + hd] = x * inv
            carry_ref[cj] = jnp.concatenate(tails, axis=1)

        @pl.when(cj < 2 * n_qk)
        def _():
            conv_tile(True)

        @pl.when(cj >= 2 * n_qk)
        def _():
            conv_tile(False)


def _inproj(x2, nw, scale, shift, w16, conv_w, seq, conv_col0, qk_w, hd):
    t, d = x2.shape
    n = w16.shape[1]
    tm = min(seq, 1024)
    tn = _tile(n, 1024)
    per_b = seq // tm
    conv_ch = conv_w.shape[1]
    assert conv_col0 % tn == 0 and qk_w % tn == 0 and conv_ch % tn == 0 and tn % hd == 0
    c0, n_qk, n_conv = conv_col0 // tn, qk_w // tn, conv_ch // tn
    rc = min(tm, 64)
    sw = min(tn, 256)
    body = functools.partial(_inproj_body, tm=tm, tn=tn, sw=sw, rc=rc, hd=hd, per_b=per_b, c0=c0, n_qk=n_qk,
                             n_conv=n_conv, q_scale=hd ** -0.5)
    return pl.pallas_call(
        body,
        grid=(t // tm, n // tn),
        in_specs=[pl.BlockSpec((tm, d), lambda i, j: (i, 0)),
                  pl.BlockSpec((1, d), lambda i, j: (0, 0)),
                  pl.BlockSpec((None, 1, d), lambda i, j: (i // per_b, 0, 0)),
                  pl.BlockSpec((None, 1, d), lambda i, j: (i // per_b, 0, 0)),
                  pl.BlockSpec((d, tn), lambda i, j: (0, j)),
                  pl.BlockSpec((conv_w.shape[0], tn), lambda i, j: (0, jnp.clip(j - c0, 0, n_conv - 1)))],
        out_specs=pl.BlockSpec((tm, tn), lambda i, j: (i, j)),
        out_shape=jax.ShapeDtypeStruct((t, n), F32),
        scratch_shapes=[pltpu.VMEM((tm, d), BF16), pltpu.VMEM((n_conv, 8, tn), F32)]
        + [pltpu.VMEM((tm + 8, sw), F32) for _ in range(tn // sw)],
        compiler_params=_cp("arbitrary", "arbitrary"),
        name="inproj",
    )(x2, nw.reshape(1, d), scale[:, None, :], shift[:, None, :], w16, conv_w)


def _dsa_prep_body(lat_ref, qnw_ref, kvnw_ref, knw_ref, wuq_ref, wuqi_ref, wuk_ref,
                   qabs_ref, qi_ref, ckv_ref, ckvt_ref, ki_ref, wi_ref,
                   *, qr, kvr, di, nh, dh, nhi, q_scale, i_scale, w_scale, wi_off):
    lat = lat_ref[...]
    cq_t = _rms(lat[:, :qr], qnw_ref[...]).T.astype(BF16)
    q_t = _dot(wuq_ref[...], cq_t)
    for h in range(nh):
        q_h = q_t[h * dh:(h + 1) * dh].astype(BF16)
        qabs_ref[h] = (_dot(wuk_ref[h], q_h) * q_scale).astype(BF16)
    qi_t = _dot(wuqi_ref[...], cq_t) * i_scale
    for h in range(nhi):
        qi_ref[h] = qi_t[h * di:(h + 1) * di].astype(BF16)
    ckv = _rms(lat[:, qr:qr + kvr], kvnw_ref[...])
    ckv_ref[...] = ckv.astype(BF16)
    ckvt_ref[0:kvr, :] = ckv.T.astype(BF16)
    ckvt_ref[kvr:, :] = jnp.ones((ONES_ROWS, ckvt_ref.shape[1]), BF16)
    ki_ref[...] = _rms(lat[:, qr + kvr:qr + kvr + di], knw_ref[...]).astype(BF16)
    misc_t = lat[:, qr + kvr + di:].T
    wi_ref[...] = misc_t[wi_off:wi_off + nhi] * w_scale


def _dsa_prep(proj, bsz, seq, qnw, kvnw, knw, wuq_t, wuqi_t, wuk_r, dims):
    qr, kvr, di, nh, dh, nhi, wi_off, lat_blk = dims
    tm = min(seq, 512)
    ns = seq // tm
    lat_w = qr + kvr + di + LANES
    body = functools.partial(
        _dsa_prep_body, qr=qr, kvr=kvr, di=di, nh=nh, dh=dh, nhi=nhi,
        q_scale=dh ** -0.5 * LOG2E, i_scale=di ** -0.5, w_scale=nhi ** -0.5, wi_off=wi_off)
    full = lambda shape: pl.BlockSpec(shape, lambda b, s: (0,) * len(shape))
    return pl.pallas_call(
        body,
        grid=(bsz, ns),
        in_specs=[pl.BlockSpec((tm, lat_w), lambda b, s: (b * ns + s, lat_blk)),
                  full((1, qr)), full((1, kvr)), full((1, di)),
                  full((nh * dh, qr)), full((nhi * di, qr)), full((nh, kvr, dh))],
        out_specs=[pl.BlockSpec((None, nh, kvr, tm), lambda b, s: (b, 0, 0, s)),
                   pl.BlockSpec((None, nhi, di, tm), lambda b, s: (b, 0, 0, s)),
                   pl.BlockSpec((None, tm, kvr), lambda b, s: (b, s, 0)),
                   pl.BlockSpec((None, kvr + ONES_ROWS, tm), lambda b, s: (b, 0, s)),
                   pl.BlockSpec((None, tm, di), lambda b, s: (b, s, 0)),
                   pl.BlockSpec((None, nhi, tm), lambda b, s: (b, 0, s))],
        out_shape=[jax.ShapeDtypeStruct((bsz, nh, kvr, seq), BF16),
                   jax.ShapeDtypeStruct((bsz, nhi, di, seq), BF16),
                   jax.ShapeDtypeStruct((bsz, seq, kvr), BF16),
                   jax.ShapeDtypeStruct((bsz, kvr + ONES_ROWS, seq), BF16),
                   jax.ShapeDtypeStruct((bsz, seq, di), BF16),
                   jax.ShapeDtypeStruct((bsz, nhi, seq), F32)],
        compiler_params=_cp("parallel", "parallel"),
        name="dsa_prep",
    )(proj, qnw.reshape(1, qr), kvnw.reshape(1, kvr), knw.reshape(1, di), wuq_t, wuqi_t, wuk_r)


def _dsa_attn_body(ki_ref, ckv_ref, ckvt_ref, qi_ref, qabs_ref, wi_ref, wuv_ref, o_ref,
                   keys_ref, k16_ref, acc_ref, m_ref, ot_ref, thr_ref, jlim_ref,
                   *, tq, tk, at, nh, nhi, dh, kvr, topk, seq_bits):
    qb = pl.program_id(1)
    nk = ((qb + 1) * tq + tk - 1) // tk
    q_pos = qb * tq + lax.broadcasted_iota(I32, (tk, tq), 1)
    k_iota = lax.broadcasted_iota(I32, (tk, tq), 0)

    def score_tile(kt, _):
        ki_t = ki_ref[pl.ds(pl.multiple_of(kt * tk, tk), tk), :]
        acc = jnp.zeros((tk, tq), F32)
        for h in range(nhi):
            rel = _dot(ki_t, qi_ref[h])
            acc = acc + wi_ref[h:h + 1, :] * jnp.maximum(rel, 0.0)
        acc = jnp.where(acc == 0.0, 0.0, acc)
        bits = lax.bitcast_convert_type(acc, I32)
        causal = kt * tk + k_iota <= q_pos
        key = jnp.where(bits < 0, bits ^ INT_MAX, bits)
        keys_ref[pl.ds(pl.multiple_of(kt * tk, tk), tk), :] = jnp.where(causal, key, INT_MIN)
        top = lax.bitcast_convert_type(bits & -65536, F32)
        k16_ref[pl.ds(pl.multiple_of(kt * tk, tk), tk), :] = jnp.where(causal, top, jnp.nan).astype(BF16)
        return 0

    lax.fori_loop(0, nk, score_tile, 0)

    one16 = jnp.ones((tk, tq), BF16)
    zero16 = jnp.zeros((tk, tq), BF16)

    def count16(cand_bf):
        cand_b = jnp.broadcast_to(cand_bf, (tk, tq))

        def tile(kt, c):
            k_t = k16_ref[pl.ds(pl.multiple_of(kt * tk, tk), tk), :]
            hit = jnp.where(k_t >= cand_b, one16, zero16)
            part = hit[0:16]
            for r in range(1, tk // 16):
                part = part + hit[r * 16:(r + 1) * 16]
            return c + part.astype(F32)
        c16 = lax.fori_loop(0, nk, tile, jnp.zeros((16, tq), F32))
        return jnp.sum(c16, axis=0, keepdims=True)

    def hi_step(it, hi):
        cand = hi + lax.shift_left(jnp.int32(1), 15 - it)
        pattern = jnp.where(cand >= 0, cand, cand ^ 0x7FFF) & 0xFFFF
        cand_bf = lax.bitcast_convert_type(lax.shift_left(pattern, 16), F32).astype(BF16)
        return jnp.where(count16(cand_bf) >= topk, cand, hi)

    hi16 = lax.fori_loop(0, 16, hi_step, jnp.full((1, tq), -(2 ** 15), I32))

    def count(pred_fn):
        def tile(kt, c):
            k_t = keys_ref[pl.ds(pl.multiple_of(kt * tk, tk), tk), :]
            hit = jnp.where(pred_fn(k_t, kt), 1, 0).astype(I32)
            return c + jnp.sum(hit.reshape(tk // 8, 8, tq), axis=0)
        c8 = lax.fori_loop(0, nk, tile, jnp.zeros((8, tq), I32))
        return jnp.sum(c8, axis=0, keepdims=True)

    def bit_cond(carry):
        it, _, settled = carry
        return (it < 32) & (jnp.min(settled) == 0)

    def bit_step(carry):
        it, thr, settled = carry
        for _ in range(4):
            cand = thr ^ lax.shift_left(jnp.int32(1), 31 - it)
            cnt = count(lambda k_t, kt: k_t >= cand)
            take = (cnt >= topk) & (settled == 0)
            settled = jnp.where(take & (cnt == topk), 1, settled)
            thr = jnp.where(take, cand, thr)
            it = it + 1
        return it, thr, settled

    _, thr, _ = lax.while_loop(
        bit_cond, bit_step, (jnp.int32(16), lax.shift_left(hi16, 16), jnp.zeros((1, tq), I32)))
    thr = jnp.maximum(thr, INT_MIN + 1)
    cnt_gt = count(lambda k_t, kt: k_t > thr)
    cnt_ge = count(lambda k_t, kt: k_t >= thr)
    thr_ref[...] = thr
    jlim_ref[...] = jnp.full((1, tq), INT_MAX, I32)

    @pl.when(jnp.max(cnt_ge) > topk)
    def _():
        need = topk - cnt_gt

        def pos_step(it, p):
            cand = p + lax.shift_left(jnp.int32(1), seq_bits - 1 - it)
            cnt = count(lambda k_t, kt: (k_t == thr) & (kt * tk + k_iota < cand))
            return jnp.where(cnt < need, cand, p)

        jlim_ref[...] = lax.fori_loop(0, seq_bits, pos_step, jnp.zeros((1, tq), I32))

    m_ref[...] = jnp.full(m_ref.shape, -jnp.inf, F32)
    acc_ref[...] = jnp.zeros(acc_ref.shape, F32)

    na = (nk + at - 1) // at

    def blank(kt, _):
        keys_ref[pl.ds(pl.multiple_of(kt * tk, tk), tk), :] = jnp.full((tk, tq), INT_MIN, I32)
        return 0

    lax.fori_loop(nk, na * at, blank, 0)
    ta = at * tk
    a_iota = lax.broadcasted_iota(I32, (ta, tq), 0)

    def attn_tile(kt, _):
        off = pl.multiple_of(kt * ta, ta)
        k_t = keys_ref[pl.ds(off, ta), :]
        thr_b = thr_ref[...]
        mask = (k_t > thr_b) | ((k_t == thr_b) & (kt * ta + a_iota <= jlim_ref[...]))
        bias = jnp.where(mask, 0.0, -jnp.inf)
        ckv_t = ckv_ref[pl.ds(off, ta), :]
        ckvt_t = ckvt_ref[:, pl.ds(off, ta)]
        logits = _dot(ckv_t, qabs_ref[0])
        for h in range(nh):
            s = logits + bias
            if h + 1 < nh:
                logits = _dot(ckv_t, qabs_ref[h + 1])
            m_old = m_ref[h]
            m_new = jnp.maximum(m_old, jnp.max(s, axis=0, keepdims=True))
            m_use = jnp.where(m_new == -jnp.inf, 0.0, m_new)
            p = jnp.exp2(s - m_use)
            alpha = jnp.exp2(m_old - m_use)
            acc_ref[h] = alpha * acc_ref[h] + _dot(ckvt_t, p.astype(BF16))
            m_ref[h] = m_new
        return 0

    lax.fori_loop(0, na, attn_tile, 0)

    for h in range(nh):
        acc = acc_ref[h]
        o_lat = (acc[:kvr] / acc[kvr:kvr + 1]).astype(BF16)
        ot_ref[h * dh:(h + 1) * dh, :] = _dot(wuv_ref[h], o_lat)
    o_ref[...] = ot_ref[...].T.astype(BF16)


def _dsa_attn(ki, ckv, ckvt, qi_t, qabs_t, wi_t, wuv_t):
    bsz, seq, di = ki.shape
    kvr = ckv.shape[2]
    nh, dh = wuv_t.shape[0], wuv_t.shape[1]
    nhi = qi_t.shape[1]
    tq = min(seq // 2, 256)
    tk = tq
    at = min(ATTN_TILES, seq // tk)
    assert seq % (at * tk) == 0
    topk = min(IDX_TOPK, seq // 4)
    body = functools.partial(_dsa_attn_body, tq=tq, tk=tk, at=at, nh=nh, nhi=nhi, dh=dh, kvr=kvr, topk=topk,
                             seq_bits=max(1, (seq - 1).bit_length()))
    return pl.pallas_call(
        body,
        grid=(bsz, seq // tq),
        in_specs=[pl.BlockSpec((None, seq, di), lambda b, q: (b, 0, 0)),
                  pl.BlockSpec((None, seq, kvr), lambda b, q: (b, 0, 0)),
                  pl.BlockSpec((None, kvr + ONES_ROWS, seq), lambda b, q: (b, 0, 0)),
                  pl.BlockSpec((None, nhi, di, tq), lambda b, q: (b, 0, 0, q)),
                  pl.BlockSpec((None, nh, kvr, tq), lambda b, q: (b, 0, 0, q)),
                  pl.BlockSpec((None, nhi, tq), lambda b, q: (b, 0, q)),
                  pl.BlockSpec((nh, dh, kvr), lambda b, q: (0, 0, 0))],
        out_specs=pl.BlockSpec((None, tq, nh * dh), lambda b, q: (b, q, 0)),
        out_shape=jax.ShapeDtypeStruct((bsz, seq, nh * dh), BF16),
        scratch_shapes=[pltpu.VMEM((seq, tq), I32),
                        pltpu.VMEM((seq, tq), BF16),
                        pltpu.VMEM((nh, kvr + ONES_ROWS, tq), F32),
                        pltpu.VMEM((nh, 1, tq), F32),
                        pltpu.VMEM((nh * dh, tq), F32),
                        pltpu.VMEM((1, tq), I32),
                        pltpu.VMEM((1, tq), I32)],
        compiler_params=_cp("parallel", "parallel"),
        name="dsa_attn",
    )(ki, ckv, ckvt, qi_t, qabs_t, wi_t, wuv_t)


GDN_GROUP = 16
GDN_INV_BATCH = 32


def _gdn_gate_body(m_ref, alog_ref, dtb_ref, gc_ref, beta_ref, gct_ref, *, ts, chunk, nhv, ng):
    x = m_ref[...]
    z = x + dtb_ref[...]
    softplus = jnp.maximum(z, 0.0) + jnp.log(1.0 + jnp.exp(-jnp.abs(z)))
    g = -jnp.exp(alog_ref[...]) * softplus
    pos = lax.broadcasted_iota(I32, (ts, LANES), 0) % chunk
    d = 1
    while d < chunk:
        g = g + jnp.where(pos >= d, pltpu.roll(g, d, axis=0), 0.0)
        d *= 2
    gct_ref[...] = g.T
    beta = jax.nn.sigmoid(x)
    for j in range(nhv // ng):
        gc_ref[:, j * LANES:(j + 1) * LANES] = pltpu.roll(g, (LANES - j * ng) % LANES, axis=1)
        beta_ref[:, j * LANES:(j + 1) * LANES] = pltpu.roll(beta, (2 * LANES - nhv - j * ng) % LANES, axis=1)


def _gdn_gates(proj, alog_p, dtb_p, bsz, seq, misc_blk, nhv):
    ts = min(seq, 512)
    ns = seq // ts
    ng = GDN_GROUP
    gw = nhv // ng * LANES
    body = functools.partial(_gdn_gate_body, ts=ts, chunk=GDN_CHUNK, nhv=nhv, ng=ng)
    return pl.pallas_call(
        body,
        grid=(bsz, ns),
        in_specs=[pl.BlockSpec((ts, LANES), lambda b, s: (b * ns + s, misc_blk)),
                  pl.BlockSpec((1, LANES), lambda b, s: (0, 0)),
                  pl.BlockSpec((1, LANES), lambda b, s: (0, 0))],
        out_specs=[pl.BlockSpec((ts, gw), lambda b, s: (b * ns + s, 0)),
                   pl.BlockSpec((ts, gw), lambda b, s: (b * ns + s, 0)),
                   pl.BlockSpec((None, LANES, ts), lambda b, s: (b, 0, s))],
        out_shape=[jax.ShapeDtypeStruct((bsz * seq, gw), F32),
                   jax.ShapeDtypeStruct((bsz * seq, gw), F32),
                   jax.ShapeDtypeStruct((bsz, LANES, seq), F32)],
        compiler_params=_cp("parallel", "parallel"),
        name="gdn_gates",
    )(proj, alog_p, dtb_p)


def _gdn_core_body(q_ref, k_ref, v_ref, gc_ref, beta_ref, gr_ref, z_ref, nw_ref, o_ref, *scratch,
                   ts, chunk, hd, ng, rep, gr_steps):
    st_refs = scratch[:ng]
    gcb_ref, bb_ref, a_ref, u_ref, w_ref, qkm_ref, rhs_ref, qd_ref, ktt_ref, egl_ref = scratch[ng:]
    nc = ts // chunk
    gr_base = (pl.program_id(2) % gr_steps) * nc

    @pl.when(pl.program_id(2) == 0)
    def _():
        for st_ref in st_refs:
            st_ref[...] = jnp.zeros(st_ref.shape, F32)

    for g in range(ng):
        gcb_ref[g] = jnp.broadcast_to(gc_ref[:, g:g + 1], (ts, LANES))
        bb_ref[g] = jnp.broadcast_to(beta_ref[:, g:g + 1], (ts, LANES))

    ri = lax.broadcasted_iota(I32, (chunk, chunk), 0)
    ci = lax.broadcasted_iota(I32, (chunk, chunk), 1)
    incl = ri >= ci
    strict = ri > ci
    eye = jnp.where(ri == ci, 1.0, 0.0).astype(F32)

    def prep(c, _):
        rows = pl.ds(pl.multiple_of(c * chunk, chunk), chunk)
        for qh in range(ng // rep):
            q = q_ref[rows, qh * hd:(qh + 1) * hd]
            k = k_ref[rows, qh * hd:(qh + 1) * hd]
            k16 = k.astype(BF16)
            kk = _dot_nt(k16, k16)
            qk = _dot_nt(q.astype(BF16), k16)
            a_pair = []
            for vh in range(rep):
                g = qh * rep + vh
                n = c * ng + g
                gcb = gcb_ref[g, rows, :]
                bb = bb_ref[g, rows, :]
                g_row = gr_ref[g, pl.ds(gr_base + c, 1), :]
                decay = jnp.where(incl, jnp.exp(gcb[:, :chunk] - g_row), 0.0)
                a_pair.append(jnp.where(strict, kk * bb[:, :chunk] * decay, 0.0))
                qkm_ref[n] = (qk * decay).astype(BF16)
                eg = jnp.exp(gcb)
                rhs_ref[n, :, 0:hd] = (v_ref[rows, g * hd:(g + 1) * hd] * bb).astype(BF16)
                rhs_ref[n, :, hd:2 * hd] = (k * bb * eg).astype(BF16)
                qd_ref[n] = (q * eg).astype(BF16)
                g_last = gcb[chunk - 1:chunk, :]
                ktt_ref[n] = (k * jnp.exp(g_last - gcb)).T.astype(BF16)
                egl_ref[n] = jnp.broadcast_to(jnp.exp(g_last), (8, LANES))
            a_ref[c * (ng // rep) + qh] = jnp.concatenate(a_pair, axis=1)
        return 0

    lax.fori_loop(0, nc, prep, 0)

    n_dbl = max(1, (chunk - 1).bit_length()) - 1

    lane2 = lax.broadcasted_iota(I32, (chunk, 2 * chunk), 1)
    left = lane2 < chunk
    eye2 = jnp.where(lax.broadcasted_iota(I32, (chunk, 2 * chunk), 0) == lane2 % chunk, 1.0, 0.0).astype(F32)

    def blockdiag(x):
        return jnp.concatenate([jnp.where(left, x, 0.0), jnp.where(left, 0.0, x)], axis=0).astype(BF16)

    def invert(ib, _):
        ms_ = [ib * (GDN_INV_BATCH // 2) + j for j in range(GDN_INV_BATCH // 2)]
        pw = [a_ref[m] for m in ms_]
        tm = [eye2 - a for a in pw]
        pw = [_dot(a.astype(BF16), blockdiag(a)) for a in pw]
        for lvl in range(n_dbl):
            last = lvl == n_dbl - 1
            lhs = [(t if last else jnp.concatenate([t, x], axis=0)).astype(BF16) for t, x in zip(tm, pw)]
            prod = [_dot(y, blockdiag(x)) for y, x in zip(lhs, pw)]
            tm = [t + y[:chunk] for t, y in zip(tm, prod)]
            pw = [y[chunk:] for y in prod]
        for m, t in zip(ms_, tm):
            zero = jnp.zeros((chunk, 2 * hd), BF16)
            rhs2 = jnp.concatenate([jnp.concatenate([rhs_ref[2 * m], zero], axis=1),
                                    jnp.concatenate([zero, rhs_ref[2 * m + 1]], axis=1)], axis=0)
            uw = _dot(t.astype(BF16), rhs2)
            for j in range(2):
                u_ref[2 * m + j] = uw[:, 2 * j * hd:(2 * j + 1) * hd]
                w_ref[2 * m + j] = uw[:, (2 * j + 1) * hd:(2 * j + 2) * hd].astype(BF16)
        return 0

    lax.fori_loop(0, nc * ng // GDN_INV_BATCH, invert, 0)

    nw = nw_ref[...]

    def step(c, _):
        rows = pl.ds(pl.multiple_of(c * chunk, chunk), chunk)
        heads = range(ng)
        ns_ = [c * ng + g for g in heads]
        state = [st_refs[g][...] for g in heads]
        st16 = [x.astype(BF16) for x in state]
        ws = [_dot(w_ref[n], x) for n, x in zip(ns_, st16)]
        qs = [_dot(qd_ref[n], x) for n, x in zip(ns_, st16)]
        vn16 = [(u_ref[n] - x).astype(BF16) for n, x in zip(ns_, ws)]
        o = [y + _dot(qkm_ref[n], x) for n, x, y in zip(ns_, vn16, qs)]
        ds = [_dot(ktt_ref[n], x) for n, x in zip(ns_, vn16)]
        for g in heads:
            st_refs[g][...] = state[g] * egl_ref[ns_[g]][0:1, :] + ds[g]
            zz = z_ref[rows, g * hd:(g + 1) * hd]
            o_ref[rows, g * hd:(g + 1) * hd] = (_rms(o[g], nw) * (zz * _sigmoid(zz))).astype(o_ref.dtype)
        return 0

    lax.fori_loop(0, nc, step, 0)


def _gdn_core(proj, gc, beta, gct4, norm_w, bsz, seq, nqk, nhv, hd, q_col, z_col):
    assert hd == LANES
    chunk = GDN_CHUNK
    ng = GDN_GROUP
    rep = nhv // nqk
    ts = min(seq, 256)
    ns = seq // ts
    n = (ts // chunk) * ng
    assert rep == 2 and ng % rep == 0 and nhv % ng == 0 and n % GDN_INV_BATCH == 0
    qw, vw = (ng // rep) * hd, ng * hd
    k_col, v_col = q_col + nqk * hd, q_col + 2 * nqk * hd
    assert q_col % qw == 0 and k_col % qw == 0 and v_col % vw == 0 and z_col % vw == 0
    gr_steps = max(1, 8 * chunk // ts)
    assert ns % gr_steps == 0
    body = functools.partial(_gdn_core_body, ts=ts, chunk=chunk, hd=hd, ng=ng, rep=rep, gr_steps=gr_steps)
    row = lambda b, h, s: b * ns + s
    return pl.pallas_call(
        body,
        grid=(bsz, nhv // ng, ns),
        in_specs=[pl.BlockSpec((ts, qw), lambda b, h, s: (row(b, h, s), q_col // qw + h)),
                  pl.BlockSpec((ts, qw), lambda b, h, s: (row(b, h, s), k_col // qw + h)),
                  pl.BlockSpec((ts, vw), lambda b, h, s: (row(b, h, s), v_col // vw + h)),
                  pl.BlockSpec((ts, LANES), lambda b, h, s: (row(b, h, s), h)),
                  pl.BlockSpec((ts, LANES), lambda b, h, s: (row(b, h, s), h)),
                  pl.BlockSpec((None, ng, gr_steps * ts // chunk, chunk), lambda b, h, s: (b, h, s // gr_steps, 0)),
                  pl.BlockSpec((ts, vw), lambda b, h, s: (row(b, h, s), z_col // vw + h)),
                  pl.BlockSpec((1, hd), lambda b, h, s: (0, 0))],
        out_specs=pl.BlockSpec((ts, vw), lambda b, h, s: (row(b, h, s), h)),
        out_shape=jax.ShapeDtypeStruct((bsz * seq, nhv * hd), BF16),
        scratch_shapes=[pltpu.VMEM((hd, hd), F32) for _ in range(ng)] + [
                        pltpu.VMEM((ng, ts, LANES), F32),
                        pltpu.VMEM((ng, ts, LANES), F32),
                        pltpu.VMEM((n // 2, chunk, 2 * chunk), F32),
                        pltpu.VMEM((n, chunk, hd), F32),
                        pltpu.VMEM((n, chunk, hd), BF16),
                        pltpu.VMEM((n, chunk, chunk), BF16),
                        pltpu.VMEM((n, chunk, 2 * hd), BF16),
                        pltpu.VMEM((n, chunk, hd), BF16),
                        pltpu.VMEM((n, hd, chunk), BF16),
                        pltpu.VMEM((n, 8, LANES), F32)],
        compiler_params=_cp("parallel", "parallel", "arbitrary"),
        name="gdn_core",
    )(proj, proj, proj, gc, beta, gct4, proj, norm_w.reshape(1, hd))


def _merge_body(a_ref, b_ref, wa_ref, wb_ref, ga_ref, gb_ref, o_ref):
    ya = _dot(a_ref[...], wa_ref[...])
    yb = _dot(b_ref[...], wb_ref[...])
    o_ref[...] = (_sigmoid(ga_ref[...]) * ya + _sigmoid(gb_ref[...]) * yb).astype(o_ref.dtype)


def _merge(o_a, o_b, wa16, wb16, proj, ga_col0, gb_col0):
    t, ka = o_a.shape
    kb = o_b.shape[1]
    d = wa16.shape[1]
    tm = min(t, 512)
    tn = _tile(d, 512)
    ga0, gb0 = ga_col0 // tn, gb_col0 // tn
    return pl.pallas_call(
        _merge_body,
        grid=(t // tm, d // tn),
        in_specs=[pl.BlockSpec((tm, ka), lambda i, j: (i, 0)),
                  pl.BlockSpec((tm, kb), lambda i, j: (i, 0)),
                  pl.BlockSpec((ka, tn), lambda i, j: (0, j)),
                  pl.BlockSpec((kb, tn), lambda i, j: (0, j)),
                  pl.BlockSpec((tm, tn), lambda i, j: (i, ga0 + j)),
                  pl.BlockSpec((tm, tn), lambda i, j: (i, gb0 + j))],
        out_specs=pl.BlockSpec((tm, tn), lambda i, j: (i, j)),
        out_shape=jax.ShapeDtypeStruct((t, d), BF16),
        compiler_params=_cp("parallel", "parallel"),
        name="merge",
    )(o_a, o_b, wa16, wb16, proj, proj)


def _outproj_router_body(m_ref, x_ref, wo_ref, g1_ref, nw_ref, sc_ref, sh_ref, wr_ref, br_ref,
                         x1_ref, h2_ref, te_ref, tg_ref, *, topk):
    x1 = x_ref[...] + g1_ref[...] * _dot(m_ref[...], wo_ref[...])
    x1_ref[...] = x1
    h2 = _rms(x1, nw_ref[...]) * (1.0 + sc_ref[...]) + sh_ref[...]
    h2_ref[...] = h2
    logits = _dot(h2.astype(BF16), wr_ref[...]) + br_ref[...]
    lane = lax.broadcasted_iota(I32, logits.shape, 1)
    te = jnp.zeros(logits.shape, I32)
    tg = jnp.zeros(logits.shape, F32)
    denom = jnp.zeros((logits.shape[0], 1), F32)
    m0 = None
    for kk in range(topk):
        mx = jnp.max(logits, axis=-1, keepdims=True)
        idx = jnp.min(jnp.where(logits == mx, lane, LANES), axis=-1, keepdims=True)
        if kk == 0:
            m0 = mx
        e = jnp.exp(mx - m0)
        denom = denom + e
        te = jnp.where(lane == kk, idx, te)
        tg = jnp.where(lane == kk, e, tg)
        logits = jnp.where(lane == idx, -jnp.inf, logits)
    te_ref[...] = te
    tg_ref[...] = tg / denom


def _outproj_router(merged, x2, wo16, gate1, nw, scale, shift, wr16, br_p, seq):
    t, d = x2.shape
    tm = min(seq, 256)
    per_b = seq // tm
    vec = lambda: pl.BlockSpec((None, 1, d), lambda i: (i // per_b, 0, 0))
    return pl.pallas_call(
        functools.partial(_outproj_router_body, topk=TOP_K),
        grid=(t // tm,),
        in_specs=[pl.BlockSpec((tm, d), lambda i: (i, 0)),
                  pl.BlockSpec((tm, d), lambda i: (i, 0)),
                  pl.BlockSpec((d, d), lambda i: (0, 0)),
                  vec(),
                  pl.BlockSpec((1, d), lambda i: (0, 0)),
                  vec(), vec(),
                  pl.BlockSpec((d, LANES), lambda i: (0, 0)),
                  pl.BlockSpec((1, LANES), lambda i: (0, 0))],
        out_specs=[pl.BlockSpec((tm, d), lambda i: (i, 0)),
                   pl.BlockSpec((tm, d), lambda i: (i, 0)),
                   pl.BlockSpec((tm, LANES), lambda i: (i, 0)),
                   pl.BlockSpec((tm, LANES), lambda i: (i, 0))],
        out_shape=[jax.ShapeDtypeStruct((t, d), F32),
                   jax.ShapeDtypeStruct((t, d), F32),
                   jax.ShapeDtypeStruct((t, LANES), I32),
                   jax.ShapeDtypeStruct((t, LANES), F32)],
        compiler_params=_cp("parallel"),
        name="outproj_router",
    )(merged, x2, wo16, gate1[:, None, :], nw.reshape(1, d), scale[:, None, :], shift[:, None, :], wr16, br_p)


def _moe_rank_body(te_ref, pos_ref, cnt_ref, carry_ref, *, tt, topk):
    @pl.when(pl.program_id(0) == 0)
    def _():
        carry_ref[...] = jnp.zeros(carry_ref.shape, F32)

    te = te_ref[...]
    lane = lax.broadcasted_iota(I32, (tt, LANES), 1)
    onehot = jnp.zeros((tt, LANES), F32)
    for kk in range(topk):
        onehot = onehot + jnp.where(lane == te[:, kk:kk + 1], 1.0, 0.0)
    r = lax.broadcasted_iota(I32, (tt, tt), 0)
    c = lax.broadcasted_iota(I32, (tt, tt), 1)
    below = jnp.where(r > c, 1.0, 0.0).astype(BF16)
    rank = _dot(below, onehot.astype(BF16)) + carry_ref[0:1, :]
    pos = jnp.zeros((tt, LANES), I32)
    for kk in range(topk):
        p = jnp.sum(jnp.where(lane == te[:, kk:kk + 1], rank, 0.0), axis=-1, keepdims=True)
        pos = jnp.where(lane == kk, p.astype(I32), pos)
    pos_ref[...] = pos
    carry_ref[...] = carry_ref[...] + jnp.sum(onehot, axis=0, keepdims=True)
    cnt_ref[...] = carry_ref[...]


def _moe_rank(te):
    t = te.shape[0]
    tt = min(t, 512)
    return pl.pallas_call(
        functools.partial(_moe_rank_body, tt=tt, topk=TOP_K),
        grid=(t // tt,),
        in_specs=[pl.BlockSpec((tt, LANES), lambda i: (i, 0))],
        out_specs=[pl.BlockSpec((tt, LANES), lambda i: (i, 0)),
                   pl.BlockSpec((8, LANES), lambda i: (0, 0))],
        out_shape=[jax.ShapeDtypeStruct((t, LANES), I32),
                   jax.ShapeDtypeStruct((8, LANES), F32)],
        scratch_shapes=[pltpu.VMEM((8, LANES), F32)],
        compiler_params=_cp("arbitrary"),
        name="moe_rank",
    )(te)


def _moe_dispatch_body(dest_ref, fill_ref, h_ref, x_hbm, zero_ref, sem, fill_sem, *, tt, topk, bm, n_blocks):
    @pl.when(pl.program_id(0) == 0)
    def _():
        zero_ref[...] = jnp.zeros(zero_ref.shape, zero_ref.dtype)

        def fill_copy(blk):
            return pltpu.make_async_copy(zero_ref, x_hbm.at[pl.ds(pl.multiple_of(blk * bm, bm), bm)], fill_sem)

        def start(blk, _):
            @pl.when(fill_ref[blk] != 0)
            def _():
                fill_copy(blk).start()
            return 0

        def drain(blk, _):
            @pl.when(fill_ref[blk] != 0)
            def _():
                fill_copy(blk).wait()
            return 0

        lax.fori_loop(0, n_blocks, start, 0)
        lax.fori_loop(0, n_blocks, drain, 0)

    base = pl.program_id(0) * tt * topk

    def issue(t, _):
        for kk in range(topk):
            dst = x_hbm.at[pl.ds(dest_ref[base + t * topk + kk], 1)]
            pltpu.make_async_copy(h_ref.at[pl.ds(t, 1)], dst, sem).start()
        return 0

    lax.fori_loop(0, tt, issue, 0, unroll=2)
    for _ in range(topk):
        pltpu.make_async_copy(h_ref, x_hbm.at[pl.ds(0, tt)], sem).wait()


def _moe_dispatch(dest_flat, fill_flags, h2, n_rows):
    t, d = h2.shape
    tt = min(t, 128)
    bm = MOE_BM
    grid_spec = pltpu.PrefetchScalarGridSpec(
        num_scalar_prefetch=2,
        grid=(t // tt,),
        in_specs=[pl.BlockSpec((tt, d), lambda i, dest, fill: (i, 0))],
        out_specs=pl.BlockSpec(memory_space=pl.ANY),
        scratch_shapes=[pltpu.VMEM((bm, d), F32), pltpu.SemaphoreType.DMA(()), pltpu.SemaphoreType.DMA(())])
    return pl.pallas_call(
        functools.partial(_moe_dispatch_body, tt=tt, topk=TOP_K, bm=bm, n_blocks=n_rows // bm),
        grid_spec=grid_spec,
        out_shape=jax.ShapeDtypeStruct((n_rows, d), F32),
        compiler_params=_cp("arbitrary"),
        name="moe_dispatch",
    )(dest_flat, fill_flags, h2)


def _moe_gu_body(be_ref, na_ref, x_ref, wg_ref, wu_ref, bg_ref, bu_ref, act_ref, wg16_ref, wu16_ref):
    i = pl.program_id(1)
    prev = be_ref[jnp.maximum(i - 1, 0)]

    @pl.when((i == 0) | (be_ref[i] != prev))
    def _():
        wg16_ref[...] = wg_ref[...].astype(BF16)
        wu16_ref[...] = wu_ref[...].astype(BF16)

    @pl.when(i < na_ref[0])
    def _():
        x = x_ref[...].astype(BF16)
        g = jnp.minimum(_dot(x, wg16_ref[...]) + bg_ref[...], SWIGLU_LIMIT)
        u = jnp.clip(_dot(x, wu16_ref[...]) + bu_ref[...], -SWIGLU_LIMIT, SWIGLU_LIMIT)
        act_ref[...] = ((u + 1.0) * (g * _sigmoid(SWIGLU_ALPHA * g))).astype(act_ref.dtype)

    @pl.when(i >= na_ref[0])
    def _():
        act_ref[...] = jnp.zeros(act_ref.shape, act_ref.dtype)


def _moe_gu(block_e, n_act, x_rows, w_gu, b_gu):
    n_rows, d = x_rows.shape
    ne, _, ff2 = w_gu.shape
    ff = ff2 // 2
    bm = MOE_BM
    tn = _tile(ff, 1024)
    nt = ff // tn
    rowblk = lambda j, i, be, na: (jnp.minimum(i, na[0] - 1), 0)
    grid_spec = pltpu.PrefetchScalarGridSpec(
        num_scalar_prefetch=2,
        grid=(nt, n_rows // bm),
        in_specs=[pl.BlockSpec((bm, d), rowblk),
                  pl.BlockSpec((None, d, tn), lambda j, i, be, na: (be[i], 0, j)),
                  pl.BlockSpec((None, d, tn), lambda j, i, be, na: (be[i], 0, nt + j)),
                  pl.BlockSpec((None, 1, tn), lambda j, i, be, na: (be[i], 0, j)),
                  pl.BlockSpec((None, 1, tn), lambda j, i, be, na: (be[i], 0, nt + j))],
        out_specs=pl.BlockSpec((bm, tn), lambda j, i, be, na: (i, j)),
        scratch_shapes=[pltpu.VMEM((d, tn), BF16), pltpu.VMEM((d, tn), BF16)])
    return pl.pallas_call(
        _moe_gu_body,
        grid_spec=grid_spec,
        out_shape=jax.ShapeDtypeStruct((n_rows, ff), BF16),
        compiler_params=_cp("arbitrary", "arbitrary"),
        name="moe_gu",
    )(block_e, n_act, x_rows, w_gu, w_gu, b_gu.reshape(ne, 1, ff2), b_gu.reshape(ne, 1, ff2))


def _moe_down_body(be_ref, na_ref, a_ref, w_ref, b_ref, y_ref, w16_ref):
    i = pl.program_id(1)
    prev = be_ref[jnp.maximum(i - 1, 0)]

    @pl.when((i == 0) | (be_ref[i] != prev))
    def _():
        w16_ref[...] = w_ref[...].astype(BF16)

    @pl.when(i < na_ref[0])
    def _():
        y_ref[...] = _dot(a_ref[...], w16_ref[...]) + b_ref[...]

    @pl.when(i >= na_ref[0])
    def _():
        y_ref[...] = jnp.zeros(y_ref.shape, y_ref.dtype)


def _moe_down(block_e, n_act, act, w_down, b_down):
    n_rows, ff = act.shape
    ne, _, d = w_down.shape
    bm = MOE_BM
    tn = _tile(d, 2048)
    grid_spec = pltpu.PrefetchScalarGridSpec(
        num_scalar_prefetch=2,
        grid=(d // tn, n_rows // bm),
        in_specs=[pl.BlockSpec((bm, ff), lambda j, i, be, na: (jnp.minimum(i, na[0] - 1), 0)),
                  pl.BlockSpec((None, ff, tn), lambda j, i, be, na: (be[i], 0, j)),
                  pl.BlockSpec((None, 1, tn), lambda j, i, be, na: (be[i], 0, j))],
        out_specs=pl.BlockSpec((bm, tn), lambda j, i, be, na: (i, j)),
        scratch_shapes=[pltpu.VMEM((ff, tn), BF16)])
    return pl.pallas_call(
        _moe_down_body,
        grid_spec=grid_spec,
        out_shape=jax.ShapeDtypeStruct((n_rows, d), F32),
        compiler_params=_cp("arbitrary", "arbitrary"),
        name="moe_down",
    )(block_e, n_act, act, w_down, b_down.reshape(ne, 1, d))


def _moe_combine_body(dest_ref, x1_ref, tg_ref, g2_ref, fw_ref, y_hbm, o_ref, buf_ref, sem, *, tt, topk):
    i = pl.program_id(0)
    slot = i % 2

    def gather(step, sl):
        base = step * tt * topk

        def issue(t, _):
            for kk in range(topk):
                src = y_hbm.at[pl.ds(dest_ref[base + t * topk + kk], 1)]
                pltpu.make_async_copy(src, buf_ref.at[sl, kk, pl.ds(t, 1)], sem.at[sl]).start()
            return 0

        lax.fori_loop(0, tt, issue, 0, unroll=2)

    @pl.when(i == 0)
    def _():
        gather(0, 0)

    @pl.when(i + 1 < pl.num_programs(0))
    def _():
        gather(i + 1, 1 - slot)

    for kk in range(topk):
        pltpu.make_async_copy(y_hbm.at[pl.ds(0, tt)], buf_ref.at[slot, kk], sem.at[slot]).wait()
    tg = tg_ref[...]
    y = tg[:, 0:1] * buf_ref[slot, 0]
    for kk in range(1, topk):
        y = y + tg[:, kk:kk + 1] * buf_ref[slot, kk]
    o_ref[...] = _rms(x1_ref[...] + g2_ref[...] * y, fw_ref[...])


def _moe_combine(dest_flat, x1, tg, gate2, final_w, y_rows, seq):
    t, d = x1.shape
    tt = min(seq, 128)
    per_b = seq // tt
    grid_spec = pltpu.PrefetchScalarGridSpec(
        num_scalar_prefetch=1,
        grid=(t // tt,),
        in_specs=[pl.BlockSpec((tt, d), lambda i, dest: (i, 0)),
                  pl.BlockSpec((tt, LANES), lambda i, dest: (i, 0)),
                  pl.BlockSpec((None, 1, d), lambda i, dest: (i // per_b, 0, 0)),
                  pl.BlockSpec((1, d), lambda i, dest: (0, 0)),
                  pl.BlockSpec(memory_space=pl.ANY)],
        out_specs=pl.BlockSpec((tt, d), lambda i, dest: (i, 0)),
        scratch_shapes=[pltpu.VMEM((2, TOP_K, tt, d), F32), pltpu.SemaphoreType.DMA((2,))])
    return pl.pallas_call(
        functools.partial(_moe_combine_body, tt=tt, topk=TOP_K),
        grid_spec=grid_spec,
        out_shape=jax.ShapeDtypeStruct((t, d), F32),
        compiler_params=_cp("arbitrary"),
        name="moe_combine",
    )(dest_flat, x1, tg, gate2[:, None, :], final_w.reshape(1, d), y_rows)


def _pad_lanes(v, off=0, fill=0.0):
    out = jnp.full((1, LANES), fill, F32)
    return out.at[0, off:off + v.shape[0]].set(v.astype(F32))


def _layer(x2, c, bsz, seq, p, final_w):
    d = x2.shape[1]
    qr, nh, dh = p["w_uq"].shape
    kvr = p["w_uk"].shape[0]
    nhi, di = p["w_uqi"].shape[1:]
    hd = p["gdn_norm_w"].shape[0]
    nhv = p["a_log"].shape[0]
    v_w = nhv * hd
    qk_w = (p["conv_w"].shape[1] - v_w) // 2
    nqk = qk_w // hd
    ne = p["w_router"].shape[1]

    mod = _ada(c, p["ada_w"], p["ada_b"])
    shift1, scale1, gate1, shift2, scale2, gate2 = jnp.split(mod, 6, axis=-1)

    widths = (qr, kvr, di, nhi, qk_w, qk_w, v_w, nhv, nhv, v_w, d, d)
    offs = [0]
    for wd in widths:
        offs.append(offs[-1] + wd)
    col = lambda k: p["w_in"][:, offs[k]:offs[k + 1]]
    pad = jnp.zeros((d, LANES - 2 * nhv - nhi), F32)
    w_in = jnp.concatenate([col(4), col(5), col(6), col(9), col(10), col(11),
                            col(0), col(1), col(2), col(7), col(8), col(3), pad], axis=1).astype(BF16)
    lat_w = qr + kvr + di + LANES
    gq_col = 0
    z_col = gq_col + 2 * qk_w + v_w
    ga_col = z_col + v_w
    gb_col = ga_col + d
    lat_col = gb_col + d
    assert lat_col % lat_w == 0

    proj = _inproj(x2, p["norm1_w"], scale1, shift1, w_in, p["conv_w"], seq, gq_col, qk_w, hd)

    wuq_t = p["w_uq"].reshape(qr, nh * dh).T.astype(BF16)
    wuqi_t = p["w_uqi"].reshape(qr, nhi * di).T.astype(BF16)
    wuk_r = jnp.transpose(p["w_uk"], (1, 0, 2)).astype(BF16)
    wuv_t = jnp.transpose(p["w_uv"], (1, 2, 0)).astype(BF16)
    qabs_t, qi_t, ckv, ckvt, ki, wi_t = _dsa_prep(
        proj, bsz, seq, p["q_lat_norm_w"], p["kv_lat_norm_w"], p["idx_k_norm_w"], wuq_t, wuqi_t, wuk_r,
        (qr, kvr, di, nh, dh, nhi, 2 * nhv, lat_col // lat_w))
    o_a = _dsa_attn(ki, ckv, ckvt, qi_t, qabs_t, wi_t, wuv_t).reshape(bsz * seq, nh * dh)

    gc, beta, gct = _gdn_gates(proj, _pad_lanes(p["a_log"]), _pad_lanes(p["dt_bias"]), bsz, seq,
                               (lat_col + lat_w - LANES) // LANES, nhv)
    gct4 = gct.reshape(bsz, LANES, seq // GDN_CHUNK, GDN_CHUNK)
    o_b = _gdn_core(proj, gc, beta, gct4, p["gdn_norm_w"], bsz, seq, nqk, nhv, hd, gq_col, z_col)

    merged = _merge(o_a, o_b, p["w_branch_a"].astype(BF16), p["w_branch_b"].astype(BF16), proj, ga_col, gb_col)

    wr16 = jnp.zeros((d, LANES), F32).at[:, :ne].set(p["w_router"]).astype(BF16)
    br_p = _pad_lanes(p["b_router"], fill=-1e30)
    x1, h2, te, tg = _outproj_router(merged, x2, p["w_out"].astype(BF16), gate1, p["norm2_w"], scale2, shift2,
                                     wr16, br_p, seq)

    t = bsz * seq
    pos, cnt = _moe_rank(te)
    counts = cnt[0, :ne].astype(I32)
    padded = (counts + MOE_BM - 1) // MOE_BM * MOE_BM
    pad_end = jnp.cumsum(padded)
    pad_start = pad_end - padded
    dest = (pad_start[te[:, :TOP_K]] + pos[:, :TOP_K]).reshape(-1).astype(I32)
    n_blocks = -(-(t * TOP_K) // MOE_BM) + ne
    n_act = (pad_end[-1] // MOE_BM).astype(I32).reshape(1)
    blk = jnp.minimum(jnp.arange(n_blocks, dtype=I32), n_act[0] - 1) * MOE_BM
    block_e = jnp.minimum(jnp.sum(pad_end[None, :] <= blk[:, None], axis=1), ne - 1).astype(I32)
    blk_end = (jnp.arange(n_blocks, dtype=I32) + 1) * MOE_BM
    has_pad = jnp.any((blk_end[:, None] == pad_end[None, :]) & (padded[None, :] > 0), axis=1)
    fill_flags = (has_pad | (blk_end > pad_end[-1])).astype(I32)
    x_rows = _moe_dispatch(dest, fill_flags, h2, n_blocks * MOE_BM)
    act = _moe_gu(block_e, n_act, x_rows, p["w_gu"], p["b_gu"])
    y_rows = _moe_down(block_e, n_act, act, p["w_down"], p["b_down"])
    return _moe_combine(dest, x1, tg, gate2, final_w, y_rows, seq)


@jax.jit
def kernel(x, c, ada_w, ada_b, norm1_w, w_in, q_lat_norm_w, kv_lat_norm_w, idx_k_norm_w, w_uq, w_uqi, w_uk, w_uv, conv_w, a_log, dt_bias, gdn_norm_w, w_branch_a, w_branch_b, w_out, norm2_w, w_router, b_router, w_gu, b_gu, w_down, b_down, final_norm_w):
    bsz, seq, d = x.shape
    stacked = dict(ada_w=ada_w, ada_b=ada_b, norm1_w=norm1_w, w_in=w_in, q_lat_norm_w=q_lat_norm_w,
                   kv_lat_norm_w=kv_lat_norm_w, idx_k_norm_w=idx_k_norm_w, w_uq=w_uq, w_uqi=w_uqi, w_uk=w_uk,
                   w_uv=w_uv, conv_w=conv_w, a_log=a_log, dt_bias=dt_bias, gdn_norm_w=gdn_norm_w,
                   w_branch_a=w_branch_a, w_branch_b=w_branch_b, w_out=w_out, norm2_w=norm2_w,
                   w_router=w_router, b_router=b_router, w_gu=w_gu, b_gu=b_gu, w_down=w_down, b_down=b_down)
    depth = ada_w.shape[0]
    assert depth == 1, "the final norm is fused into the last layer's combine kernel"
    x2 = x.reshape(bsz * seq, d)
    p = {k: v[0] for k, v in stacked.items()}
    out = _layer(x2, c, bsz, seq, p, final_norm_w)
    return out.reshape(bsz, seq, d)
```

```python
import functools

import jax
import jax.numpy as jnp
from jax import lax
from jax.experimental import pallas as pl
from jax.experimental.pallas import tpu as pltpu

F32 = jnp.float32
BF16 = jnp.bfloat16
I32 = jnp.int32

EPS = 1e-6
LANES = 128
VMEM_LIMIT = 56 * 1024 * 1024

IDX_TOPK = 256
GDN_CHUNK = 64
TOP_K = 4
SWIGLU_LIMIT = 7.0
SWIGLU_ALPHA = 1.702
MOE_BM = 512
ATTN_TILES = 2
ONES_ROWS = 16
LOG2E = 1.4426950408889634
INT_MIN = -(2 ** 31)
INT_MAX = 2 ** 31 - 1


def _cp(*sem):
    return pltpu.CompilerParams(dimension_semantics=sem, vmem_limit_bytes=VMEM_LIMIT)


def _tile(n, pref):
    if n <= pref:
        return n
    t = pref - pref % LANES
    while n % t:
        t -= LANES
    return t


def _rms(x, w):
    return x * lax.rsqrt(jnp.mean(x * x, axis=-1, keepdims=True) + EPS) * w


def _sigmoid(x):
    return 0.5 * jnp.tanh(0.5 * x) + 0.5


def _dot(a, b):
    return jnp.dot(a, b, preferred_element_type=F32)


def _dot_nt(a, b):
    return lax.dot_general(a, b, (((1,), (1,)), ((), ())), preferred_element_type=F32)


def _dot_tn(a, b):
    return lax.dot_general(a, b, (((0,), (0,)), ((), ())), preferred_element_type=F32)


def _ada_body(ct_ref, w_ref, b_ref, o_ref, *, nb):
    ct = ct_ref[...]
    s = ct * jax.nn.sigmoid(ct)
    w = w_ref[...]
    for b in range(nb):
        o_ref[b:b + 1, :] = jnp.sum(w * s[:, b:b + 1], axis=0, keepdims=True) + b_ref[...]


def _ada(c, w, b):
    nb, d = c.shape
    n = w.shape[1]
    tn = _tile(n, 1024)
    return pl.pallas_call(
        functools.partial(_ada_body, nb=nb),
        grid=(n // tn,),
        in_specs=[pl.BlockSpec((d, nb), lambda j: (0, 0)),
                  pl.BlockSpec((d, tn), lambda j: (0, j)),
                  pl.BlockSpec((1, tn), lambda j: (0, j))],
        out_specs=pl.BlockSpec((nb, tn), lambda j: (0, j)),
        out_shape=jax.ShapeDtypeStruct((nb, n), F32),
        compiler_params=_cp("parallel"),
        name="ada",
    )(c.T, w, b.reshape(1, n))


def _inproj_body(x_ref, nw_ref, sc_ref, sh_ref, w_ref, cw_ref, o_ref, h_ref, carry_ref, *ubuf_refs,
                 tm, tn, sw, rc, hd, per_b, c0, n_qk, n_conv, q_scale):
    i = pl.program_id(0)
    j = pl.program_id(1)

    @pl.when(j == 0)
    def _():
        h = _rms(x_ref[...], nw_ref[...]) * (1.0 + sc_ref[...]) + sh_ref[...]
        h_ref[...] = h.astype(BF16)

    cj = j - c0
    in_conv = (cj >= 0) & (cj < n_conv)

    @pl.when(jnp.logical_not(in_conv))
    def _():
        o_ref[...] = _dot(h_ref[...], w_ref[...])

    @pl.when(in_conv)
    def _():
        taps = cw_ref.shape[0]

        @pl.when(i % per_b == 0)
        def _():
            carry_ref[cj] = jnp.zeros((8, tn), F32)

        prev = carry_ref[cj]
        scale = jnp.where(cj < n_qk, q_scale, 1.0)

        def conv_tile(normalise):
            tails = []
            for s, ubuf_ref in enumerate(ubuf_refs):
                cols = slice(s * sw, (s + 1) * sw)
                w = cw_ref[:, cols]
                u = _dot(h_ref[...], w_ref[:, cols])
                ubuf_ref[0:8, :] = prev[:, cols]
                ubuf_ref[8:, :] = u
                tails.append(u[tm - 8:, :])
                for r in range(tm // rc):
                    ext = ubuf_ref[r * rc:(r + 1) * rc + 8, :]
                    if taps == 4:
                        ext1 = pltpu.roll(ext, 1, axis=0)
                        near = ext * w[3:4, :] + ext1 * w[2:3, :]
                        far = ext * w[1:2, :] + ext1 * w[0:1, :]
                        y = near[8:] + pltpu.roll(far, 2, axis=0)[8:]
                    else:
                        y = ext[8:] * w[taps - 1:taps, :]
                        for k in range(1, taps):
                            y = y + pltpu.roll(ext, k, axis=0)[8:] * w[taps - 1 - k:taps - k, :]
                    y = y * _sigmoid(y)
                    if not normalise:
                        o_ref[r * rc:(r + 1) * rc, cols] = y
                        continue
                    for hh in range(sw // hd):
                        x = y[:, hh * hd:(hh + 1) * hd]
                        inv = lax.rsqrt(jnp.sum(x * x, axis=-1, keepdims=True) + EPS) * scale
                        c_lo = s * sw + hh * hd
                        o_ref[r * rc:(r + 1) * rc, c_lo:c_lo + hd] = x * inv
            carry_ref[cj] = jnp.concatenate(tails, axis=1)

        @pl.when(cj < 2 * n_qk)
        def _():
            conv_tile(True)

        @pl.when(cj >= 2 * n_qk)
        def _():
            conv_tile(False)


def _inproj(x2, nw, scale, shift, w16, conv_w, seq, conv_col0, qk_w, hd):
    t, d = x2.shape
    n = w16.shape[1]
    tm = min(seq, 1024)
    tn = _tile(n, 1024)
    per_b = seq // tm
    conv_ch = conv_w.shape[1]
    assert conv_col0 % tn == 0 and qk_w % tn == 0 and conv_ch % tn == 0 and tn % hd == 0
    c0, n_qk, n_conv = conv_col0 // tn, qk_w // tn, conv_ch // tn
    rc = min(tm, 64)
    sw = min(tn, 256)
    body = functools.partial(_inproj_body, tm=tm, tn=tn, sw=sw, rc=rc, hd=hd, per_b=per_b, c0=c0, n_qk=n_qk,
                             n_conv=n_conv, q_scale=hd ** -0.5)
    return pl.pallas_call(
        body,
        grid=(t // tm, n // tn),
        in_specs=[pl.BlockSpec((tm, d), lambda i, j: (i, 0)),
                  pl.BlockSpec((1, d), lambda i, j: (0, 0)),
                  pl.BlockSpec((None, 1, d), lambda i, j: (i // per_b, 0, 0)),
                  pl.BlockSpec((None, 1, d), lambda i, j: (i // per_b, 0, 0)),
                  pl.BlockSpec((d, tn), lambda i, j: (0, j)),
                  pl.BlockSpec((conv_w.shape[0], tn), lambda i, j: (0, jnp.clip(j - c0, 0, n_conv - 1)))],
        out_specs=pl.BlockSpec((tm, tn), lambda i, j: (i, j)),
        out_shape=jax.ShapeDtypeStruct((t, n), F32),
        scratch_shapes=[pltpu.VMEM((tm, d), BF16), pltpu.VMEM((n_conv, 8, tn), F32)]
        + [pltpu.VMEM((tm + 8, sw), F32) for _ in range(tn // sw)],
        compiler_params=_cp("arbitrary", "arbitrary"),
        name="inproj",
    )(x2, nw.reshape(1, d), scale[:, None, :], shift[:, None, :], w16, conv_w)


def _dsa_prep_body(lat_ref, qnw_ref, kvnw_ref, knw_ref, wuq_ref, wuqi_ref, wuk_ref,
                   qabs_ref, qi_ref, ckv_ref, ckvt_ref, ki_ref, wi_ref,
                   *, qr, kvr, di, nh, dh, nhi, q_scale, i_scale, w_scale, wi_off):
    lat = lat_ref[...]
    cq_t = _rms(lat[:, :qr], qnw_ref[...]).T.astype(BF16)
    q_t = _dot(wuq_ref[...], cq_t)
    for h in range(nh):
        q_h = q_t[h * dh:(h + 1) * dh].astype(BF16)
        qabs_ref[h] = (_dot(wuk_ref[h], q_h) * q_scale).astype(BF16)
    qi_t = _dot(wuqi_ref[...], cq_t) * i_scale
    for h in range(nhi):
        qi_ref[h] = qi_t[h * di:(h + 1) * di].astype(BF16)
    ckv = _rms(lat[:, qr:qr + kvr], kvnw_ref[...])
    ckv_ref[...] = ckv.astype(BF16)
    ckvt_ref[0:kvr, :] = ckv.T.astype(BF16)
    ckvt_ref[kvr:, :] = jnp.ones((ONES_ROWS, ckvt_ref.shape[1]), BF16)
    ki_ref[...] = _rms(lat[:, qr + kvr:qr + kvr + di], knw_ref[...]).astype(BF16)
    misc_t = lat[:, qr + kvr + di:].T
    wi_ref[...] = misc_t[wi_off:wi_off + nhi] * w_scale


def _dsa_prep(proj, bsz, seq, qnw, kvnw, knw, wuq_t, wuqi_t, wuk_r, dims):
    qr, kvr, di, nh, dh, nhi, wi_off, lat_blk = dims
    tm = min(seq, 512)
    ns = seq // tm
    lat_w = qr + kvr + di + LANES
    body = functools.partial(
        _dsa_prep_body, qr=qr, kvr=kvr, di=di, nh=nh, dh=dh, nhi=nhi,
        q_scale=dh ** -0.5 * LOG2E, i_scale=di ** -0.5, w_scale=nhi ** -0.5, wi_off=wi_off)
    full = lambda shape: pl.BlockSpec(shape, lambda b, s: (0,) * len(shape))
    return pl.pallas_call(
        body,
        grid=(bsz, ns),
        in_specs=[pl.BlockSpec((tm, lat_w), lambda b, s: (b * ns + s, lat_blk)),
                  full((1, qr)), full((1, kvr)), full((1, di)),
                  full((nh * dh, qr)), full((nhi * di, qr)), full((nh, kvr, dh))],
        out_specs=[pl.BlockSpec((None, nh, kvr, tm), lambda b, s: (b, 0, 0, s)),
                   pl.BlockSpec((None, nhi, di, tm), lambda b, s: (b, 0, 0, s)),
                   pl.BlockSpec((None, tm, kvr), lambda b, s: (b, s, 0)),
                   pl.BlockSpec((None, kvr + ONES_ROWS, tm), lambda b, s: (b, 0, s)),
                   pl.BlockSpec((None, tm, di), lambda b, s: (b, s, 0)),
                   pl.BlockSpec((None, nhi, tm), lambda b, s: (b, 0, s))],
        out_shape=[jax.ShapeDtypeStruct((bsz, nh, kvr, seq), BF16),
                   jax.ShapeDtypeStruct((bsz, nhi, di, seq), BF16),
                   jax.ShapeDtypeStruct((bsz, seq, kvr), BF16),
                   jax.ShapeDtypeStruct((bsz, kvr + ONES_ROWS, seq), BF16),
                   jax.ShapeDtypeStruct((bsz, seq, di), BF16),
                   jax.ShapeDtypeStruct((bsz, nhi, seq), F32)],
        compiler_params=_cp("parallel", "parallel"),
        name="dsa_prep",
    )(proj, qnw.reshape(1, qr), kvnw.reshape(1, kvr), knw.reshape(1, di), wuq_t, wuqi_t, wuk_r)


def _dsa_attn_body(ki_ref, ckv_ref, ckvt_ref, qi_ref, qabs_ref, wi_ref, wuv_ref, o_ref,
                   keys_ref, k16_ref, acc_ref, m_ref, ot_ref, thr_ref, jlim_ref,
                   *, tq, tk, at, nh, nhi, dh, kvr, topk, seq_bits):
    qb = pl.program_id(1)
    nk = ((qb + 1) * tq + tk - 1) // tk
    q_pos = qb * tq + lax.broadcasted_iota(I32, (tk, tq), 1)
    k_iota = lax.broadcasted_iota(I32, (tk, tq), 0)

    def score_tile(kt, _):
        ki_t = ki_ref[pl.ds(pl.multiple_of(kt * tk, tk), tk), :]
        acc = jnp.zeros((tk, tq), F32)
        for h in range(nhi):
            rel = _dot(ki_t, qi_ref[h])
            acc = acc + wi_ref[h:h + 1, :] * jnp.maximum(rel, 0.0)
        acc = jnp.where(acc == 0.0, 0.0, acc)
        bits = lax.bitcast_convert_type(acc, I32)
        causal = kt * tk + k_iota <= q_pos
        key = jnp.where(bits < 0, bits ^ INT_MAX, bits)
        keys_ref[pl.ds(pl.multiple_of(kt * tk, tk), tk), :] = jnp.where(causal, key, INT_MIN)
        top = lax.bitcast_convert_type(bits & -65536, F32)
        k16_ref[pl.ds(pl.multiple_of(kt * tk, tk), tk), :] = jnp.where(causal, top, jnp.nan).astype(BF16)
        return 0

    lax.fori_loop(0, nk, score_tile, 0)

    one16 = jnp.ones((tk, tq), BF16)
    zero16 = jnp.zeros((tk, tq), BF16)

    def count16(cand_bf):
        cand_b = jnp.broadcast_to(cand_bf, (tk, tq))

        def tile(kt, c):
            k_t = k16_ref[pl.ds(pl.multiple_of(kt * tk, tk), tk), :]
            hit = jnp.where(k_t >= cand_b, one16, zero16)
            part = hit[0:16]
            for r in range(1, tk // 16):
                part = part + hit[r * 16:(r + 1) * 16]
            return c + part.astype(F32)
        c16 = lax.fori_loop(0, nk, tile, jnp.zeros((16, tq), F32))
        return jnp.sum(c16, axis=0, keepdims=True)

    def hi_step(it, hi):
        cand = hi + lax.shift_left(jnp.int32(1), 15 - it)
        pattern = jnp.where(cand >= 0, cand, cand ^ 0x7FFF) & 0xFFFF
        cand_bf = lax.bitcast_convert_type(lax.shift_left(pattern, 16), F32).astype(BF16)
        return jnp.where(count16(cand_bf) >= topk, cand, hi)

    hi16 = lax.fori_loop(0, 16, hi_step, jnp.full((1, tq), -(2 ** 15), I32))

    def count(pred_fn):
        def tile(kt, c):
            k_t = keys_ref[pl.ds(pl.multiple_of(kt * tk, tk), tk), :]
            hit = jnp.where(pred_fn(k_t, kt), 1, 0).astype(I32)
            return c + jnp.sum(hit.reshape(tk // 8, 8, tq), axis=0)
        c8 = lax.fori_loop(0, nk, tile, jnp.zeros((8, tq), I32))
        return jnp.sum(c8, axis=0, keepdims=True)

    def bit_cond(carry):
        it, _, settled = carry
        return (it < 32) & (jnp.min(settled) == 0)

    def bit_step(carry):
        it, thr, settled = carry
        for _ in range(4):
            cand = thr ^ lax.shift_left(jnp.int32(1), 31 - it)
            cnt = count(lambda k_t, kt: k_t >= cand)
            take = (cnt >= topk) & (settled == 0)
            settled = jnp.where(take & (cnt == topk), 1, settled)
            thr = jnp.where(take, cand, thr)
            it = it + 1
        return it, thr, settled

    _, thr, _ = lax.while_loop(
        bit_cond, bit_step, (jnp.int32(16), lax.shift_left(hi16, 16), jnp.zeros((1, tq), I32)))
    thr = jnp.maximum(thr, INT_MIN + 1)
    cnt_gt = count(lambda k_t, kt: k_t > thr)
    cnt_ge = count(lambda k_t, kt: k_t >= thr)
    thr_ref[...] = thr
    jlim_ref[...] = jnp.full((1, tq), INT_MAX, I32)

    @pl.when(jnp.max(cnt_ge) > topk)
    def _():
        need = topk - cnt_gt

        def pos_step(it, p):
            cand = p + lax.shift_left(jnp.int32(1), seq_bits - 1 - it)
            cnt = count(lambda k_t, kt: (k_t == thr) & (kt * tk + k_iota < cand))
            return jnp.where(cnt < need, cand, p)

        jlim_ref[...] = lax.fori_loop(0, seq_bits, pos_step, jnp.zeros((1, tq), I32))

    m_ref[...] = jnp.full(m_ref.shape, -jnp.inf, F32)
    acc_ref[...] = jnp.zeros(acc_ref.shape, F32)

    na = (nk + at - 1) // at

    def blank(kt, _):
        keys_ref[pl.ds(pl.multiple_of(kt * tk, tk), tk), :] = jnp.full((tk, tq), INT_MIN, I32)
        return 0

    lax.fori_loop(nk, na * at, blank, 0)
    ta = at * tk
    a_iota = lax.broadcasted_iota(I32, (ta, tq), 0)

    def attn_tile(kt, _):
        off = pl.multiple_of(kt * ta, ta)
        k_t = keys_ref[pl.ds(off, ta), :]
        thr_b = thr_ref[...]
        mask = (k_t > thr_b) | ((k_t == thr_b) & (kt * ta + a_iota <= jlim_ref[...]))
        bias = jnp.where(mask, 0.0, -jnp.inf)
        ckv_t = ckv_ref[pl.ds(off, ta), :]
        ckvt_t = ckvt_ref[:, pl.ds(off, ta)]
        logits = _dot(ckv_t, qabs_ref[0])
        for h in range(nh):
            s = logits + bias
            if h + 1 < nh:
                logits = _dot(ckv_t, qabs_ref[h + 1])
            m_old = m_ref[h]
            m_new = jnp.maximum(m_old, jnp.max(s, axis=0, keepdims=True))
            m_use = jnp.where(m_new == -jnp.inf, 0.0, m_new)
            p = jnp.exp2(s - m_use)
            alpha = jnp.exp2(m_old - m_use)
            acc_ref[h] = alpha * acc_ref[h] + _dot(ckvt_t, p.astype(BF16))
            m_ref[h] = m_new
        return 0

    lax.fori_loop(0, na, attn_tile, 0)

    for h in range(nh):
        acc = acc_ref[h]
        o_lat = (acc[:kvr] / acc[kvr:kvr + 1]).astype(BF16)
        ot_ref[h * dh:(h + 1) * dh, :] = _dot(wuv_ref[h], o_lat)
    o_ref[...] = ot_ref[...].T.astype(BF16)


def _dsa_attn(ki, ckv, ckvt, qi_t, qabs_t, wi_t, wuv_t):
    bsz, seq, di = ki.shape
    kvr = ckv.shape[2]
    nh, dh = wuv_t.shape[0], wuv_t.shape[1]
    nhi = qi_t.shape[1]
    tq = min(seq // 2, 256)
    tk = tq
    at = min(ATTN_TILES, seq // tk)
    assert seq % (at * tk) == 0
    topk = min(IDX_TOPK, seq // 4)
    body = functools.partial(_dsa_attn_body, tq=tq, tk=tk, at=at, nh=nh, nhi=nhi, dh=dh, kvr=kvr, topk=topk,
                             seq_bits=max(1, (seq - 1).bit_length()))
    return pl.pallas_call(
        body,
        grid=(bsz, seq // tq),
        in_specs=[pl.BlockSpec((None, seq, di), lambda b, q: (b, 0, 0)),
                  pl.BlockSpec((None, seq, kvr), lambda b, q: (b, 0, 0)),
                  pl.BlockSpec((None, kvr + ONES_ROWS, seq), lambda b, q: (b, 0, 0)),
                  pl.BlockSpec((None, nhi, di, tq), lambda b, q: (b, 0, 0, q)),
                  pl.BlockSpec((None, nh, kvr, tq), lambda b, q: (b, 0, 0, q)),
                  pl.BlockSpec((None, nhi, tq), lambda b, q: (b, 0, q)),
                  pl.BlockSpec((nh, dh, kvr), lambda b, q: (0, 0, 0))],
        out_specs=pl.BlockSpec((None, tq, nh * dh), lambda b, q: (b, q, 0)),
        out_shape=jax.ShapeDtypeStruct((bsz, seq, nh * dh), BF16),
        scratch_shapes=[pltpu.VMEM((seq, tq), I32),
                        pltpu.VMEM((seq, tq), BF16),
                        pltpu.VMEM((nh, kvr + ONES_ROWS, tq), F32),
                        pltpu.VMEM((nh, 1, tq), F32),
                        pltpu.VMEM((nh * dh, tq), F32),
                        pltpu.VMEM((1, tq), I32),
                        pltpu.VMEM((1, tq), I32)],
        compiler_params=_cp("parallel", "parallel"),
        name="dsa_attn",
    )(ki, ckv, ckvt, qi_t, qabs_t, wi_t, wuv_t)


GDN_GROUP = 16
GDN_INV_BATCH = 32


def _gdn_gate_body(m_ref, alog_ref, dtb_ref, gc_ref, beta_ref, gct_ref, *, ts, chunk, nhv, ng):
    x = m_ref[...]
    z = x + dtb_ref[...]
    softplus = jnp.maximum(z, 0.0) + jnp.log(1.0 + jnp.exp(-jnp.abs(z)))
    g = -jnp.exp(alog_ref[...]) * softplus
    pos = lax.broadcasted_iota(I32, (ts, LANES), 0) % chunk
    d = 1
    while d < chunk:
        g = g + jnp.where(pos >= d, pltpu.roll(g, d, axis=0), 0.0)
        d *= 2
    gct_ref[...] = g.T
    beta = jax.nn.sigmoid(x)
    for j in range(nhv // ng):
        gc_ref[:, j * LANES:(j + 1) * LANES] = pltpu.roll(g, (LANES - j * ng) % LANES, axis=1)
        beta_ref[:, j * LANES:(j + 1) * LANES] = pltpu.roll(beta, (2 * LANES - nhv - j * ng) % LANES, axis=1)


def _gdn_gates(proj, alog_p, dtb_p, bsz, seq, misc_blk, nhv):
    ts = min(seq, 512)
    ns = seq // ts
    ng = GDN_GROUP
    gw = nhv // ng * LANES
    body = functools.partial(_gdn_gate_body, ts=ts, chunk=GDN_CHUNK, nhv=nhv, ng=ng)
    return pl.pallas_call(
        body,
        grid=(bsz, ns),
        in_specs=[pl.BlockSpec((ts, LANES), lambda b, s: (b * ns + s, misc_blk)),
                  pl.BlockSpec((1, LANES), lambda b, s: (0, 0)),
                  pl.BlockSpec((1, LANES), lambda b, s: (0, 0))],
        out_specs=[pl.BlockSpec((ts, gw), lambda b, s: (b * ns + s, 0)),
                   pl.BlockSpec((ts, gw), lambda b, s: (b * ns + s, 0)),
                   pl.BlockSpec((None, LANES, ts), lambda b, s: (b, 0, s))],
        out_shape=[jax.ShapeDtypeStruct((bsz * seq, gw), F32),
                   jax.ShapeDtypeStruct((bsz * seq, gw), F32),
                   jax.ShapeDtypeStruct((bsz, LANES, seq), F32)],
        compiler_params=_cp("parallel", "parallel"),
        name="gdn_gates",
    )(proj, alog_p, dtb_p)


def _gdn_core_body(q_ref, k_ref, v_ref, gc_ref, beta_ref, gr_ref, z_ref, nw_ref, o_ref, *scratch,
                   ts, chunk, hd, ng, rep, gr_steps):
    st_refs = scratch[:ng]
    gcb_ref, bb_ref, a_ref, u_ref, wq_ref, qkkt_ref, rhs_ref, egl_ref = scratch[ng:]
    nc = ts // chunk
    gr_base = (pl.program_id(2) % gr_steps) * nc

    @pl.when(pl.program_id(2) == 0)
    def _():
        for st_ref in st_refs:
            st_ref[...] = jnp.zeros(st_ref.shape, F32)

    for g in range(ng):
        gcb_ref[g] = jnp.broadcast_to(gc_ref[:, g:g + 1], (ts, LANES))
        bb_ref[g] = jnp.broadcast_to(beta_ref[:, g:g + 1], (ts, LANES))

    ri = lax.broadcasted_iota(I32, (chunk, chunk), 0)
    ci = lax.broadcasted_iota(I32, (chunk, chunk), 1)
    incl = ri >= ci
    strict = ri > ci
    eye = jnp.where(ri == ci, 1.0, 0.0).astype(F32)

    def prep(c, _):
        rows = pl.ds(pl.multiple_of(c * chunk, chunk), chunk)
        for qh in range(ng // rep):
            q = q_ref[rows, qh * hd:(qh + 1) * hd]
            k = k_ref[rows, qh * hd:(qh + 1) * hd]
            k16 = k.astype(BF16)
            kk = _dot_nt(k16, k16)
            qk = _dot_nt(q.astype(BF16), k16)
            a_pair = []
            for vh in range(rep):
                g = qh * rep + vh
                n = c * ng + g
                gcb = gcb_ref[g, rows, :]
                bb = bb_ref[g, rows, :]
                g_row = gr_ref[g, pl.ds(gr_base + c, 1), :]
                decay = jnp.where(incl, jnp.exp(gcb[:, :chunk] - g_row), 0.0)
                a_pair.append(jnp.where(strict, kk * bb[:, :chunk] * decay, 0.0))
                qkkt_ref[n, 0:chunk, :] = (qk * decay).astype(BF16)
                eg = jnp.exp(gcb)
                rhs_ref[n, :, 0:hd] = (v_ref[rows, g * hd:(g + 1) * hd] * bb).astype(BF16)
                rhs_ref[n, :, hd:2 * hd] = (k * bb * eg).astype(BF16)
                wq_ref[n, chunk:2 * chunk, :] = (q * eg).astype(BF16)
                g_last = gcb[chunk - 1:chunk, :]
                qkkt_ref[n, chunk:chunk + hd, :] = (k * jnp.exp(g_last - gcb)).T.astype(BF16)
                egl_ref[n] = jnp.broadcast_to(jnp.exp(g_last), (8, LANES))
            a_ref[c * (ng // rep) + qh] = jnp.concatenate(a_pair, axis=1)
        return 0

    lax.fori_loop(0, nc, prep, 0)

    n_dbl = max(1, (chunk - 1).bit_length()) - 1

    lane2 = lax.broadcasted_iota(I32, (chunk, 2 * chunk), 1)
    left = lane2 < chunk
    eye2 = jnp.where(lax.broadcasted_iota(I32, (chunk, 2 * chunk), 0) == lane2 % chunk, 1.0, 0.0).astype(F32)

    def blockdiag(x):
        return jnp.concatenate([jnp.where(left, x, 0.0), jnp.where(left, 0.0, x)], axis=0).astype(BF16)

    def invert(ib, _):
        ms_ = [ib * (GDN_INV_BATCH // 2) + j for j in range(GDN_INV_BATCH // 2)]
        pw = [a_ref[m] for m in ms_]
        tm = [eye2 - a for a in pw]
        pw = [_dot(a.astype(BF16), blockdiag(a)) for a in pw]
        for lvl in range(n_dbl):
            last = lvl == n_dbl - 1
            lhs = [(t if last else jnp.concatenate([t, x], axis=0)).astype(BF16) for t, x in zip(tm, pw)]
            prod = [_dot(y, blockdiag(x)) for y, x in zip(lhs, pw)]
            tm = [t + y[:chunk] for t, y in zip(tm, prod)]
            pw = [y[chunk:] for y in prod]
        for m, t in zip(ms_, tm):
            zero = jnp.zeros((chunk, 2 * hd), BF16)
            rhs2 = jnp.concatenate([jnp.concatenate([rhs_ref[2 * m], zero], axis=1),
                                    jnp.concatenate([zero, rhs_ref[2 * m + 1]], axis=1)], axis=0)
            uw = _dot(t.astype(BF16), rhs2)
            for j in range(2):
                u_ref[2 * m + j] = uw[:, 2 * j * hd:(2 * j + 1) * hd]
                wq_ref[2 * m + j, 0:chunk, :] = uw[:, (2 * j + 1) * hd:(2 * j + 2) * hd].astype(BF16)
        return 0

    lax.fori_loop(0, nc * ng // GDN_INV_BATCH, invert, 0)

    nw = nw_ref[...]

    def step(c, _):
        rows = pl.ds(pl.multiple_of(c * chunk, chunk), chunk)
        heads = range(ng)
        ns_ = [c * ng + g for g in heads]
        state = [st_refs[g][...] for g in heads]
        st16 = [x.astype(BF16) for x in state]
        wq = [_dot(wq_ref[n], x) for n, x in zip(ns_, st16)]
        vn16 = [(u_ref[n] - x[:chunk]).astype(BF16) for n, x in zip(ns_, wq)]
        ok = [_dot(qkkt_ref[n], x) for n, x in zip(ns_, vn16)]
        o = [x[chunk:] + y[:chunk] for x, y in zip(wq, ok)]
        ds = [y[chunk:] for y in ok]
        for g in heads:
            st_refs[g][...] = state[g] * egl_ref[ns_[g]][0:1, :] + ds[g]
            zz = z_ref[rows, g * hd:(g + 1) * hd]
            o_ref[rows, g * hd:(g + 1) * hd] = (_rms(o[g], nw) * (zz * _sigmoid(zz))).astype(o_ref.dtype)
        return 0

    lax.fori_loop(0, nc, step, 0)


def _gdn_core(proj, gc, beta, gct4, norm_w, bsz, seq, nqk, nhv, hd, q_col, z_col):
    assert hd == LANES
    chunk = GDN_CHUNK
    ng = GDN_GROUP
    rep = nhv // nqk
    ts = min(seq, 256)
    ns = seq // ts
    n = (ts // chunk) * ng
    assert rep == 2 and ng % rep == 0 and nhv % ng == 0 and n % GDN_INV_BATCH == 0
    qw, vw = (ng // rep) * hd, ng * hd
    k_col, v_col = q_col + nqk * hd, q_col + 2 * nqk * hd
    assert q_col % qw == 0 and k_col % qw == 0 and v_col % vw == 0 and z_col % vw == 0
    gr_steps = max(1, 8 * chunk // ts)
    assert ns % gr_steps == 0
    body = functools.partial(_gdn_core_body, ts=ts, chunk=chunk, hd=hd, ng=ng, rep=rep, gr_steps=gr_steps)
    row = lambda b, h, s: b * ns + s
    return pl.pallas_call(
        body,
        grid=(bsz, nhv // ng, ns),
        in_specs=[pl.BlockSpec((ts, qw), lambda b, h, s: (row(b, h, s), q_col // qw + h)),
                  pl.BlockSpec((ts, qw), lambda b, h, s: (row(b, h, s), k_col // qw + h)),
                  pl.BlockSpec((ts, vw), lambda b, h, s: (row(b, h, s), v_col // vw + h)),
                  pl.BlockSpec((ts, LANES), lambda b, h, s: (row(b, h, s), h)),
                  pl.BlockSpec((ts, LANES), lambda b, h, s: (row(b, h, s), h)),
                  pl.BlockSpec((None, ng, gr_steps * ts // chunk, chunk), lambda b, h, s: (b, h, s // gr_steps, 0)),
                  pl.BlockSpec((ts, vw), lambda b, h, s: (row(b, h, s), z_col // vw + h)),
                  pl.BlockSpec((1, hd), lambda b, h, s: (0, 0))],
        out_specs=pl.BlockSpec((ts, vw), lambda b, h, s: (row(b, h, s), h)),
        out_shape=jax.ShapeDtypeStruct((bsz * seq, nhv * hd), BF16),
        scratch_shapes=[pltpu.VMEM((hd, hd), F32) for _ in range(ng)] + [
                        pltpu.VMEM((ng, ts, LANES), F32),
                        pltpu.VMEM((ng, ts, LANES), F32),
                        pltpu.VMEM((n // 2, chunk, 2 * chunk), F32),
                        pltpu.VMEM((n, chunk, hd), F32),
                        pltpu.VMEM((n, 2 * chunk, hd), BF16),
                        pltpu.VMEM((n, chunk + hd, chunk), BF16),
                        pltpu.VMEM((n, chunk, 2 * hd), BF16),
                        pltpu.VMEM((n, 8, LANES), F32)],
        compiler_params=_cp("parallel", "parallel", "arbitrary"),
        name="gdn_core",
    )(proj, proj, proj, gc, beta, gct4, proj, norm_w.reshape(1, hd))


def _merge_body(a_ref, b_ref, wa_ref, wb_ref, ga_ref, gb_ref, o_ref):
    ya = _dot(a_ref[...], wa_ref[...])
    yb = _dot(b_ref[...], wb_ref[...])
    o_ref[...] = (_sigmoid(ga_ref[...]) * ya + _sigmoid(gb_ref[...]) * yb).astype(o_ref.dtype)


def _merge(o_a, o_b, wa16, wb16, proj, ga_col0, gb_col0):
    t, ka = o_a.shape
    kb = o_b.shape[1]
    d = wa16.shape[1]
    tm = min(t, 1024)
    tn = _tile(d, 512)
    ga0, gb0 = ga_col0 // tn, gb_col0 // tn
    return pl.pallas_call(
        _merge_body,
        grid=(t // tm, d // tn),
        in_specs=[pl.BlockSpec((tm, ka), lambda i, j: (i, 0)),
                  pl.BlockSpec((tm, kb), lambda i, j: (i, 0)),
                  pl.BlockSpec((ka, tn), lambda i, j: (0, j)),
                  pl.BlockSpec((kb, tn), lambda i, j: (0, j)),
                  pl.BlockSpec((tm, tn), lambda i, j: (i, ga0 + j)),
                  pl.BlockSpec((tm, tn), lambda i, j: (i, gb0 + j))],
        out_specs=pl.BlockSpec((tm, tn), lambda i, j: (i, j)),
        out_shape=jax.ShapeDtypeStruct((t, d), BF16),
        compiler_params=_cp("parallel", "parallel"),
        name="merge",
    )(o_a, o_b, wa16, wb16, proj, proj)


def _outproj_router_body(m_ref, x_ref, wo_ref, g1_ref, nw_ref, sc_ref, sh_ref, wr_ref, br_ref,
                         x1_ref, h2_ref, te_ref, tg_ref, *, topk):
    x1 = x_ref[...] + g1_ref[...] * _dot(m_ref[...], wo_ref[...])
    x1_ref[...] = x1
    h2 = _rms(x1, nw_ref[...]) * (1.0 + sc_ref[...]) + sh_ref[...]
    h2_ref[...] = h2
    logits = _dot(h2.astype(BF16), wr_ref[...]) + br_ref[...]
    lane = lax.broadcasted_iota(I32, logits.shape, 1)
    te = jnp.zeros(logits.shape, I32)
    tg = jnp.zeros(logits.shape, F32)
    denom = jnp.zeros((logits.shape[0], 1), F32)
    m0 = None
    for kk in range(topk):
        mx = jnp.max(logits, axis=-1, keepdims=True)
        idx = jnp.min(jnp.where(logits == mx, lane, LANES), axis=-1, keepdims=True)
        if kk == 0:
            m0 = mx
        e = jnp.exp(mx - m0)
        denom = denom + e
        te = jnp.where(lane == kk, idx, te)
        tg = jnp.where(lane == kk, e, tg)
        logits = jnp.where(lane == idx, -jnp.inf, logits)
    te_ref[...] = te
    tg_ref[...] = tg / denom


def _outproj_router(merged, x2, wo16, gate1, nw, scale, shift, wr16, br_p, seq):
    t, d = x2.shape
    tm = min(seq, 512)
    per_b = seq // tm
    vec = lambda: pl.BlockSpec((None, 1, d), lambda i: (i // per_b, 0, 0))
    return pl.pallas_call(
        functools.partial(_outproj_router_body, topk=TOP_K),
        grid=(t // tm,),
        in_specs=[pl.BlockSpec((tm, d), lambda i: (i, 0)),
                  pl.BlockSpec((tm, d), lambda i: (i, 0)),
                  pl.BlockSpec((d, d), lambda i: (0, 0)),
                  vec(),
                  pl.BlockSpec((1, d), lambda i: (0, 0)),
                  vec(), vec(),
                  pl.BlockSpec((d, LANES), lambda i: (0, 0)),
                  pl.BlockSpec((1, LANES), lambda i: (0, 0))],
        out_specs=[pl.BlockSpec((tm, d), lambda i: (i, 0)),
                   pl.BlockSpec((tm, d), lambda i: (i, 0)),
                   pl.BlockSpec((tm, LANES), lambda i: (i, 0)),
                   pl.BlockSpec((tm, LANES), lambda i: (i, 0))],
        out_shape=[jax.ShapeDtypeStruct((t, d), F32),
                   jax.ShapeDtypeStruct((t, d), F32),
                   jax.ShapeDtypeStruct((t, LANES), I32),
                   jax.ShapeDtypeStruct((t, LANES), F32)],
        compiler_params=_cp("parallel"),
        name="outproj_router",
    )(merged, x2, wo16, gate1[:, None, :], nw.reshape(1, d), scale[:, None, :], shift[:, None, :], wr16, br_p)


def _moe_rank_body(te_ref, pos_ref, cnt_ref, carry_ref, *, tt, topk):
    @pl.when(pl.program_id(0) == 0)
    def _():
        carry_ref[...] = jnp.zeros(carry_ref.shape, F32)

    te = te_ref[...]
    lane = lax.broadcasted_iota(I32, (tt, LANES), 1)
    onehot = jnp.zeros((tt, LANES), F32)
    for kk in range(topk):
        onehot = onehot + jnp.where(lane == te[:, kk:kk + 1], 1.0, 0.0)
    r = lax.broadcasted_iota(I32, (tt, tt), 0)
    c = lax.broadcasted_iota(I32, (tt, tt), 1)
    below = jnp.where(r > c, 1.0, 0.0).astype(BF16)
    rank = _dot(below, onehot.astype(BF16)) + carry_ref[0:1, :]
    pos = jnp.zeros((tt, LANES), I32)
    for kk in range(topk):
        p = jnp.sum(jnp.where(lane == te[:, kk:kk + 1], rank, 0.0), axis=-1, keepdims=True)
        pos = jnp.where(lane == kk, p.astype(I32), pos)
    pos_ref[...] = pos
    carry_ref[...] = carry_ref[...] + jnp.sum(onehot, axis=0, keepdims=True)
    cnt_ref[...] = carry_ref[...]


def _moe_rank(te):
    t = te.shape[0]
    tt = min(t, 512)
    return pl.pallas_call(
        functools.partial(_moe_rank_body, tt=tt, topk=TOP_K),
        grid=(t // tt,),
        in_specs=[pl.BlockSpec((tt, LANES), lambda i: (i, 0))],
        out_specs=[pl.BlockSpec((tt, LANES), lambda i: (i, 0)),
                   pl.BlockSpec((8, LANES), lambda i: (0, 0))],
        out_shape=[jax.ShapeDtypeStruct((t, LANES), I32),
                   jax.ShapeDtypeStruct((8, LANES), F32)],
        scratch_shapes=[pltpu.VMEM((8, LANES), F32)],
        compiler_params=_cp("arbitrary"),
        name="moe_rank",
    )(te)


def _moe_dispatch_body(dest_ref, fill_ref, h_ref, x_hbm, zero_ref, sem, fill_sem, *, tt, topk, bm, n_blocks):
    @pl.when(pl.program_id(0) == 0)
    def _():
        zero_ref[...] = jnp.zeros(zero_ref.shape, zero_ref.dtype)

        def fill_copy(blk):
            return pltpu.make_async_copy(zero_ref, x_hbm.at[pl.ds(pl.multiple_of(blk * bm, bm), bm)], fill_sem)

        def start(blk, _):
            @pl.when(fill_ref[blk] != 0)
            def _():
                fill_copy(blk).start()
            return 0

        def drain(blk, _):
            @pl.when(fill_ref[blk] != 0)
            def _():
                fill_copy(blk).wait()
            return 0

        lax.fori_loop(0, n_blocks, start, 0)
        lax.fori_loop(0, n_blocks, drain, 0)

    base = pl.program_id(0) * tt * topk

    def issue(t, _):
        for kk in range(topk):
            dst = x_hbm.at[pl.ds(dest_ref[base + t * topk + kk], 1)]
            pltpu.make_async_copy(h_ref.at[pl.ds(t, 1)], dst, sem).start()
        return 0

    lax.fori_loop(0, tt, issue, 0, unroll=4)
    for _ in range(topk):
        pltpu.make_async_copy(h_ref, x_hbm.at[pl.ds(0, tt)], sem).wait()


def _moe_dispatch(dest_flat, fill_flags, h2, n_rows):
    t, d = h2.shape
    tt = min(t, 128)
    bm = MOE_BM
    grid_spec = pltpu.PrefetchScalarGridSpec(
        num_scalar_prefetch=2,
        grid=(t // tt,),
        in_specs=[pl.BlockSpec((tt, d), lambda i, dest, fill: (i, 0))],
        out_specs=pl.BlockSpec(memory_space=pl.ANY),
        scratch_shapes=[pltpu.VMEM((bm, d), F32), pltpu.SemaphoreType.DMA(()), pltpu.SemaphoreType.DMA(())])
    return pl.pallas_call(
        functools.partial(_moe_dispatch_body, tt=tt, topk=TOP_K, bm=bm, n_blocks=n_rows // bm),
        grid_spec=grid_spec,
        out_shape=jax.ShapeDtypeStruct((n_rows, d), F32),
        compiler_params=_cp("arbitrary"),
        name="moe_dispatch",
    )(dest_flat, fill_flags, h2)


def _moe_gu_body(be_ref, na_ref, x_ref, wg_ref, wu_ref, bg_ref, bu_ref, act_ref, wg16_ref, wu16_ref):
    i = pl.program_id(1)
    prev = be_ref[jnp.maximum(i - 1, 0)]

    @pl.when((i == 0) | (be_ref[i] != prev))
    def _():
        wg16_ref[...] = wg_ref[...].astype(BF16)
        wu16_ref[...] = wu_ref[...].astype(BF16)

    @pl.when(i < na_ref[0])
    def _():
        x = x_ref[...].astype(BF16)
        g = jnp.minimum(_dot(x, wg16_ref[...]) + bg_ref[...], SWIGLU_LIMIT)
        u = jnp.clip(_dot(x, wu16_ref[...]) + bu_ref[...], -SWIGLU_LIMIT, SWIGLU_LIMIT)
        act_ref[...] = ((u + 1.0) * (g * _sigmoid(SWIGLU_ALPHA * g))).astype(act_ref.dtype)

    @pl.when(i >= na_ref[0])
    def _():
        act_ref[...] = jnp.zeros(act_ref.shape, act_ref.dtype)


def _moe_gu(block_e, n_act, x_rows, w_gu, b_gu):
    n_rows, d = x_rows.shape
    ne, _, ff2 = w_gu.shape
    ff = ff2 // 2
    bm = MOE_BM
    tn = _tile(ff, 1024)
    nt = ff // tn
    rowblk = lambda j, i, be, na: (jnp.minimum(i, na[0] - 1), 0)
    grid_spec = pltpu.PrefetchScalarGridSpec(
        num_scalar_prefetch=2,
        grid=(nt, n_rows // bm),
        in_specs=[pl.BlockSpec((bm, d), rowblk),
                  pl.BlockSpec((None, d, tn), lambda j, i, be, na: (be[i], 0, j)),
                  pl.BlockSpec((None, d, tn), lambda j, i, be, na: (be[i], 0, nt + j)),
                  pl.BlockSpec((None, 1, tn), lambda j, i, be, na: (be[i], 0, j)),
                  pl.BlockSpec((None, 1, tn), lambda j, i, be, na: (be[i], 0, nt + j))],
        out_specs=pl.BlockSpec((bm, tn), lambda j, i, be, na: (i, j)),
        scratch_shapes=[pltpu.VMEM((d, tn), BF16), pltpu.VMEM((d, tn), BF16)])
    return pl.pallas_call(
        _moe_gu_body,
        grid_spec=grid_spec,
        out_shape=jax.ShapeDtypeStruct((n_rows, ff), BF16),
        compiler_params=_cp("arbitrary", "arbitrary"),
        name="moe_gu",
    )(block_e, n_act, x_rows, w_gu, w_gu, b_gu.reshape(ne, 1, ff2), b_gu.reshape(ne, 1, ff2))


def _moe_down_body(be_ref, na_ref, a_ref, w_ref, b_ref, y_ref, w16_ref):
    i = pl.program_id(1)
    prev = be_ref[jnp.maximum(i - 1, 0)]

    @pl.when((i == 0) | (be_ref[i] != prev))
    def _():
        w16_ref[...] = w_ref[...].astype(BF16)

    @pl.when(i < na_ref[0])
    def _():
        y_ref[...] = _dot(a_ref[...], w16_ref[...]) + b_ref[...]

    @pl.when(i >= na_ref[0])
    def _():
        y_ref[...] = jnp.zeros(y_ref.shape, y_ref.dtype)


def _moe_down(block_e, n_act, act, w_down, b_down):
    n_rows, ff = act.shape
    ne, _, d = w_down.shape
    bm = MOE_BM
    tn = _tile(d, 2048)
    grid_spec = pltpu.PrefetchScalarGridSpec(
        num_scalar_prefetch=2,
        grid=(d // tn, n_rows // bm),
        in_specs=[pl.BlockSpec((bm, ff), lambda j, i, be, na: (jnp.minimum(i, na[0] - 1), 0)),
                  pl.BlockSpec((None, ff, tn), lambda j, i, be, na: (be[i], 0, j)),
                  pl.BlockSpec((None, 1, tn), lambda j, i, be, na: (be[i], 0, j))],
        out_specs=pl.BlockSpec((bm, tn), lambda j, i, be, na: (i, j)),
        scratch_shapes=[pltpu.VMEM((ff, tn), BF16)])
    return pl.pallas_call(
        _moe_down_body,
        grid_spec=grid_spec,
        out_shape=jax.ShapeDtypeStruct((n_rows, d), F32),
        compiler_params=_cp("arbitrary", "arbitrary"),
        name="moe_down",
    )(block_e, n_act, act, w_down, b_down.reshape(ne, 1, d))


def _moe_combine_body(dest_ref, x1_ref, tg_ref, g2_ref, fw_ref, y_hbm, o_ref, buf_ref, sem, *, tt, topk):
    i = pl.program_id(0)
    slot = i % 2

    def gather(step, sl):
        base = step * tt * topk

        def issue(t, _):
            for kk in range(topk):
                src = y_hbm.at[pl.ds(dest_ref[base + t * topk + kk], 1)]
                pltpu.make_async_copy(src, buf_ref.at[sl, kk, pl.ds(t, 1)], sem.at[sl]).start()
            return 0

        lax.fori_loop(0, tt, issue, 0, unroll=4)

    @pl.when(i == 0)
    def _():
        gather(0, 0)

    @pl.when(i + 1 < pl.num_programs(0))
    def _():
        gather(i + 1, 1 - slot)

    for kk in range(topk):
        pltpu.make_async_copy(y_hbm.at[pl.ds(0, tt)], buf_ref.at[slot, kk], sem.at[slot]).wait()
    tg = tg_ref[...]
    y = tg[:, 0:1] * buf_ref[slot, 0]
    for kk in range(1, topk):
        y = y + tg[:, kk:kk + 1] * buf_ref[slot, kk]
    o_ref[...] = _rms(x1_ref[...] + g2_ref[...] * y, fw_ref[...])


def _moe_combine(dest_flat, x1, tg, gate2, final_w, y_rows, seq):
    t, d = x1.shape
    tt = min(seq, 128)
    per_b = seq // tt
    grid_spec = pltpu.PrefetchScalarGridSpec(
        num_scalar_prefetch=1,
        grid=(t // tt,),
        in_specs=[pl.BlockSpec((tt, d), lambda i, dest: (i, 0)),
                  pl.BlockSpec((tt, LANES), lambda i, dest: (i, 0)),
                  pl.BlockSpec((None, 1, d), lambda i, dest: (i // per_b, 0, 0)),
                  pl.BlockSpec((1, d), lambda i, dest: (0, 0)),
                  pl.BlockSpec(memory_space=pl.ANY)],
        out_specs=pl.BlockSpec((tt, d), lambda i, dest: (i, 0)),
        scratch_shapes=[pltpu.VMEM((2, TOP_K, tt, d), F32), pltpu.SemaphoreType.DMA((2,))])
    return pl.pallas_call(
        functools.partial(_moe_combine_body, tt=tt, topk=TOP_K),
        grid_spec=grid_spec,
        out_shape=jax.ShapeDtypeStruct((t, d), F32),
        compiler_params=_cp("arbitrary"),
        name="moe_combine",
    )(dest_flat, x1, tg, gate2[:, None, :], final_w.reshape(1, d), y_rows)


def _pad_lanes(v, off=0, fill=0.0):
    out = jnp.full((1, LANES), fill, F32)
    return out.at[0, off:off + v.shape[0]].set(v.astype(F32))


def _layer(x2, c, bsz, seq, p, final_w):
    d = x2.shape[1]
    qr, nh, dh = p["w_uq"].shape
    kvr = p["w_uk"].shape[0]
    nhi, di = p["w_uqi"].shape[1:]
    hd = p["gdn_norm_w"].shape[0]
    nhv = p["a_log"].shape[0]
    v_w = nhv * hd
    qk_w = (p["conv_w"].shape[1] - v_w) // 2
    nqk = qk_w // hd
    ne = p["w_router"].shape[1]

    mod = _ada(c, p["ada_w"], p["ada_b"])
    shift1, scale1, gate1, shift2, scale2, gate2 = jnp.split(mod, 6, axis=-1)

    widths = (qr, kvr, di, nhi, qk_w, qk_w, v_w, nhv, nhv, v_w, d, d)
    offs = [0]
    for wd in widths:
        offs.append(offs[-1] + wd)
    col = lambda k: p["w_in"][:, offs[k]:offs[k + 1]]
    pad = jnp.zeros((d, LANES - 2 * nhv - nhi), F32)
    w_in = jnp.concatenate([col(4), col(5), col(6), col(9), col(10), col(11),
                            col(0), col(1), col(2), col(7), col(8), col(3), pad], axis=1).astype(BF16)
    lat_w = qr + kvr + di + LANES
    gq_col = 0
    z_col = gq_col + 2 * qk_w + v_w
    ga_col = z_col + v_w
    gb_col = ga_col + d
    lat_col = gb_col + d
    assert lat_col % lat_w == 0

    proj = _inproj(x2, p["norm1_w"], scale1, shift1, w_in, p["conv_w"], seq, gq_col, qk_w, hd)

    wuq_t = p["w_uq"].reshape(qr, nh * dh).T.astype(BF16)
    wuqi_t = p["w_uqi"].reshape(qr, nhi * di).T.astype(BF16)
    wuk_r = jnp.transpose(p["w_uk"], (1, 0, 2)).astype(BF16)
    wuv_t = jnp.transpose(p["w_uv"], (1, 2, 0)).astype(BF16)
    qabs_t, qi_t, ckv, ckvt, ki, wi_t = _dsa_prep(
        proj, bsz, seq, p["q_lat_norm_w"], p["kv_lat_norm_w"], p["idx_k_norm_w"], wuq_t, wuqi_t, wuk_r,
        (qr, kvr, di, nh, dh, nhi, 2 * nhv, lat_col // lat_w))
    o_a = _dsa_attn(ki, ckv, ckvt, qi_t, qabs_t, wi_t, wuv_t).reshape(bsz * seq, nh * dh)

    gc, beta, gct = _gdn_gates(proj, _pad_lanes(p["a_log"]), _pad_lanes(p["dt_bias"]), bsz, seq,
                               (lat_col + lat_w - LANES) // LANES, nhv)
    gct4 = gct.reshape(bsz, LANES, seq // GDN_CHUNK, GDN_CHUNK)
    o_b = _gdn_core(proj, gc, beta, gct4, p["gdn_norm_w"], bsz, seq, nqk, nhv, hd, gq_col, z_col)

    merged = _merge(o_a, o_b, p["w_branch_a"].astype(BF16), p["w_branch_b"].astype(BF16), proj, ga_col, gb_col)

    wr16 = jnp.zeros((d, LANES), F32).at[:, :ne].set(p["w_router"]).astype(BF16)
    br_p = _pad_lanes(p["b_router"], fill=-1e30)
    x1, h2, te, tg = _outproj_router(merged, x2, p["w_out"].astype(BF16), gate1, p["norm2_w"], scale2, shift2,
                                     wr16, br_p, seq)

    t = bsz * seq
    pos, cnt = _moe_rank(te)
    counts = cnt[0, :ne].astype(I32)
    padded = (counts + MOE_BM - 1) // MOE_BM * MOE_BM
    pad_end = jnp.cumsum(padded)
    pad_start = pad_end - padded
    dest = (pad_start[te[:, :TOP_K]] + pos[:, :TOP_K]).reshape(-1).astype(I32)
    n_blocks = -(-(t * TOP_K) // MOE_BM) + ne
    n_act = (pad_end[-1] // MOE_BM).astype(I32).reshape(1)
    blk = jnp.minimum(jnp.arange(n_blocks, dtype=I32), n_act[0] - 1) * MOE_BM
    block_e = jnp.minimum(jnp.sum(pad_end[None, :] <= blk[:, None], axis=1), ne - 1).astype(I32)
    blk_end = (jnp.arange(n_blocks, dtype=I32) + 1) * MOE_BM
    has_pad = jnp.any((blk_end[:, None] == pad_end[None, :]) & (padded[None, :] > 0), axis=1)
    fill_flags = (has_pad | (blk_end > pad_end[-1])).astype(I32)
    x_rows = _moe_dispatch(dest, fill_flags, h2, n_blocks * MOE_BM)
    act = _moe_gu(block_e, n_act, x_rows, p["w_gu"], p["b_gu"])
    y_rows = _moe_down(block_e, n_act, act, p["w_down"], p["b_down"])
    return _moe_combine(dest, x1, tg, gate2, final_w, y_rows, seq)


@jax.jit
def kernel(x, c, ada_w, ada_b, norm1_w, w_in, q_lat_norm_w, kv_lat_norm_w, idx_k_norm_w, w_uq, w_uqi, w_uk, w_uv, conv_w, a_log, dt_bias, gdn_norm_w, w_branch_a, w_branch_b, w_out, norm2_w, w_router, b_router, w_gu, b_gu, w_down, b_down, final_norm_w):
    bsz, seq, d = x.shape
    stacked = dict(ada_w=ada_w, ada_b=ada_b, norm1_w=norm1_w, w_in=w_in, q_lat_norm_w=q_lat_norm_w,
                   kv_lat_norm_w=kv_lat_norm_w, idx_k_norm_w=idx_k_norm_w, w_uq=w_uq, w_uqi=w_uqi, w_uk=w_uk,
                   w_uv=w_uv, conv_w=conv_w, a_log=a_log, dt_bias=dt_bias, gdn_norm_w=gdn_norm_w,
                   w_branch_a=w_branch_a, w_branch_b=w_branch_b, w_out=w_out, norm2_w=norm2_w,
                   w_router=w_router, b_router=b_router, w_gu=w_gu, b_gu=b_gu, w_down=w_down, b_down=b_down)
    depth = ada_w.shape[0]
    assert depth == 1, "the final norm is fused into the last layer's combine kernel"
    x2 = x.reshape(bsz * seq, d)
    p = {k: v[0] for k, v in stacked.items()}
    out = _layer(x2, c, bsz, seq, p, final_norm_w)
    return out.reshape(bsz, seq, d)
```

```python
import functools

import jax
import jax.numpy as jnp
from jax import lax
from jax.experimental import pallas as pl
from jax.experimental.pallas import tpu as pltpu

F32 = jnp.float32
BF16 = jnp.bfloat16
I32 = jnp.int32

EPS = 1e-6
LANES = 128
VMEM_LIMIT = 56 * 1024 * 1024

IDX_TOPK = 256
GDN_CHUNK = 64
TOP_K = 4
SWIGLU_LIMIT = 7.0
SWIGLU_ALPHA = 1.702
MOE_BM = 512
ATTN_TILES = 2
ONES_ROWS = 16
LOG2E = 1.4426950408889634
INT_MIN = -(2 ** 31)
INT_MAX = 2 ** 31 - 1


def _cp(*sem):
    return pltpu.CompilerParams(dimension_semantics=sem, vmem_limit_bytes=VMEM_LIMIT)


def _tile(n, pref):
    if n <= pref:
        return n
    t = pref - pref % LANES
    while n % t:
        t -= LANES
    return t


def _rms(x, w):
    return x * lax.rsqrt(jnp.mean(x * x, axis=-1, keepdims=True) + EPS) * w


def _sigmoid(x):
    return 0.5 * jnp.tanh(0.5 * x) + 0.5


def _dot(a, b):
    return jnp.dot(a, b, preferred_element_type=F32)


def _dot_nt(a, b):
    return lax.dot_general(a, b, (((1,), (1,)), ((), ())), preferred_element_type=F32)


def _dot_tn(a, b):
    return lax.dot_general(a, b, (((0,), (0,)), ((), ())), preferred_element_type=F32)


def _ada_body(ct_ref, w_ref, b_ref, o_ref, *, nb):
    ct = ct_ref[...]
    s = ct * jax.nn.sigmoid(ct)
    w = w_ref[...]
    for b in range(nb):
        o_ref[b:b + 1, :] = jnp.sum(w * s[:, b:b + 1], axis=0, keepdims=True) + b_ref[...]


def _ada(c, w, b):
    nb, d = c.shape
    n = w.shape[1]
    tn = _tile(n, 1024)
    return pl.pallas_call(
        functools.partial(_ada_body, nb=nb),
        grid=(n // tn,),
        in_specs=[pl.BlockSpec((d, nb), lambda j: (0, 0)),
                  pl.BlockSpec((d, tn), lambda j: (0, j)),
                  pl.BlockSpec((1, tn), lambda j: (0, j))],
        out_specs=pl.BlockSpec((nb, tn), lambda j: (0, j)),
        out_shape=jax.ShapeDtypeStruct((nb, n), F32),
        compiler_params=_cp("parallel"),
        name="ada",
    )(c.T, w, b.reshape(1, n))


def _inproj_body(x_ref, nw_ref, sc_ref, sh_ref, w_ref, cw_ref, o_ref, h_ref, carry_ref, *ubuf_refs,
                 tm, tn, sw, rc, hd, per_b, c0, n_qk, n_conv, q_scale):
    i = pl.program_id(0)
    j = pl.program_id(1)

    @pl.when(j == 0)
    def _():
        h = _rms(x_ref[...], nw_ref[...]) * (1.0 + sc_ref[...]) + sh_ref[...]
        h_ref[...] = h.astype(BF16)

    cj = j - c0
    in_conv = (cj >= 0) & (cj < n_conv)

    @pl.when(jnp.logical_not(in_conv))
    def _():
        o_ref[...] = _dot(h_ref[...], w_ref[...])

    @pl.when(in_conv)
    def _():
        taps = cw_ref.shape[0]

        @pl.when(i % per_b == 0)
        def _():
            carry_ref[cj] = jnp.zeros((8, tn), F32)

        prev = carry_ref[cj]
        scale = jnp.where(cj < n_qk, q_scale, 1.0)

        def conv_tile(normalise):
            tails = []
            for s, ubuf_ref in enumerate(ubuf_refs):
                cols = slice(s * sw, (s + 1) * sw)
                w = cw_ref[:, cols]
                u = _dot(h_ref[...], w_ref[:, cols])
                ubuf_ref[0:8, :] = prev[:, cols]
                ubuf_ref[8:, :] = u
                tails.append(u[tm - 8:, :])
                for r in range(tm // rc):
                    ext = ubuf_ref[r * rc:(r + 1) * rc + 8, :]
                    if taps == 4:
                        ext1 = pltpu.roll(ext, 1, axis=0)
                        near = ext * w[3:4, :] + ext1 * w[2:3, :]
                        far = ext * w[1:2, :] + ext1 * w[0:1, :]
                        y = near[8:] + pltpu.roll(far, 2, axis=0)[8:]
                    else:
                        y = ext[8:] * w[taps - 1:taps, :]
                        for k in range(1, taps):
                            y = y + pltpu.roll(ext, k, axis=0)[8:] * w[taps - 1 - k:taps - k, :]
                    y = y * _sigmoid(y)
                    if not normalise:
                        o_ref[r * rc:(r + 1) * rc, cols] = y
                        continue
                    for hh in range(sw // hd):
                        x = y[:, hh * hd:(hh + 1) * hd]
                        inv = lax.rsqrt(jnp.sum(x * x, axis=-1, keepdims=True) + EPS) * scale
                        c_lo = s * sw + hh * hd
                        o_ref[r * rc:(r + 1) * rc, c_lo:c_lo + hd] = x * inv
            carry_ref[cj] = jnp.concatenate(tails, axis=1)

        @pl.when(cj < 2 * n_qk)
        def _():
            conv_tile(True)

        @pl.when(cj >= 2 * n_qk)
        def _():
            conv_tile(False)


def _inproj(x2, nw, scale, shift, w16, conv_w, seq, conv_col0, qk_w, hd):
    t, d = x2.shape
    n = w16.shape[1]
    tm = min(seq, 1024)
    tn = _tile(n, 1024)
    per_b = seq // tm
    conv_ch = conv_w.shape[1]
    assert conv_col0 % tn == 0 and qk_w % tn == 0 and conv_ch % tn == 0 and tn % hd == 0
    c0, n_qk, n_conv = conv_col0 // tn, qk_w // tn, conv_ch // tn
    rc = min(tm, 64)
    sw = min(tn, 256)
    body = functools.partial(_inproj_body, tm=tm, tn=tn, sw=sw, rc=rc, hd=hd, per_b=per_b, c0=c0, n_qk=n_qk,
                             n_conv=n_conv, q_scale=hd ** -0.5)
    return pl.pallas_call(
        body,
        grid=(t // tm, n // tn),
        in_specs=[pl.BlockSpec((tm, d), lambda i, j: (i, 0)),
                  pl.BlockSpec((1, d), lambda i, j: (0, 0)),
                  pl.BlockSpec((None, 1, d), lambda i, j: (i // per_b, 0, 0)),
                  pl.BlockSpec((None, 1, d), lambda i, j: (i // per_b, 0, 0)),
                  pl.BlockSpec((d, tn), lambda i, j: (0, j)),
                  pl.BlockSpec((conv_w.shape[0], tn), lambda i, j: (0, jnp.clip(j - c0, 0, n_conv - 1)))],
        out_specs=pl.BlockSpec((tm, tn), lambda i, j: (i, j)),
        out_shape=jax.ShapeDtypeStruct((t, n), F32),
        scratch_shapes=[pltpu.VMEM((tm, d), BF16), pltpu.VMEM((n_conv, 8, tn), F32)]
        + [pltpu.VMEM((tm + 8, sw), F32) for _ in range(tn // sw)],
        compiler_params=_cp("arbitrary", "arbitrary"),
        name="inproj",
    )(x2, nw.reshape(1, d), scale[:, None, :], shift[:, None, :], w16, conv_w)


def _dsa_prep_body(lat_ref, qnw_ref, kvnw_ref, knw_ref, wuq_ref, wuqi_ref, wuk_ref,
                   qabs_ref, qi_ref, ckv_ref, ckvt_ref, ki_ref, wi_ref,
                   *, qr, kvr, di, nh, dh, nhi, q_scale, i_scale, w_scale, wi_off):
    lat = lat_ref[...]
    cq_t = _rms(lat[:, :qr], qnw_ref[...]).T.astype(BF16)
    q_t = _dot(wuq_ref[...], cq_t)
    for h in range(nh):
        q_h = q_t[h * dh:(h + 1) * dh].astype(BF16)
        qabs_ref[h] = (_dot(wuk_ref[h], q_h) * q_scale).astype(BF16)
    qi_t = _dot(wuqi_ref[...], cq_t) * i_scale
    for h in range(nhi):
        qi_ref[h] = qi_t[h * di:(h + 1) * di].astype(BF16)
    ckv = _rms(lat[:, qr:qr + kvr], kvnw_ref[...])
    ckv_ref[...] = ckv.astype(BF16)
    ckvt_ref[0:kvr, :] = ckv.T.astype(BF16)
    ckvt_ref[kvr:, :] = jnp.ones((ONES_ROWS, ckvt_ref.shape[1]), BF16)
    ki_ref[...] = _rms(lat[:, qr + kvr:qr + kvr + di], knw_ref[...]).astype(BF16)
    misc_t = lat[:, qr + kvr + di:].T
    wi_ref[...] = misc_t[wi_off:wi_off + nhi] * w_scale


def _dsa_prep(proj, bsz, seq, qnw, kvnw, knw, wuq_t, wuqi_t, wuk_r, dims):
    qr, kvr, di, nh, dh, nhi, wi_off, lat_blk = dims
    tm = min(seq, 512)
    ns = seq // tm
    lat_w = qr + kvr + di + LANES
    body = functools.partial(
        _dsa_prep_body, qr=qr, kvr=kvr, di=di, nh=nh, dh=dh, nhi=nhi,
        q_scale=dh ** -0.5 * LOG2E, i_scale=di ** -0.5, w_scale=nhi ** -0.5, wi_off=wi_off)
    full = lambda shape: pl.BlockSpec(shape, lambda b, s: (0,) * len(shape))
    return pl.pallas_call(
        body,
        grid=(bsz, ns),
        in_specs=[pl.BlockSpec((tm, lat_w), lambda b, s: (b * ns + s, lat_blk)),
                  full((1, qr)), full((1, kvr)), full((1, di)),
                  full((nh * dh, qr)), full((nhi * di, qr)), full((nh, kvr, dh))],
        out_specs=[pl.BlockSpec((None, nh, kvr, tm), lambda b, s: (b, 0, 0, s)),
                   pl.BlockSpec((None, nhi, di, tm), lambda b, s: (b, 0, 0, s)),
                   pl.BlockSpec((None, tm, kvr), lambda b, s: (b, s, 0)),
                   pl.BlockSpec((None, kvr + ONES_ROWS, tm), lambda b, s: (b, 0, s)),
                   pl.BlockSpec((None, tm, di), lambda b, s: (b, s, 0)),
                   pl.BlockSpec((None, nhi, tm), lambda b, s: (b, 0, s))],
        out_shape=[jax.ShapeDtypeStruct((bsz, nh, kvr, seq), BF16),
                   jax.ShapeDtypeStruct((bsz, nhi, di, seq), BF16),
                   jax.ShapeDtypeStruct((bsz, seq, kvr), BF16),
                   jax.ShapeDtypeStruct((bsz, kvr + ONES_ROWS, seq), BF16),
                   jax.ShapeDtypeStruct((bsz, seq, di), BF16),
                   jax.ShapeDtypeStruct((bsz, nhi, seq), F32)],
        compiler_params=_cp("parallel", "parallel"),
        name="dsa_prep",
    )(proj, qnw.reshape(1, qr), kvnw.reshape(1, kvr), knw.reshape(1, di), wuq_t, wuqi_t, wuk_r)


def _dsa_attn_body(ki_ref, ckv_ref, ckvt_ref, qi_ref, qabs_ref, wi_ref, wuv_ref, o_ref,
                   keys_ref, k16_ref, acc_ref, m_ref, ot_ref, thr_ref, jlim_ref,
                   *, tq, tk, at, nh, nhi, dh, kvr, topk, seq_bits):
    qb = pl.program_id(1)
    nk = ((qb + 1) * tq + tk - 1) // tk
    q_pos = qb * tq + lax.broadcasted_iota(I32, (tk, tq), 1)
    k_iota = lax.broadcasted_iota(I32, (tk, tq), 0)

    def score_tile(kt, _):
        ki_t = ki_ref[pl.ds(pl.multiple_of(kt * tk, tk), tk), :]
        acc = jnp.zeros((tk, tq), F32)
        for h in range(nhi):
            rel = _dot(ki_t, qi_ref[h])
            acc = acc + wi_ref[h:h + 1, :] * jnp.maximum(rel, 0.0)
        acc = jnp.where(acc == 0.0, 0.0, acc)
        bits = lax.bitcast_convert_type(acc, I32)
        causal = kt * tk + k_iota <= q_pos
        key = jnp.where(bits < 0, bits ^ INT_MAX, bits)
        keys_ref[pl.ds(pl.multiple_of(kt * tk, tk), tk), :] = jnp.where(causal, key, INT_MIN)
        top = lax.bitcast_convert_type(bits & -65536, F32)
        k16_ref[pl.ds(pl.multiple_of(kt * tk, tk), tk), :] = jnp.where(causal, top, jnp.nan).astype(BF16)
        return 0

    lax.fori_loop(0, nk, score_tile, 0)

    one16 = jnp.ones((tk, tq), BF16)
    zero16 = jnp.zeros((tk, tq), BF16)

    def count16(cand_bf):
        cand_b = jnp.broadcast_to(cand_bf, (tk, tq))

        def tile(kt, c):
            k_t = k16_ref[pl.ds(pl.multiple_of(kt * tk, tk), tk), :]
            hit = jnp.where(k_t >= cand_b, one16, zero16)
            part = hit[0:16]
            for r in range(1, tk // 16):
                part = part + hit[r * 16:(r + 1) * 16]
            return c + part.astype(F32)
        c16 = lax.fori_loop(0, nk, tile, jnp.zeros((16, tq), F32))
        return jnp.sum(c16, axis=0, keepdims=True)

    def hi_step(it, hi):
        cand = hi + lax.shift_left(jnp.int32(1), 15 - it)
        pattern = jnp.where(cand >= 0, cand, cand ^ 0x7FFF) & 0xFFFF
        cand_bf = lax.bitcast_convert_type(lax.shift_left(pattern, 16), F32).astype(BF16)
        return jnp.where(count16(cand_bf) >= topk, cand, hi)

    hi16 = lax.fori_loop(0, 16, hi_step, jnp.full((1, tq), -(2 ** 15), I32))

    def count(pred_fn):
        def tile(kt, c):
            k_t = keys_ref[pl.ds(pl.multiple_of(kt * tk, tk), tk), :]
            hit = jnp.where(pred_fn(k_t, kt), 1, 0).astype(I32)
            return c + jnp.sum(hit.reshape(tk // 8, 8, tq), axis=0)
        c8 = lax.fori_loop(0, nk, tile, jnp.zeros((8, tq), I32))
        return jnp.sum(c8, axis=0, keepdims=True)

    def bit_cond(carry):
        it, _, settled = carry
        return (it < 32) & (jnp.min(settled) == 0)

    def bit_step(carry):
        it, thr, settled = carry
        for _ in range(4):
            cand = thr ^ lax.shift_left(jnp.int32(1), 31 - it)
            cnt = count(lambda k_t, kt: k_t >= cand)
            take = (cnt >= topk) & (settled == 0)
            settled = jnp.where(take & (cnt == topk), 1, settled)
            thr = jnp.where(take, cand, thr)
            it = it + 1
        return it, thr, settled

    _, thr, _ = lax.while_loop(
        bit_cond, bit_step, (jnp.int32(16), lax.shift_left(hi16, 16), jnp.zeros((1, tq), I32)))
    thr = jnp.maximum(thr, INT_MIN + 1)
    cnt_ge = count(lambda k_t, kt: k_t >= thr)
    thr_ref[...] = thr
    jlim_ref[...] = jnp.full((1, tq), INT_MAX, I32)

    @pl.when(jnp.max(cnt_ge) > topk)
    def _():
        need = topk - count(lambda k_t, kt: k_t > thr)

        def pos_step(it, p):
            cand = p + lax.shift_left(jnp.int32(1), seq_bits - 1 - it)
            cnt = count(lambda k_t, kt: (k_t == thr) & (kt * tk + k_iota < cand))
            return jnp.where(cnt < need, cand, p)

        jlim_ref[...] = lax.fori_loop(0, seq_bits, pos_step, jnp.zeros((1, tq), I32))

    m_ref[...] = jnp.full(m_ref.shape, -jnp.inf, F32)
    acc_ref[...] = jnp.zeros(acc_ref.shape, F32)

    na = (nk + at - 1) // at

    def blank(kt, _):
        keys_ref[pl.ds(pl.multiple_of(kt * tk, tk), tk), :] = jnp.full((tk, tq), INT_MIN, I32)
        return 0

    lax.fori_loop(nk, na * at, blank, 0)
    ta = at * tk
    a_iota = lax.broadcasted_iota(I32, (ta, tq), 0)

    def attn_tile(kt, _):
        off = pl.multiple_of(kt * ta, ta)
        k_t = keys_ref[pl.ds(off, ta), :]
        thr_b = thr_ref[...]
        mask = (k_t > thr_b) | ((k_t == thr_b) & (kt * ta + a_iota <= jlim_ref[...]))
        bias = jnp.where(mask, 0.0, -jnp.inf)
        ckv_t = ckv_ref[pl.ds(off, ta), :]
        ckvt_t = ckvt_ref[:, pl.ds(off, ta)]
        logits = _dot(ckv_t, qabs_ref[0])
        for h in range(nh):
            s = logits + bias
            if h + 1 < nh:
                logits = _dot(ckv_t, qabs_ref[h + 1])
            m_old = m_ref[h]
            m_new = jnp.maximum(m_old, jnp.max(s, axis=0, keepdims=True))
            m_use = jnp.where(m_new == -jnp.inf, 0.0, m_new)
            p = jnp.exp2(s - m_use)
            alpha = jnp.exp2(m_old - m_use)
            acc_ref[h] = alpha * acc_ref[h] + _dot(ckvt_t, p.astype(BF16))
            m_ref[h] = m_new
        return 0

    lax.fori_loop(0, na, attn_tile, 0)

    for h in range(nh):
        acc = acc_ref[h]
        o_lat = (acc[:kvr] * (1.0 / acc[kvr:kvr + 1])).astype(BF16)
        ot_ref[h * dh:(h + 1) * dh, :] = _dot(wuv_ref[h], o_lat)
    o_ref[...] = ot_ref[...].T.astype(BF16)


def _dsa_attn(ki, ckv, ckvt, qi_t, qabs_t, wi_t, wuv_t):
    bsz, seq, di = ki.shape
    kvr = ckv.shape[2]
    nh, dh = wuv_t.shape[0], wuv_t.shape[1]
    nhi = qi_t.shape[1]
    tq = min(seq // 2, 256)
    tk = tq
    at = min(ATTN_TILES, seq // tk)
    assert seq % (at * tk) == 0
    topk = min(IDX_TOPK, seq // 4)
    body = functools.partial(_dsa_attn_body, tq=tq, tk=tk, at=at, nh=nh, nhi=nhi, dh=dh, kvr=kvr, topk=topk,
                             seq_bits=max(1, (seq - 1).bit_length()))
    return pl.pallas_call(
        body,
        grid=(bsz, seq // tq),
        in_specs=[pl.BlockSpec((None, seq, di), lambda b, q: (b, 0, 0)),
                  pl.BlockSpec((None, seq, kvr), lambda b, q: (b, 0, 0)),
                  pl.BlockSpec((None, kvr + ONES_ROWS, seq), lambda b, q: (b, 0, 0)),
                  pl.BlockSpec((None, nhi, di, tq), lambda b, q: (b, 0, 0, q)),
                  pl.BlockSpec((None, nh, kvr, tq), lambda b, q: (b, 0, 0, q)),
                  pl.BlockSpec((None, nhi, tq), lambda b, q: (b, 0, q)),
                  pl.BlockSpec((nh, dh, kvr), lambda b, q: (0, 0, 0))],
        out_specs=pl.BlockSpec((None, tq, nh * dh), lambda b, q: (b, q, 0)),
        out_shape=jax.ShapeDtypeStruct((bsz, seq, nh * dh), BF16),
        scratch_shapes=[pltpu.VMEM((seq, tq), I32),
                        pltpu.VMEM((seq, tq), BF16),
                        pltpu.VMEM((nh, kvr + ONES_ROWS, tq), F32),
                        pltpu.VMEM((nh, 1, tq), F32),
                        pltpu.VMEM((nh * dh, tq), F32),
                        pltpu.VMEM((1, tq), I32),
                        pltpu.VMEM((1, tq), I32)],
        compiler_params=_cp("parallel", "parallel"),
        name="dsa_attn",
    )(ki, ckv, ckvt, qi_t, qabs_t, wi_t, wuv_t)


GDN_GROUP = 16
GDN_INV_BATCH = 64


def _gdn_gate_body(m_ref, alog_ref, dtb_ref, gc_ref, beta_ref, gct_ref, *, ts, chunk, nhv, ng):
    x = m_ref[...]
    z = x + dtb_ref[...]
    softplus = jnp.maximum(z, 0.0) + jnp.log(1.0 + jnp.exp(-jnp.abs(z)))
    g = -jnp.exp(alog_ref[...]) * softplus
    pos = lax.broadcasted_iota(I32, (ts, LANES), 0) % chunk
    d = 1
    while d < chunk:
        g = g + jnp.where(pos >= d, pltpu.roll(g, d, axis=0), 0.0)
        d *= 2
    gct_ref[...] = g.T
    beta = jax.nn.sigmoid(x)
    for j in range(nhv // ng):
        gc_ref[:, j * LANES:(j + 1) * LANES] = pltpu.roll(g, (LANES - j * ng) % LANES, axis=1)
        beta_ref[:, j * LANES:(j + 1) * LANES] = pltpu.roll(beta, (2 * LANES - nhv - j * ng) % LANES, axis=1)


def _gdn_gates(proj, alog_p, dtb_p, bsz, seq, misc_blk, nhv):
    ts = min(seq, 512)
    ns = seq // ts
    ng = GDN_GROUP
    gw = nhv // ng * LANES
    body = functools.partial(_gdn_gate_body, ts=ts, chunk=GDN_CHUNK, nhv=nhv, ng=ng)
    return pl.pallas_call(
        body,
        grid=(bsz, ns),
        in_specs=[pl.BlockSpec((ts, LANES), lambda b, s: (b * ns + s, misc_blk)),
                  pl.BlockSpec((1, LANES), lambda b, s: (0, 0)),
                  pl.BlockSpec((1, LANES), lambda b, s: (0, 0))],
        out_specs=[pl.BlockSpec((ts, gw), lambda b, s: (b * ns + s, 0)),
                   pl.BlockSpec((ts, gw), lambda b, s: (b * ns + s, 0)),
                   pl.BlockSpec((None, LANES, ts), lambda b, s: (b, 0, s))],
        out_shape=[jax.ShapeDtypeStruct((bsz * seq, gw), F32),
                   jax.ShapeDtypeStruct((bsz * seq, gw), F32),
                   jax.ShapeDtypeStruct((bsz, LANES, seq), F32)],
        compiler_params=_cp("parallel", "parallel"),
        name="gdn_gates",
    )(proj, alog_p, dtb_p)


def _gdn_core_body(q_ref, k_ref, v_ref, gc_ref, beta_ref, gr_ref, z_ref, nw_ref, o_ref, *scratch,
                   ts, chunk, hd, ng, rep, gr_steps):
    st_refs = scratch[:ng]
    gcb_ref, bb_ref, a_ref, u_ref, wq_ref, qkkt_ref, rhs_ref, egl_ref = scratch[ng:]
    nc = ts // chunk
    gr_base = (pl.program_id(2) % gr_steps) * nc

    @pl.when(pl.program_id(2) == 0)
    def _():
        for st_ref in st_refs:
            st_ref[...] = jnp.zeros(st_ref.shape, F32)

    for g in range(ng):
        gcb_ref[g] = jnp.broadcast_to(gc_ref[:, g:g + 1], (ts, LANES))
        bb_ref[g] = jnp.broadcast_to(beta_ref[:, g:g + 1], (ts, LANES))

    ri = lax.broadcasted_iota(I32, (chunk, chunk), 0)
    ci = lax.broadcasted_iota(I32, (chunk, chunk), 1)
    incl = ri >= ci
    strict = ri > ci
    eye = jnp.where(ri == ci, 1.0, 0.0).astype(F32)

    def prep(c, _):
        rows = pl.ds(pl.multiple_of(c * chunk, chunk), chunk)
        for qh in range(ng // rep):
            q = q_ref[rows, qh * hd:(qh + 1) * hd]
            k = k_ref[rows, qh * hd:(qh + 1) * hd]
            k16 = k.astype(BF16)
            kk = _dot_nt(k16, k16)
            qk = _dot_nt(q.astype(BF16), k16)
            a_pair = []
            for vh in range(rep):
                g = qh * rep + vh
                n = c * ng + g
                gcb = gcb_ref[g, rows, :]
                bb = bb_ref[g, rows, :]
                g_row = gr_ref[g, pl.ds(gr_base + c, 1), :]
                decay = jnp.where(incl, jnp.exp(gcb[:, :chunk] - g_row), 0.0)
                a_pair.append(jnp.where(strict, kk * bb[:, :chunk] * decay, 0.0))
                qkkt_ref[n, 0:chunk, :] = (qk * decay).astype(BF16)
                eg = jnp.exp(gcb)
                rhs_ref[n, :, 0:hd] = (v_ref[rows, g * hd:(g + 1) * hd] * bb).astype(BF16)
                rhs_ref[n, :, hd:2 * hd] = (k * bb * eg).astype(BF16)
                wq_ref[n, chunk:2 * chunk, :] = (q * eg).astype(BF16)
                g_last = gcb[chunk - 1:chunk, :]
                qkkt_ref[n, chunk:chunk + hd, :] = (k * jnp.exp(g_last - gcb)).T.astype(BF16)
                egl_ref[n] = jnp.broadcast_to(jnp.exp(g_last), (8, LANES))
            a_ref[c * (ng // rep) + qh] = jnp.concatenate(a_pair, axis=1)
        return 0

    lax.fori_loop(0, nc, prep, 0)

    n_dbl = max(1, (chunk - 1).bit_length()) - 1

    lane2 = lax.broadcasted_iota(I32, (chunk, 2 * chunk), 1)
    left = lane2 < chunk
    eye2 = jnp.where(lax.broadcasted_iota(I32, (chunk, 2 * chunk), 0) == lane2 % chunk, 1.0, 0.0).astype(F32)

    def blockdiag(x):
        return jnp.concatenate([jnp.where(left, x, 0.0), jnp.where(left, 0.0, x)], axis=0).astype(BF16)

    def invert(ib, _):
        ms_ = [ib * (GDN_INV_BATCH // 2) + j for j in range(GDN_INV_BATCH // 2)]
        pw = [a_ref[m] for m in ms_]
        tm = [eye2 - a for a in pw]
        pw = [_dot(a.astype(BF16), blockdiag(a)) for a in pw]
        for lvl in range(n_dbl):
            last = lvl == n_dbl - 1
            lhs = [(t if last else jnp.concatenate([t, x], axis=0)).astype(BF16) for t, x in zip(tm, pw)]
            prod = [_dot(y, blockdiag(x)) for y, x in zip(lhs, pw)]
            tm = [t + y[:chunk] for t, y in zip(tm, prod)]
            pw = [y[chunk:] for y in prod]
        for m, t in zip(ms_, tm):
            zero = jnp.zeros((chunk, 2 * hd), BF16)
            rhs2 = jnp.concatenate([jnp.concatenate([rhs_ref[2 * m], zero], axis=1),
                                    jnp.concatenate([zero, rhs_ref[2 * m + 1]], axis=1)], axis=0)
            uw = _dot(t.astype(BF16), rhs2)
            for j in range(2):
                u_ref[2 * m + j] = uw[:, 2 * j * hd:(2 * j + 1) * hd]
                wq_ref[2 * m + j, 0:chunk, :] = uw[:, (2 * j + 1) * hd:(2 * j + 2) * hd].astype(BF16)
        return 0

    lax.fori_loop(0, nc * ng // GDN_INV_BATCH, invert, 0)

    nw = nw_ref[...]

    def step(c, _):
        rows = pl.ds(pl.multiple_of(c * chunk, chunk), chunk)
        heads = range(ng)
        ns_ = [c * ng + g for g in heads]
        state = [st_refs[g][...] for g in heads]
        st16 = [x.astype(BF16) for x in state]
        wq = [_dot(wq_ref[n], x) for n, x in zip(ns_, st16)]
        vn16 = [(u_ref[n] - x[:chunk]).astype(BF16) for n, x in zip(ns_, wq)]
        ok = [_dot(qkkt_ref[n], x) for n, x in zip(ns_, vn16)]
        o = [x[chunk:] + y[:chunk] for x, y in zip(wq, ok)]
        ds = [y[chunk:] for y in ok]
        for g in heads:
            st_refs[g][...] = state[g] * egl_ref[ns_[g]][0:1, :] + ds[g]
            zz = z_ref[rows, g * hd:(g + 1) * hd]
            o_ref[rows, g * hd:(g + 1) * hd] = (_rms(o[g], nw) * (zz * _sigmoid(zz))).astype(o_ref.dtype)
        return 0

    lax.fori_loop(0, nc, step, 0)


def _gdn_core(proj, gc, beta, gct4, norm_w, bsz, seq, nqk, nhv, hd, q_col, z_col):
    assert hd == LANES
    chunk = GDN_CHUNK
    ng = GDN_GROUP
    rep = nhv // nqk
    ts = min(seq, 256)
    ns = seq // ts
    n = (ts // chunk) * ng
    assert rep == 2 and ng % rep == 0 and nhv % ng == 0 and n % GDN_INV_BATCH == 0
    qw, vw = (ng // rep) * hd, ng * hd
    k_col, v_col = q_col + nqk * hd, q_col + 2 * nqk * hd
    assert q_col % qw == 0 and k_col % qw == 0 and v_col % vw == 0 and z_col % vw == 0
    gr_steps = max(1, 8 * chunk // ts)
    assert ns % gr_steps == 0
    body = functools.partial(_gdn_core_body, ts=ts, chunk=chunk, hd=hd, ng=ng, rep=rep, gr_steps=gr_steps)
    row = lambda b, h, s: b * ns + s
    return pl.pallas_call(
        body,
        grid=(bsz, nhv // ng, ns),
        in_specs=[pl.BlockSpec((ts, qw), lambda b, h, s: (row(b, h, s), q_col // qw + h)),
                  pl.BlockSpec((ts, qw), lambda b, h, s: (row(b, h, s), k_col // qw + h)),
                  pl.BlockSpec((ts, vw), lambda b, h, s: (row(b, h, s), v_col // vw + h)),
                  pl.BlockSpec((ts, LANES), lambda b, h, s: (row(b, h, s), h)),
                  pl.BlockSpec((ts, LANES), lambda b, h, s: (row(b, h, s), h)),
                  pl.BlockSpec((None, ng, gr_steps * ts // chunk, chunk), lambda b, h, s: (b, h, s // gr_steps, 0)),
                  pl.BlockSpec((ts, vw), lambda b, h, s: (row(b, h, s), z_col // vw + h)),
                  pl.BlockSpec((1, hd), lambda b, h, s: (0, 0))],
        out_specs=pl.BlockSpec((ts, vw), lambda b, h, s: (row(b, h, s), h)),
        out_shape=jax.ShapeDtypeStruct((bsz * seq, nhv * hd), BF16),
        scratch_shapes=[pltpu.VMEM((hd, hd), F32) for _ in range(ng)] + [
                        pltpu.VMEM((ng, ts, LANES), F32),
                        pltpu.VMEM((ng, ts, LANES), F32),
                        pltpu.VMEM((n // 2, chunk, 2 * chunk), F32),
                        pltpu.VMEM((n, chunk, hd), F32),
                        pltpu.VMEM((n, 2 * chunk, hd), BF16),
                        pltpu.VMEM((n, chunk + hd, chunk), BF16),
                        pltpu.VMEM((n, chunk, 2 * hd), BF16),
                        pltpu.VMEM((n, 8, LANES), F32)],
        compiler_params=_cp("parallel", "parallel", "arbitrary"),
        name="gdn_core",
    )(proj, proj, proj, gc, beta, gct4, proj, norm_w.reshape(1, hd))


def _merge_body(a_ref, b_ref, wa_ref, wb_ref, ga_ref, gb_ref, o_ref):
    ya = _dot(a_ref[...], wa_ref[...])
    yb = _dot(b_ref[...], wb_ref[...])
    o_ref[...] = (_sigmoid(ga_ref[...]) * ya + _sigmoid(gb_ref[...]) * yb).astype(o_ref.dtype)


def _merge(o_a, o_b, wa16, wb16, proj, ga_col0, gb_col0):
    t, ka = o_a.shape
    kb = o_b.shape[1]
    d = wa16.shape[1]
    tm = min(t, 1024)
    tn = _tile(d, 512)
    ga0, gb0 = ga_col0 // tn, gb_col0 // tn
    return pl.pallas_call(
        _merge_body,
        grid=(t // tm, d // tn),
        in_specs=[pl.BlockSpec((tm, ka), lambda i, j: (i, 0)),
                  pl.BlockSpec((tm, kb), lambda i, j: (i, 0)),
                  pl.BlockSpec((ka, tn), lambda i, j: (0, j)),
                  pl.BlockSpec((kb, tn), lambda i, j: (0, j)),
                  pl.BlockSpec((tm, tn), lambda i, j: (i, ga0 + j)),
                  pl.BlockSpec((tm, tn), lambda i, j: (i, gb0 + j))],
        out_specs=pl.BlockSpec((tm, tn), lambda i, j: (i, j)),
        out_shape=jax.ShapeDtypeStruct((t, d), BF16),
        compiler_params=_cp("parallel", "parallel"),
        name="merge",
    )(o_a, o_b, wa16, wb16, proj, proj)


def _outproj_router_body(m_ref, x_ref, wo_ref, g1_ref, nw_ref, sc_ref, sh_ref, wr_ref, br_ref,
                         x1_ref, h2_ref, te_ref, tg_ref, *, topk):
    x1 = x_ref[...] + g1_ref[...] * _dot(m_ref[...], wo_ref[...])
    x1_ref[...] = x1
    h2 = _rms(x1, nw_ref[...]) * (1.0 + sc_ref[...]) + sh_ref[...]
    h2_ref[...] = h2
    logits = _dot(h2.astype(BF16), wr_ref[...]) + br_ref[...]
    lane = lax.broadcasted_iota(I32, logits.shape, 1)
    te = jnp.zeros(logits.shape, I32)
    tg = jnp.zeros(logits.shape, F32)
    denom = jnp.zeros((logits.shape[0], 1), F32)
    m0 = None
    for kk in range(topk):
        mx = jnp.max(logits, axis=-1, keepdims=True)
        idx = jnp.min(jnp.where(logits == mx, lane, LANES), axis=-1, keepdims=True)
        if kk == 0:
            m0 = mx
        e = jnp.exp(mx - m0)
        denom = denom + e
        te = jnp.where(lane == kk, idx, te)
        tg = jnp.where(lane == kk, e, tg)
        logits = jnp.where(lane == idx, -jnp.inf, logits)
    te_ref[...] = te
    tg_ref[...] = tg / denom


def _outproj_router(merged, x2, wo16, gate1, nw, scale, shift, wr16, br_p, seq):
    t, d = x2.shape
    tm = min(seq, 512)
    per_b = seq // tm
    vec = lambda: pl.BlockSpec((None, 1, d), lambda i: (i // per_b, 0, 0))
    return pl.pallas_call(
        functools.partial(_outproj_router_body, topk=TOP_K),
        grid=(t // tm,),
        in_specs=[pl.BlockSpec((tm, d), lambda i: (i, 0)),
                  pl.BlockSpec((tm, d), lambda i: (i, 0)),
                  pl.BlockSpec((d, d), lambda i: (0, 0)),
                  vec(),
                  pl.BlockSpec((1, d), lambda i: (0, 0)),
                  vec(), vec(),
                  pl.BlockSpec((d, LANES), lambda i: (0, 0)),
                  pl.BlockSpec((1, LANES), lambda i: (0, 0))],
        out_specs=[pl.BlockSpec((tm, d), lambda i: (i, 0)),
                   pl.BlockSpec((tm, d), lambda i: (i, 0)),
                   pl.BlockSpec((tm, LANES), lambda i: (i, 0)),
                   pl.BlockSpec((tm, LANES), lambda i: (i, 0))],
        out_shape=[jax.ShapeDtypeStruct((t, d), F32),
                   jax.ShapeDtypeStruct((t, d), F32),
                   jax.ShapeDtypeStruct((t, LANES), I32),
                   jax.ShapeDtypeStruct((t, LANES), F32)],
        compiler_params=_cp("parallel"),
        name="outproj_router",
    )(merged, x2, wo16, gate1[:, None, :], nw.reshape(1, d), scale[:, None, :], shift[:, None, :], wr16, br_p)


def _moe_rank_body(te_ref, pos_ref, cnt_ref, carry_ref, *, tt, topk):
    @pl.when(pl.program_id(0) == 0)
    def _():
        carry_ref[...] = jnp.zeros(carry_ref.shape, F32)

    te = te_ref[...]
    lane = lax.broadcasted_iota(I32, (tt, LANES), 1)
    onehot = jnp.zeros((tt, LANES), F32)
    for kk in range(topk):
        onehot = onehot + jnp.where(lane == te[:, kk:kk + 1], 1.0, 0.0)
    r = lax.broadcasted_iota(I32, (tt, tt), 0)
    c = lax.broadcasted_iota(I32, (tt, tt), 1)
    below = jnp.where(r > c, 1.0, 0.0).astype(BF16)
    rank = _dot(below, onehot.astype(BF16)) + carry_ref[0:1, :]
    pos = jnp.zeros((tt, LANES), I32)
    for kk in range(topk):
        p = jnp.sum(jnp.where(lane == te[:, kk:kk + 1], rank, 0.0), axis=-1, keepdims=True)
        pos = jnp.where(lane == kk, p.astype(I32), pos)
    pos_ref[...] = pos
    carry_ref[...] = carry_ref[...] + jnp.sum(onehot, axis=0, keepdims=True)
    cnt_ref[...] = carry_ref[...]


def _moe_rank(te):
    t = te.shape[0]
    tt = min(t, 512)
    return pl.pallas_call(
        functools.partial(_moe_rank_body, tt=tt, topk=TOP_K),
        grid=(t // tt,),
        in_specs=[pl.BlockSpec((tt, LANES), lambda i: (i, 0))],
        out_specs=[pl.BlockSpec((tt, LANES), lambda i: (i, 0)),
                   pl.BlockSpec((8, LANES), lambda i: (0, 0))],
        out_shape=[jax.ShapeDtypeStruct((t, LANES), I32),
                   jax.ShapeDtypeStruct((8, LANES), F32)],
        scratch_shapes=[pltpu.VMEM((8, LANES), F32)],
        compiler_params=_cp("arbitrary"),
        name="moe_rank",
    )(te)


def _moe_dispatch_body(dest_ref, fill_ref, h_ref, x_hbm, zero_ref, sem, fill_sem, *, tt, topk, bm, n_blocks):
    @pl.when(pl.program_id(0) == 0)
    def _():
        zero_ref[...] = jnp.zeros(zero_ref.shape, zero_ref.dtype)

        def fill_copy(blk):
            return pltpu.make_async_copy(zero_ref, x_hbm.at[pl.ds(pl.multiple_of(blk * bm, bm), bm)], fill_sem)

        def start(blk, _):
            @pl.when(fill_ref[blk] != 0)
            def _():
                fill_copy(blk).start()
            return 0

        def drain(blk, _):
            @pl.when(fill_ref[blk] != 0)
            def _():
                fill_copy(blk).wait()
            return 0

        lax.fori_loop(0, n_blocks, start, 0)
        lax.fori_loop(0, n_blocks, drain, 0)

    base = pl.program_id(0) * tt * topk

    def issue(t, _):
        for kk in range(topk):
            dst = x_hbm.at[pl.ds(dest_ref[base + t * topk + kk], 1)]
            pltpu.make_async_copy(h_ref.at[pl.ds(t, 1)], dst, sem).start()
        return 0

    lax.fori_loop(0, tt, issue, 0, unroll=4)
    for _ in range(topk):
        pltpu.make_async_copy(h_ref, x_hbm.at[pl.ds(0, tt)], sem).wait()


def _moe_dispatch(dest_flat, fill_flags, h2, n_rows):
    t, d = h2.shape
    tt = min(t, 128)
    bm = MOE_BM
    grid_spec = pltpu.PrefetchScalarGridSpec(
        num_scalar_prefetch=2,
        grid=(t // tt,),
        in_specs=[pl.BlockSpec((tt, d), lambda i, dest, fill: (i, 0))],
        out_specs=pl.BlockSpec(memory_space=pl.ANY),
        scratch_shapes=[pltpu.VMEM((bm, d), F32), pltpu.SemaphoreType.DMA(()), pltpu.SemaphoreType.DMA(())])
    return pl.pallas_call(
        functools.partial(_moe_dispatch_body, tt=tt, topk=TOP_K, bm=bm, n_blocks=n_rows // bm),
        grid_spec=grid_spec,
        out_shape=jax.ShapeDtypeStruct((n_rows, d), F32),
        compiler_params=_cp("arbitrary"),
        name="moe_dispatch",
    )(dest_flat, fill_flags, h2)


def _moe_gu_body(be_ref, na_ref, x_ref, wg_ref, wu_ref, bg_ref, bu_ref, act_ref, wg16_ref, wu16_ref):
    i = pl.program_id(1)
    prev = be_ref[jnp.maximum(i - 1, 0)]

    @pl.when((i == 0) | (be_ref[i] != prev))
    def _():
        wg16_ref[...] = wg_ref[...].astype(BF16)
        wu16_ref[...] = wu_ref[...].astype(BF16)

    @pl.when(i < na_ref[0])
    def _():
        x = x_ref[...].astype(BF16)
        g = jnp.minimum(_dot(x, wg16_ref[...]) + bg_ref[...], SWIGLU_LIMIT)
        u = jnp.clip(_dot(x, wu16_ref[...]) + bu_ref[...], -SWIGLU_LIMIT, SWIGLU_LIMIT)
        act_ref[...] = ((u + 1.0) * (g * _sigmoid(SWIGLU_ALPHA * g))).astype(act_ref.dtype)

    @pl.when(i >= na_ref[0])
    def _():
        act_ref[...] = jnp.zeros(act_ref.shape, act_ref.dtype)


def _moe_gu(block_e, n_act, x_rows, w_gu, b_gu):
    n_rows, d = x_rows.shape
    ne, _, ff2 = w_gu.shape
    ff = ff2 // 2
    bm = MOE_BM
    tn = _tile(ff, 1024)
    nt = ff // tn
    rowblk = lambda j, i, be, na: (jnp.minimum(i, na[0] - 1), 0)
    grid_spec = pltpu.PrefetchScalarGridSpec(
        num_scalar_prefetch=2,
        grid=(nt, n_rows // bm),
        in_specs=[pl.BlockSpec((bm, d), rowblk),
                  pl.BlockSpec((None, d, tn), lambda j, i, be, na: (be[i], 0, j)),
                  pl.BlockSpec((None, d, tn), lambda j, i, be, na: (be[i], 0, nt + j)),
                  pl.BlockSpec((None, 1, tn), lambda j, i, be, na: (be[i], 0, j)),
                  pl.BlockSpec((None, 1, tn), lambda j, i, be, na: (be[i], 0, nt + j))],
        out_specs=pl.BlockSpec((bm, tn), lambda j, i, be, na: (i, j)),
        scratch_shapes=[pltpu.VMEM((d, tn), BF16), pltpu.VMEM((d, tn), BF16)])
    return pl.pallas_call(
        _moe_gu_body,
        grid_spec=grid_spec,
        out_shape=jax.ShapeDtypeStruct((n_rows, ff), BF16),
        compiler_params=_cp("arbitrary", "arbitrary"),
        name="moe_gu",
    )(block_e, n_act, x_rows, w_gu, w_gu, b_gu.reshape(ne, 1, ff2), b_gu.reshape(ne, 1, ff2))


def _moe_down_body(be_ref, na_ref, a_ref, w_ref, b_ref, y_ref, w16_ref):
    i = pl.program_id(1)
    prev = be_ref[jnp.maximum(i - 1, 0)]

    @pl.when((i == 0) | (be_ref[i] != prev))
    def _():
        w16_ref[...] = w_ref[...].astype(BF16)

    @pl.when(i < na_ref[0])
    def _():
        y_ref[...] = _dot(a_ref[...], w16_ref[...]) + b_ref[...]

    @pl.when(i >= na_ref[0])
    def _():
        y_ref[...] = jnp.zeros(y_ref.shape, y_ref.dtype)


def _moe_down(block_e, n_act, act, w_down, b_down):
    n_rows, ff = act.shape
    ne, _, d = w_down.shape
    bm = MOE_BM
    tn = _tile(d, 2048)
    grid_spec = pltpu.PrefetchScalarGridSpec(
        num_scalar_prefetch=2,
        grid=(d // tn, n_rows // bm),
        in_specs=[pl.BlockSpec((bm, ff), lambda j, i, be, na: (jnp.minimum(i, na[0] - 1), 0)),
                  pl.BlockSpec((None, ff, tn), lambda j, i, be, na: (be[i], 0, j)),
                  pl.BlockSpec((None, 1, tn), lambda j, i, be, na: (be[i], 0, j))],
        out_specs=pl.BlockSpec((bm, tn), lambda j, i, be, na: (i, j)),
        scratch_shapes=[pltpu.VMEM((ff, tn), BF16)])
    return pl.pallas_call(
        _moe_down_body,
        grid_spec=grid_spec,
        out_shape=jax.ShapeDtypeStruct((n_rows, d), F32),
        compiler_params=_cp("arbitrary", "arbitrary"),
        name="moe_down",
    )(block_e, n_act, act, w_down, b_down.reshape(ne, 1, d))


def _moe_combine_body(dest_ref, x1_ref, tg_ref, g2_ref, fw_ref, y_hbm, o_ref, buf_ref, sem, *, tt, topk):
    i = pl.program_id(0)
    slot = i % 2

    def gather(step, sl):
        base = step * tt * topk

        def issue(t, _):
            for kk in range(topk):
                src = y_hbm.at[pl.ds(dest_ref[base + t * topk + kk], 1)]
                pltpu.make_async_copy(src, buf_ref.at[sl, kk, pl.ds(t, 1)], sem.at[sl]).start()
            return 0

        lax.fori_loop(0, tt, issue, 0, unroll=4)

    @pl.when(i == 0)
    def _():
        gather(0, 0)

    @pl.when(i + 1 < pl.num_programs(0))
    def _():
        gather(i + 1, 1 - slot)

    for kk in range(topk):
        pltpu.make_async_copy(y_hbm.at[pl.ds(0, tt)], buf_ref.at[slot, kk], sem.at[slot]).wait()
    tg = tg_ref[...]
    y = tg[:, 0:1] * buf_ref[slot, 0]
    for kk in range(1, topk):
        y = y + tg[:, kk:kk + 1] * buf_ref[slot, kk]
    o_ref[...] = _rms(x1_ref[...] + g2_ref[...] * y, fw_ref[...])


def _moe_combine(dest_flat, x1, tg, gate2, final_w, y_rows, seq):
    t, d = x1.shape
    tt = min(seq, 128)
    per_b = seq // tt
    grid_spec = pltpu.PrefetchScalarGridSpec(
        num_scalar_prefetch=1,
        grid=(t // tt,),
        in_specs=[pl.BlockSpec((tt, d), lambda i, dest: (i, 0)),
                  pl.BlockSpec((tt, LANES), lambda i, dest: (i, 0)),
                  pl.BlockSpec((None, 1, d), lambda i, dest: (i // per_b, 0, 0)),
                  pl.BlockSpec((1, d), lambda i, dest: (0, 0)),
                  pl.BlockSpec(memory_space=pl.ANY)],
        out_specs=pl.BlockSpec((tt, d), lambda i, dest: (i, 0)),
        scratch_shapes=[pltpu.VMEM((2, TOP_K, tt, d), F32), pltpu.SemaphoreType.DMA((2,))])
    return pl.pallas_call(
        functools.partial(_moe_combine_body, tt=tt, topk=TOP_K),
        grid_spec=grid_spec,
        out_shape=jax.ShapeDtypeStruct((t, d), F32),
        compiler_params=_cp("arbitrary"),
        name="moe_combine",
    )(dest_flat, x1, tg, gate2[:, None, :], final_w.reshape(1, d), y_rows)


def _pad_lanes(v, off=0, fill=0.0):
    out = jnp.full((1, LANES), fill, F32)
    return out.at[0, off:off + v.shape[0]].set(v.astype(F32))


def _layer(x2, c, bsz, seq, p, final_w):
    d = x2.shape[1]
    qr, nh, dh = p["w_uq"].shape
    kvr = p["w_uk"].shape[0]
    nhi, di = p["w_uqi"].shape[1:]
    hd = p["gdn_norm_w"].shape[0]
    nhv = p["a_log"].shape[0]
    v_w = nhv * hd
    qk_w = (p["conv_w"].shape[1] - v_w) // 2
    nqk = qk_w // hd
    ne = p["w_router"].shape[1]

    mod = _ada(c, p["ada_w"], p["ada_b"])
    shift1, scale1, gate1, shift2, scale2, gate2 = jnp.split(mod, 6, axis=-1)

    widths = (qr, kvr, di, nhi, qk_w, qk_w, v_w, nhv, nhv, v_w, d, d)
    offs = [0]
    for wd in widths:
        offs.append(offs[-1] + wd)
    col = lambda k: p["w_in"][:, offs[k]:offs[k + 1]]
    pad = jnp.zeros((d, LANES - 2 * nhv - nhi), F32)
    w_in = jnp.concatenate([col(4), col(5), col(6), col(9), col(10), col(11),
                            col(0), col(1), col(2), col(7), col(8), col(3), pad], axis=1).astype(BF16)
    lat_w = qr + kvr + di + LANES
    gq_col = 0
    z_col = gq_col + 2 * qk_w + v_w
    ga_col = z_col + v_w
    gb_col = ga_col + d
    lat_col = gb_col + d
    assert lat_col % lat_w == 0

    proj = _inproj(x2, p["norm1_w"], scale1, shift1, w_in, p["conv_w"], seq, gq_col, qk_w, hd)

    wuq_t = p["w_uq"].reshape(qr, nh * dh).T.astype(BF16)
    wuqi_t = p["w_uqi"].reshape(qr, nhi * di).T.astype(BF16)
    wuk_r = jnp.transpose(p["w_uk"], (1, 0, 2)).astype(BF16)
    wuv_t = jnp.transpose(p["w_uv"], (1, 2, 0)).astype(BF16)
    qabs_t, qi_t, ckv, ckvt, ki, wi_t = _dsa_prep(
        proj, bsz, seq, p["q_lat_norm_w"], p["kv_lat_norm_w"], p["idx_k_norm_w"], wuq_t, wuqi_t, wuk_r,
        (qr, kvr, di, nh, dh, nhi, 2 * nhv, lat_col // lat_w))
    o_a = _dsa_attn(ki, ckv, ckvt, qi_t, qabs_t, wi_t, wuv_t).reshape(bsz * seq, nh * dh)

    gc, beta, gct = _gdn_gates(proj, _pad_lanes(p["a_log"]), _pad_lanes(p["dt_bias"]), bsz, seq,
                               (lat_col + lat_w - LANES) // LANES, nhv)
    gct4 = gct.reshape(bsz, LANES, seq // GDN_CHUNK, GDN_CHUNK)
    o_b = _gdn_core(proj, gc, beta, gct4, p["gdn_norm_w"], bsz, seq, nqk, nhv, hd, gq_col, z_col)

    merged = _merge(o_a, o_b, p["w_branch_a"].astype(BF16), p["w_branch_b"].astype(BF16), proj, ga_col, gb_col)

    wr16 = jnp.zeros((d, LANES), F32).at[:, :ne].set(p["w_router"]).astype(BF16)
    br_p = _pad_lanes(p["b_router"], fill=-1e30)
    x1, h2, te, tg = _outproj_router(merged, x2, p["w_out"].astype(BF16), gate1, p["norm2_w"], scale2, shift2,
                                     wr16, br_p, seq)

    t = bsz * seq
    pos, cnt = _moe_rank(te)
    counts = cnt[0, :ne].astype(I32)
    padded = (counts + MOE_BM - 1) // MOE_BM * MOE_BM
    pad_end = jnp.cumsum(padded)
    pad_start = pad_end - padded
    dest = (pad_start[te[:, :TOP_K]] + pos[:, :TOP_K]).reshape(-1).astype(I32)
    n_blocks = -(-(t * TOP_K) // MOE_BM) + ne
    n_act = (pad_end[-1] // MOE_BM).astype(I32).reshape(1)
    blk = jnp.minimum(jnp.arange(n_blocks, dtype=I32), n_act[0] - 1) * MOE_BM
    block_e = jnp.minimum(jnp.sum(pad_end[None, :] <= blk[:, None], axis=1), ne - 1).astype(I32)
    blk_end = (jnp.arange(n_blocks, dtype=I32) + 1) * MOE_BM
    has_pad = jnp.any((blk_end[:, None] == pad_end[None, :]) & (padded[None, :] > 0), axis=1)
    fill_flags = (has_pad | (blk_end > pad_end[-1])).astype(I32)
    x_rows = _moe_dispatch(dest, fill_flags, h2, n_blocks * MOE_BM)
    act = _moe_gu(block_e, n_act, x_rows, p["w_gu"], p["b_gu"])
    y_rows = _moe_down(block_e, n_act, act, p["w_down"], p["b_down"])
    return _moe_combine(dest, x1, tg, gate2, final_w, y_rows, seq)


@jax.jit
def kernel(x, c, ada_w, ada_b, norm1_w, w_in, q_lat_norm_w, kv_lat_norm_w, idx_k_norm_w, w_uq, w_uqi, w_uk, w_uv, conv_w, a_log, dt_bias, gdn_norm_w, w_branch_a, w_branch_b, w_out, norm2_w, w_router, b_router, w_gu, b_gu, w_down, b_down, final_norm_w):
    bsz, seq, d = x.shape
    stacked = dict(ada_w=ada_w, ada_b=ada_b, norm1_w=norm1_w, w_in=w_in, q_lat_norm_w=q_lat_norm_w,
                   kv_lat_norm_w=kv_lat_norm_w, idx_k_norm_w=idx_k_norm_w, w_uq=w_uq, w_uqi=w_uqi, w_uk=w_uk,
                   w_uv=w_uv, conv_w=conv_w, a_log=a_log, dt_bias=dt_bias, gdn_norm_w=gdn_norm_w,
                   w_branch_a=w_branch_a, w_branch_b=w_branch_b, w_out=w_out, norm2_w=norm2_w,
                   w_router=w_router, b_router=b_router, w_gu=w_gu, b_gu=b_gu, w_down=w_down, b_down=b_down)
    depth = ada_w.shape[0]
    assert depth == 1, "the final norm is fused into the last layer's combine kernel"
    x2 = x.reshape(bsz * seq, d)
    p = {k: v[0] for k, v in stacked.items()}
    out = _layer(x2, c, bsz, seq, p, final_norm_w)
    return out.reshape(bsz, seq, d)
```

```python
import functools

import jax
import jax.numpy as jnp
from jax import lax
from jax.experimental import pallas as pl
from jax.experimental.pallas import tpu as pltpu

F32 = jnp.float32
BF16 = jnp.bfloat16
I32 = jnp.int32

EPS = 1e-6
LANES = 128
VMEM_LIMIT = 56 * 1024 * 1024

IDX_TOPK = 256
GDN_CHUNK = 64
TOP_K = 4
SWIGLU_LIMIT = 7.0
SWIGLU_ALPHA = 1.702
MOE_BM = 512
ATTN_TILES = 2
ONES_ROWS = 16
LOG2E = 1.4426950408889634
INT_MIN = -(2 ** 31)
INT_MAX = 2 ** 31 - 1


def _cp(*sem):
    return pltpu.CompilerParams(dimension_semantics=sem, vmem_limit_bytes=VMEM_LIMIT)


def _tile(n, pref):
    if n <= pref:
        return n
    t = pref - pref % LANES
    while n % t:
        t -= LANES
    return t


def _rms(x, w):
    return x * lax.rsqrt(jnp.mean(x * x, axis=-1, keepdims=True) + EPS) * w


def _sigmoid(x):
    return 0.5 * jnp.tanh(0.5 * x) + 0.5


def _dot(a, b):
    return jnp.dot(a, b, preferred_element_type=F32)


def _dot_nt(a, b):
    return lax.dot_general(a, b, (((1,), (1,)), ((), ())), preferred_element_type=F32)


def _dot_tn(a, b):
    return lax.dot_general(a, b, (((0,), (0,)), ((), ())), preferred_element_type=F32)


def _ada_body(ct_ref, w_ref, b_ref, o_ref, *, nb):
    ct = ct_ref[...]
    s = ct * jax.nn.sigmoid(ct)
    w = w_ref[...]
    for b in range(nb):
        o_ref[b:b + 1, :] = jnp.sum(w * s[:, b:b + 1], axis=0, keepdims=True) + b_ref[...]


def _ada(c, w, b):
    nb, d = c.shape
    n = w.shape[1]
    tn = _tile(n, 1024)
    return pl.pallas_call(
        functools.partial(_ada_body, nb=nb),
        grid=(n // tn,),
        in_specs=[pl.BlockSpec((d, nb), lambda j: (0, 0)),
                  pl.BlockSpec((d, tn), lambda j: (0, j)),
                  pl.BlockSpec((1, tn), lambda j: (0, j))],
        out_specs=pl.BlockSpec((nb, tn), lambda j: (0, j)),
        out_shape=jax.ShapeDtypeStruct((nb, n), F32),
        compiler_params=_cp("parallel"),
        name="ada",
    )(c.T, w, b.reshape(1, n))


def _inproj_body(x_ref, nw_ref, sc_ref, sh_ref, w_ref, cw_ref, o_ref, h_ref, carry_ref, *ubuf_refs,
                 tm, tn, sw, rc, hd, per_b, c0, n_qk, n_conv, q_scale):
    i = pl.program_id(0)
    j = pl.program_id(1)

    @pl.when(j == 0)
    def _():
        h = _rms(x_ref[...], nw_ref[...]) * (1.0 + sc_ref[...]) + sh_ref[...]
        h_ref[...] = h.astype(BF16)

    cj = j - c0
    in_conv = (cj >= 0) & (cj < n_conv)

    @pl.when(jnp.logical_not(in_conv))
    def _():
        o_ref[...] = _dot(h_ref[...], w_ref[...])

    @pl.when(in_conv)
    def _():
        taps = cw_ref.shape[0]

        @pl.when(i % per_b == 0)
        def _():
            carry_ref[cj] = jnp.zeros((8, tn), F32)

        prev = carry_ref[cj]
        scale = jnp.where(cj < n_qk, q_scale, 1.0)

        def conv_tile(normalise):
            tails = []
            for s, ubuf_ref in enumerate(ubuf_refs):
                cols = slice(s * sw, (s + 1) * sw)
                w = cw_ref[:, cols]
                u = _dot(h_ref[...], w_ref[:, cols])
                ubuf_ref[0:8, :] = prev[:, cols]
                ubuf_ref[8:, :] = u
                tails.append(u[tm - 8:, :])
                for r in range(tm // rc):
                    ext = ubuf_ref[r * rc:(r + 1) * rc + 8, :]
                    if taps == 4:
                        ext1 = pltpu.roll(ext, 1, axis=0)
                        near = ext * w[3:4, :] + ext1 * w[2:3, :]
                        far = ext * w[1:2, :] + ext1 * w[0:1, :]
                        y = near[8:] + pltpu.roll(far, 2, axis=0)[8:]
                    else:
                        y = ext[8:] * w[taps - 1:taps, :]
                        for k in range(1, taps):
                            y = y + pltpu.roll(ext, k, axis=0)[8:] * w[taps - 1 - k:taps - k, :]
                    y = y * _sigmoid(y)
                    if not normalise:
                        o_ref[r * rc:(r + 1) * rc, cols] = y
                        continue
                    for hh in range(sw // hd):
                        x = y[:, hh * hd:(hh + 1) * hd]
                        inv = lax.rsqrt(jnp.sum(x * x, axis=-1, keepdims=True) + EPS) * scale
                        c_lo = s * sw + hh * hd
                        o_ref[r * rc:(r + 1) * rc, c_lo:c_lo + hd] = x * inv
            carry_ref[cj] = jnp.concatenate(tails, axis=1)

        @pl.when(cj < 2 * n_qk)
        def _():
            conv_tile(True)

        @pl.when(cj >= 2 * n_qk)
        def _():
            conv_tile(False)


def _inproj(x2, nw, scale, shift, w16, conv_w, seq, conv_col0, qk_w, hd):
    t, d = x2.shape
    n = w16.shape[1]
    tm = min(seq, 1024)
    tn = _tile(n, 1024)
    per_b = seq // tm
    conv_ch = conv_w.shape[1]
    assert conv_col0 % tn == 0 and qk_w % tn == 0 and conv_ch % tn == 0 and tn % hd == 0
    c0, n_qk, n_conv = conv_col0 // tn, qk_w // tn, conv_ch // tn
    rc = min(tm, 64)
    sw = min(tn, 256)
    body = functools.partial(_inproj_body, tm=tm, tn=tn, sw=sw, rc=rc, hd=hd, per_b=per_b, c0=c0, n_qk=n_qk,
                             n_conv=n_conv, q_scale=hd ** -0.5)
    return pl.pallas_call(
        body,
        grid=(t // tm, n // tn),
        in_specs=[pl.BlockSpec((tm, d), lambda i, j: (i, 0)),
                  pl.BlockSpec((1, d), lambda i, j: (0, 0)),
                  pl.BlockSpec((None, 1, d), lambda i, j: (i // per_b, 0, 0)),
                  pl.BlockSpec((None, 1, d), lambda i, j: (i // per_b, 0, 0)),
                  pl.BlockSpec((d, tn), lambda i, j: (0, j)),
                  pl.BlockSpec((conv_w.shape[0], tn), lambda i, j: (0, jnp.clip(j - c0, 0, n_conv - 1)))],
        out_specs=pl.BlockSpec((tm, tn), lambda i, j: (i, j)),
        out_shape=jax.ShapeDtypeStruct((t, n), F32),
        scratch_shapes=[pltpu.VMEM((tm, d), BF16), pltpu.VMEM((n_conv, 8, tn), F32)]
        + [pltpu.VMEM((tm + 8, sw), F32) for _ in range(tn // sw)],
        compiler_params=_cp("arbitrary", "arbitrary"),
        name="inproj",
    )(x2, nw.reshape(1, d), scale[:, None, :], shift[:, None, :], w16, conv_w)


def _dsa_prep_body(lat_ref, qnw_ref, kvnw_ref, knw_ref, wuq_ref, wuqi_ref, wuk_ref,
                   qabs_ref, qi_ref, ckv_ref, ckvt_ref, ki_ref, wi_ref,
                   *, qr, kvr, di, nh, dh, nhi, q_scale, i_scale, w_scale, wi_off):
    lat = lat_ref[...]
    cq_t = _rms(lat[:, :qr], qnw_ref[...]).T.astype(BF16)
    q_t = _dot(wuq_ref[...], cq_t)
    for h in range(nh):
        q_h = q_t[h * dh:(h + 1) * dh].astype(BF16)
        qabs_ref[h] = (_dot(wuk_ref[h], q_h) * q_scale).astype(BF16)
    qi_t = _dot(wuqi_ref[...], cq_t) * i_scale
    for h in range(nhi):
        qi_ref[h] = qi_t[h * di:(h + 1) * di].astype(BF16)
    ckv = _rms(lat[:, qr:qr + kvr], kvnw_ref[...])
    ckv_ref[...] = ckv.astype(BF16)
    ckvt_ref[0:kvr, :] = ckv.T.astype(BF16)
    ckvt_ref[kvr:, :] = jnp.ones((ONES_ROWS, ckvt_ref.shape[1]), BF16)
    ki_ref[...] = _rms(lat[:, qr + kvr:qr + kvr + di], knw_ref[...]).astype(BF16)
    misc_t = lat[:, qr + kvr + di:].T
    wi_ref[...] = misc_t[wi_off:wi_off + nhi] * w_scale


def _dsa_prep(proj, bsz, seq, qnw, kvnw, knw, wuq_t, wuqi_t, wuk_r, dims):
    qr, kvr, di, nh, dh, nhi, wi_off, lat_blk = dims
    tm = min(seq, 512)
    ns = seq // tm
    lat_w = qr + kvr + di + LANES
    body = functools.partial(
        _dsa_prep_body, qr=qr, kvr=kvr, di=di, nh=nh, dh=dh, nhi=nhi,
        q_scale=dh ** -0.5 * LOG2E, i_scale=di ** -0.5, w_scale=nhi ** -0.5, wi_off=wi_off)
    full = lambda shape: pl.BlockSpec(shape, lambda b, s: (0,) * len(shape))
    return pl.pallas_call(
        body,
        grid=(bsz, ns),
        in_specs=[pl.BlockSpec((tm, lat_w), lambda b, s: (b * ns + s, lat_blk)),
                  full((1, qr)), full((1, kvr)), full((1, di)),
                  full((nh * dh, qr)), full((nhi * di, qr)), full((nh, kvr, dh))],
        out_specs=[pl.BlockSpec((None, nh, kvr, tm), lambda b, s: (b, 0, 0, s)),
                   pl.BlockSpec((None, nhi, di, tm), lambda b, s: (b, 0, 0, s)),
                   pl.BlockSpec((None, tm, kvr), lambda b, s: (b, s, 0)),
                   pl.BlockSpec((None, kvr + ONES_ROWS, tm), lambda b, s: (b, 0, s)),
                   pl.BlockSpec((None, tm, di), lambda b, s: (b, s, 0)),
                   pl.BlockSpec((None, nhi, tm), lambda b, s: (b, 0, s))],
        out_shape=[jax.ShapeDtypeStruct((bsz, nh, kvr, seq), BF16),
                   jax.ShapeDtypeStruct((bsz, nhi, di, seq), BF16),
                   jax.ShapeDtypeStruct((bsz, seq, kvr), BF16),
                   jax.ShapeDtypeStruct((bsz, kvr + ONES_ROWS, seq), BF16),
                   jax.ShapeDtypeStruct((bsz, seq, di), BF16),
                   jax.ShapeDtypeStruct((bsz, nhi, seq), F32)],
        compiler_params=_cp("parallel", "parallel"),
        name="dsa_prep",
    )(proj, qnw.reshape(1, qr), kvnw.reshape(1, kvr), knw.reshape(1, di), wuq_t, wuqi_t, wuk_r)


def _dsa_attn_body(ki_ref, ckv_ref, ckvt_ref, qi_ref, qabs_ref, wi_ref, wuv_ref, o_ref,
                   keys_ref, k16_ref, acc_ref, m_ref, ot_ref, thr_ref, jlim_ref,
                   *, tq, tk, at, nh, nhi, dh, kvr, topk, seq_bits):
    qb = pl.program_id(1)
    nk = ((qb + 1) * tq + tk - 1) // tk
    q_pos = qb * tq + lax.broadcasted_iota(I32, (tk, tq), 1)
    k_iota = lax.broadcasted_iota(I32, (tk, tq), 0)

    def score_tile(kt, _):
        ki_t = ki_ref[pl.ds(pl.multiple_of(kt * tk, tk), tk), :]
        acc = jnp.zeros((tk, tq), F32)
        for h in range(nhi):
            rel = _dot(ki_t, qi_ref[h])
            acc = acc + wi_ref[h:h + 1, :] * jnp.maximum(rel, 0.0)
        acc = jnp.where(acc == 0.0, 0.0, acc)
        bits = lax.bitcast_convert_type(acc, I32)
        causal = kt * tk + k_iota <= q_pos
        key = jnp.where(bits < 0, bits ^ INT_MAX, bits)
        keys_ref[pl.ds(pl.multiple_of(kt * tk, tk), tk), :] = jnp.where(causal, key, INT_MIN)
        top = lax.bitcast_convert_type(bits & -65536, F32)
        k16_ref[pl.ds(pl.multiple_of(kt * tk, tk), tk), :] = jnp.where(causal, top, jnp.nan).astype(BF16)
        return 0

    lax.fori_loop(0, nk, score_tile, 0)

    one16 = jnp.ones((tk, tq), BF16)
    zero16 = jnp.zeros((tk, tq), BF16)

    def count16(cand_bf):
        cand_b = jnp.broadcast_to(cand_bf, (tk, tq))

        def tile(kt, c):
            k_t = k16_ref[pl.ds(pl.multiple_of(kt * tk, tk), tk), :]
            hit = jnp.where(k_t >= cand_b, one16, zero16)
            part = hit[0:16]
            for r in range(1, tk // 16):
                part = part + hit[r * 16:(r + 1) * 16]
            return c + part.astype(F32)
        c16 = lax.fori_loop(0, nk, tile, jnp.zeros((16, tq), F32))
        return jnp.sum(c16, axis=0, keepdims=True)

    def hi_step(it, hi):
        cand = hi + lax.shift_left(jnp.int32(1), 15 - it)
        pattern = jnp.where(cand >= 0, cand, cand ^ 0x7FFF) & 0xFFFF
        cand_bf = lax.bitcast_convert_type(lax.shift_left(pattern, 16), F32).astype(BF16)
        return jnp.where(count16(cand_bf) >= topk, cand, hi)

    hi16 = lax.fori_loop(0, 16, hi_step, jnp.full((1, tq), -(2 ** 15), I32))

    def count(pred_fn):
        def tile(kt, c):
            k_t = keys_ref[pl.ds(pl.multiple_of(kt * tk, tk), tk), :]
            hit = jnp.where(pred_fn(k_t, kt), 1, 0).astype(I32)
            return c + jnp.sum(hit.reshape(tk // 8, 8, tq), axis=0)
        c8 = lax.fori_loop(0, nk, tile, jnp.zeros((8, tq), I32))
        return jnp.sum(c8, axis=0, keepdims=True)

    def bit_cond(carry):
        it, _, settled = carry
        return (it < 32) & (jnp.min(settled) == 0)

    def bit_step(carry):
        it, thr, settled = carry
        for _ in range(2):
            cand = thr ^ lax.shift_left(jnp.int32(1), 31 - it)
            cnt = count(lambda k_t, kt: k_t >= cand)
            take = (cnt >= topk) & (settled == 0)
            settled = jnp.where(take & (cnt == topk), 1, settled)
            thr = jnp.where(take, cand, thr)
            it = it + 1
        return it, thr, settled

    _, thr, _ = lax.while_loop(
        bit_cond, bit_step, (jnp.int32(16), lax.shift_left(hi16, 16), jnp.zeros((1, tq), I32)))
    thr = jnp.maximum(thr, INT_MIN + 1)
    cnt_ge = count(lambda k_t, kt: k_t >= thr)
    thr_ref[...] = thr
    jlim_ref[...] = jnp.full((1, tq), INT_MAX, I32)

    @pl.when(jnp.max(cnt_ge) > topk)
    def _():
        need = topk - count(lambda k_t, kt: k_t > thr)

        def pos_step(it, p):
            cand = p + lax.shift_left(jnp.int32(1), seq_bits - 1 - it)
            cnt = count(lambda k_t, kt: (k_t == thr) & (kt * tk + k_iota < cand))
            return jnp.where(cnt < need, cand, p)

        jlim_ref[...] = lax.fori_loop(0, seq_bits, pos_step, jnp.zeros((1, tq), I32))

    m_ref[...] = jnp.full(m_ref.shape, -jnp.inf, F32)
    acc_ref[...] = jnp.zeros(acc_ref.shape, F32)

    na = (nk + at - 1) // at

    def blank(kt, _):
        keys_ref[pl.ds(pl.multiple_of(kt * tk, tk), tk), :] = jnp.full((tk, tq), INT_MIN, I32)
        return 0

    lax.fori_loop(nk, na * at, blank, 0)
    ta = at * tk
    a_iota = lax.broadcasted_iota(I32, (ta, tq), 0)

    def attn_tile(kt, _):
        off = pl.multiple_of(kt * ta, ta)
        k_t = keys_ref[pl.ds(off, ta), :]
        thr_b = thr_ref[...]
        mask = (k_t > thr_b) | ((k_t == thr_b) & (kt * ta + a_iota <= jlim_ref[...]))
        bias = jnp.where(mask, 0.0, -jnp.inf)
        ckv_t = ckv_ref[pl.ds(off, ta), :]
        ckvt_t = ckvt_ref[:, pl.ds(off, ta)]
        logits = _dot(ckv_t, qabs_ref[0])
        for h in range(nh):
            s = logits + bias
            if h + 1 < nh:
                logits = _dot(ckv_t, qabs_ref[h + 1])
            m_old = m_ref[h]
            m_new = jnp.maximum(m_old, jnp.max(s, axis=0, keepdims=True))
            m_use = jnp.where(m_new == -jnp.inf, 0.0, m_new)
            p = jnp.exp2(s - m_use)
            alpha = jnp.exp2(m_old - m_use)
            acc_ref[h] = alpha * acc_ref[h] + _dot(ckvt_t, p.astype(BF16))
            m_ref[h] = m_new
        return 0

    lax.fori_loop(0, na, attn_tile, 0)

    for h in range(nh):
        acc = acc_ref[h]
        o_lat = (acc[:kvr] * (1.0 / acc[kvr:kvr + 1])).astype(BF16)
        ot_ref[h * dh:(h + 1) * dh, :] = _dot(wuv_ref[h], o_lat)
    o_ref[...] = ot_ref[...].T.astype(BF16)


def _dsa_attn(ki, ckv, ckvt, qi_t, qabs_t, wi_t, wuv_t):
    bsz, seq, di = ki.shape
    kvr = ckv.shape[2]
    nh, dh = wuv_t.shape[0], wuv_t.shape[1]
    nhi = qi_t.shape[1]
    tq = min(seq // 2, 256)
    tk = tq
    at = min(ATTN_TILES, seq // tk)
    assert seq % (at * tk) == 0
    topk = min(IDX_TOPK, seq // 4)
    body = functools.partial(_dsa_attn_body, tq=tq, tk=tk, at=at, nh=nh, nhi=nhi, dh=dh, kvr=kvr, topk=topk,
                             seq_bits=max(1, (seq - 1).bit_length()))
    return pl.pallas_call(
        body,
        grid=(bsz, seq // tq),
        in_specs=[pl.BlockSpec((None, seq, di), lambda b, q: (b, 0, 0)),
                  pl.BlockSpec((None, seq, kvr), lambda b, q: (b, 0, 0)),
                  pl.BlockSpec((None, kvr + ONES_ROWS, seq), lambda b, q: (b, 0, 0)),
                  pl.BlockSpec((None, nhi, di, tq), lambda b, q: (b, 0, 0, q)),
                  pl.BlockSpec((None, nh, kvr, tq), lambda b, q: (b, 0, 0, q)),
                  pl.BlockSpec((None, nhi, tq), lambda b, q: (b, 0, q)),
                  pl.BlockSpec((nh, dh, kvr), lambda b, q: (0, 0, 0))],
        out_specs=pl.BlockSpec((None, tq, nh * dh), lambda b, q: (b, q, 0)),
        out_shape=jax.ShapeDtypeStruct((bsz, seq, nh * dh), BF16),
        scratch_shapes=[pltpu.VMEM((seq, tq), I32),
                        pltpu.VMEM((seq, tq), BF16),
                        pltpu.VMEM((nh, kvr + ONES_ROWS, tq), F32),
                        pltpu.VMEM((nh, 1, tq), F32),
                        pltpu.VMEM((nh * dh, tq), F32),
                        pltpu.VMEM((1, tq), I32),
                        pltpu.VMEM((1, tq), I32)],
        compiler_params=_cp("parallel", "parallel"),
        name="dsa_attn",
    )(ki, ckv, ckvt, qi_t, qabs_t, wi_t, wuv_t)


GDN_GROUP = 16
GDN_INV_BATCH = 64


def _gdn_gate_body(m_ref, alog_ref, dtb_ref, gc_ref, beta_ref, gct_ref, *, ts, chunk, nhv, ng):
    x = m_ref[...]
    z = x + dtb_ref[...]
    softplus = jnp.maximum(z, 0.0) + jnp.log(1.0 + jnp.exp(-jnp.abs(z)))
    g = -jnp.exp(alog_ref[...]) * softplus
    pos = lax.broadcasted_iota(I32, (ts, LANES), 0) % chunk
    d = 1
    while d < chunk:
        g = g + jnp.where(pos >= d, pltpu.roll(g, d, axis=0), 0.0)
        d *= 2
    gct_ref[...] = g.T
    beta = jax.nn.sigmoid(x)
    for j in range(nhv // ng):
        gc_ref[:, j * LANES:(j + 1) * LANES] = pltpu.roll(g, (LANES - j * ng) % LANES, axis=1)
        beta_ref[:, j * LANES:(j + 1) * LANES] = pltpu.roll(beta, (2 * LANES - nhv - j * ng) % LANES, axis=1)


def _gdn_gates(proj, alog_p, dtb_p, bsz, seq, misc_blk, nhv):
    ts = min(seq, 512)
    ns = seq // ts
    ng = GDN_GROUP
    gw = nhv // ng * LANES
    body = functools.partial(_gdn_gate_body, ts=ts, chunk=GDN_CHUNK, nhv=nhv, ng=ng)
    return pl.pallas_call(
        body,
        grid=(bsz, ns),
        in_specs=[pl.BlockSpec((ts, LANES), lambda b, s: (b * ns + s, misc_blk)),
                  pl.BlockSpec((1, LANES), lambda b, s: (0, 0)),
                  pl.BlockSpec((1, LANES), lambda b, s: (0, 0))],
        out_specs=[pl.BlockSpec((ts, gw), lambda b, s: (b * ns + s, 0)),
                   pl.BlockSpec((ts, gw), lambda b, s: (b * ns + s, 0)),
                   pl.BlockSpec((None, LANES, ts), lambda b, s: (b, 0, s))],
        out_shape=[jax.ShapeDtypeStruct((bsz * seq, gw), F32),
                   jax.ShapeDtypeStruct((bsz * seq, gw), F32),
                   jax.ShapeDtypeStruct((bsz, LANES, seq), F32)],
        compiler_params=_cp("parallel", "parallel"),
        name="gdn_gates",
    )(proj, alog_p, dtb_p)


def _gdn_core_body(q_ref, k_ref, v_ref, gc_ref, beta_ref, gr_ref, z_ref, nw_ref, o_ref, *scratch,
                   ts, chunk, hd, ng, rep, gr_steps):
    st_refs = scratch[:ng]
    gcb_ref, bb_ref, a_ref, u_ref, wq_ref, qkkt_ref, rhs_ref, egl_ref = scratch[ng:]
    nc = ts // chunk
    gr_base = (pl.program_id(2) % gr_steps) * nc

    @pl.when(pl.program_id(2) == 0)
    def _():
        for st_ref in st_refs:
            st_ref[...] = jnp.zeros(st_ref.shape, F32)

    for g in range(ng):
        gcb_ref[g] = jnp.broadcast_to(gc_ref[:, g:g + 1], (ts, LANES))
        bb_ref[g] = jnp.broadcast_to(beta_ref[:, g:g + 1], (ts, LANES))

    ri = lax.broadcasted_iota(I32, (chunk, chunk), 0)
    ci = lax.broadcasted_iota(I32, (chunk, chunk), 1)
    incl = ri >= ci
    strict = ri > ci
    eye = jnp.where(ri == ci, 1.0, 0.0).astype(F32)

    def prep(c, _):
        rows = pl.ds(pl.multiple_of(c * chunk, chunk), chunk)
        for qh in range(ng // rep):
            q = q_ref[rows, qh * hd:(qh + 1) * hd]
            k = k_ref[rows, qh * hd:(qh + 1) * hd]
            k16 = k.astype(BF16)
            kk = _dot_nt(k16, k16)
            qk = _dot_nt(q.astype(BF16), k16)
            a_pair = []
            for vh in range(rep):
                g = qh * rep + vh
                n = c * ng + g
                gcb = gcb_ref[g, rows, :]
                bb = bb_ref[g, rows, :]
                g_row = gr_ref[g, pl.ds(gr_base + c, 1), :]
                decay = jnp.where(incl, jnp.exp(gcb[:, :chunk] - g_row), 0.0)
                a_pair.append(jnp.where(strict, kk * bb[:, :chunk] * decay, 0.0))
                qkkt_ref[n, 0:chunk, :] = (qk * decay).astype(BF16)
                eg = jnp.exp(gcb)
                rhs_ref[n, :, 0:hd] = (v_ref[rows, g * hd:(g + 1) * hd] * bb).astype(BF16)
                rhs_ref[n, :, hd:2 * hd] = (k * bb * eg).astype(BF16)
                wq_ref[n, chunk:2 * chunk, :] = (q * eg).astype(BF16)
                g_last = gcb[chunk - 1:chunk, :]
                qkkt_ref[n, chunk:chunk + hd, :] = (k * jnp.exp(g_last - gcb)).T.astype(BF16)
                egl_ref[n] = jnp.broadcast_to(jnp.exp(g_last), (8, LANES))
            a_ref[c * (ng // rep) + qh] = jnp.concatenate(a_pair, axis=1)
        return 0

    lax.fori_loop(0, nc, prep, 0)

    n_dbl = max(1, (chunk - 1).bit_length()) - 1

    lane2 = lax.broadcasted_iota(I32, (chunk, 2 * chunk), 1)
    left = lane2 < chunk
    eye2 = jnp.where(lax.broadcasted_iota(I32, (chunk, 2 * chunk), 0) == lane2 % chunk, 1.0, 0.0).astype(F32)

    def blockdiag(x):
        return jnp.concatenate([jnp.where(left, x, 0.0), jnp.where(left, 0.0, x)], axis=0).astype(BF16)

    def invert(ib, _):
        ms_ = [ib * (GDN_INV_BATCH // 2) + j for j in range(GDN_INV_BATCH // 2)]
        pw = [a_ref[m] for m in ms_]
        tm = [eye2 - a for a in pw]
        pw = [_dot(a.astype(BF16), blockdiag(a)) for a in pw]
        for lvl in range(n_dbl):
            last = lvl == n_dbl - 1
            lhs = [(t if last else jnp.concatenate([t, x], axis=0)).astype(BF16) for t, x in zip(tm, pw)]
            prod = [_dot(y, blockdiag(x)) for y, x in zip(lhs, pw)]
            tm = [t + y[:chunk] for t, y in zip(tm, prod)]
            pw = [y[chunk:] for y in prod]
        for m, t in zip(ms_, tm):
            zero = jnp.zeros((chunk, 2 * hd), BF16)
            rhs2 = jnp.concatenate([jnp.concatenate([rhs_ref[2 * m], zero], axis=1),
                                    jnp.concatenate([zero, rhs_ref[2 * m + 1]], axis=1)], axis=0)
            uw = _dot(t.astype(BF16), rhs2)
            for j in range(2):
                u_ref[2 * m + j] = uw[:, 2 * j * hd:(2 * j + 1) * hd]
                wq_ref[2 * m + j, 0:chunk, :] = uw[:, (2 * j + 1) * hd:(2 * j + 2) * hd].astype(BF16)
        return 0

    lax.fori_loop(0, nc * ng // GDN_INV_BATCH, invert, 0)

    nw = nw_ref[...]

    def step(c, _):
        rows = pl.ds(pl.multiple_of(c * chunk, chunk), chunk)
        heads = range(ng)
        ns_ = [c * ng + g for g in heads]
        state = [st_refs[g][...] for g in heads]
        st16 = [x.astype(BF16) for x in state]
        wq = [_dot(wq_ref[n], x) for n, x in zip(ns_, st16)]
        vn16 = [(u_ref[n] - x[:chunk]).astype(BF16) for n, x in zip(ns_, wq)]
        ok = [_dot(qkkt_ref[n], x) for n, x in zip(ns_, vn16)]
        o = [x[chunk:] + y[:chunk] for x, y in zip(wq, ok)]
        ds = [y[chunk:] for y in ok]
        for g in heads:
            st_refs[g][...] = state[g] * egl_ref[ns_[g]][0:1, :] + ds[g]
            zz = z_ref[rows, g * hd:(g + 1) * hd]
            o_ref[rows, g * hd:(g + 1) * hd] = (_rms(o[g], nw) * (zz * _sigmoid(zz))).astype(o_ref.dtype)
        return 0

    lax.fori_loop(0, nc, step, 0)


def _gdn_core(proj, gc, beta, gct4, norm_w, bsz, seq, nqk, nhv, hd, q_col, z_col):
    assert hd == LANES
    chunk = GDN_CHUNK
    ng = GDN_GROUP
    rep = nhv // nqk
    ts = min(seq, 256)
    ns = seq // ts
    n = (ts // chunk) * ng
    assert rep == 2 and ng % rep == 0 and nhv % ng == 0 and n % GDN_INV_BATCH == 0
    qw, vw = (ng // rep) * hd, ng * hd
    k_col, v_col = q_col + nqk * hd, q_col + 2 * nqk * hd
    assert q_col % qw == 0 and k_col % qw == 0 and v_col % vw == 0 and z_col % vw == 0
    gr_steps = max(1, 8 * chunk // ts)
    assert ns % gr_steps == 0
    body = functools.partial(_gdn_core_body, ts=ts, chunk=chunk, hd=hd, ng=ng, rep=rep, gr_steps=gr_steps)
    row = lambda b, h, s: b * ns + s
    return pl.pallas_call(
        body,
        grid=(bsz, nhv // ng, ns),
        in_specs=[pl.BlockSpec((ts, qw), lambda b, h, s: (row(b, h, s), q_col // qw + h)),
                  pl.BlockSpec((ts, qw), lambda b, h, s: (row(b, h, s), k_col // qw + h)),
                  pl.BlockSpec((ts, vw), lambda b, h, s: (row(b, h, s), v_col // vw + h)),
                  pl.BlockSpec((ts, LANES), lambda b, h, s: (row(b, h, s), h)),
                  pl.BlockSpec((ts, LANES), lambda b, h, s: (row(b, h, s), h)),
                  pl.BlockSpec((None, ng, gr_steps * ts // chunk, chunk), lambda b, h, s: (b, h, s // gr_steps, 0)),
                  pl.BlockSpec((ts, vw), lambda b, h, s: (row(b, h, s), z_col // vw + h)),
                  pl.BlockSpec((1, hd), lambda b, h, s: (0, 0))],
        out_specs=pl.BlockSpec((ts, vw), lambda b, h, s: (row(b, h, s), h)),
        out_shape=jax.ShapeDtypeStruct((bsz * seq, nhv * hd), BF16),
        scratch_shapes=[pltpu.VMEM((hd, hd), F32) for _ in range(ng)] + [
                        pltpu.VMEM((ng, ts, LANES), F32),
                        pltpu.VMEM((ng, ts, LANES), F32),
                        pltpu.VMEM((n // 2, chunk, 2 * chunk), F32),
                        pltpu.VMEM((n, chunk, hd), F32),
                        pltpu.VMEM((n, 2 * chunk, hd), BF16),
                        pltpu.VMEM((n, chunk + hd, chunk), BF16),
                        pltpu.VMEM((n, chunk, 2 * hd), BF16),
                        pltpu.VMEM((n, 8, LANES), F32)],
        compiler_params=_cp("parallel", "parallel", "arbitrary"),
        name="gdn_core",
    )(proj, proj, proj, gc, beta, gct4, proj, norm_w.reshape(1, hd))


def _merge_body(a_ref, b_ref, wa_ref, wb_ref, ga_ref, gb_ref, o_ref):
    ya = _dot(a_ref[...], wa_ref[...])
    yb = _dot(b_ref[...], wb_ref[...])
    o_ref[...] = (_sigmoid(ga_ref[...]) * ya + _sigmoid(gb_ref[...]) * yb).astype(o_ref.dtype)


def _merge(o_a, o_b, wa16, wb16, proj, ga_col0, gb_col0):
    t, ka = o_a.shape
    kb = o_b.shape[1]
    d = wa16.shape[1]
    tm = min(t, 1024)
    tn = _tile(d, 512)
    ga0, gb0 = ga_col0 // tn, gb_col0 // tn
    return pl.pallas_call(
        _merge_body,
        grid=(t // tm, d // tn),
        in_specs=[pl.BlockSpec((tm, ka), lambda i, j: (i, 0)),
                  pl.BlockSpec((tm, kb), lambda i, j: (i, 0)),
                  pl.BlockSpec((ka, tn), lambda i, j: (0, j)),
                  pl.BlockSpec((kb, tn), lambda i, j: (0, j)),
                  pl.BlockSpec((tm, tn), lambda i, j: (i, ga0 + j)),
                  pl.BlockSpec((tm, tn), lambda i, j: (i, gb0 + j))],
        out_specs=pl.BlockSpec((tm, tn), lambda i, j: (i, j)),
        out_shape=jax.ShapeDtypeStruct((t, d), BF16),
        compiler_params=_cp("parallel", "parallel"),
        name="merge",
    )(o_a, o_b, wa16, wb16, proj, proj)


def _outproj_router_body(m_ref, x_ref, wo_ref, g1_ref, nw_ref, sc_ref, sh_ref, wr_ref, br_ref,
                         x1_ref, h2_ref, te_ref, tg_ref, *, topk):
    x1 = x_ref[...] + g1_ref[...] * _dot(m_ref[...], wo_ref[...])
    x1_ref[...] = x1
    h2 = _rms(x1, nw_ref[...]) * (1.0 + sc_ref[...]) + sh_ref[...]
    h2_ref[...] = h2
    logits = _dot(h2.astype(BF16), wr_ref[...]) + br_ref[...]
    lane = lax.broadcasted_iota(I32, logits.shape, 1)
    te = jnp.zeros(logits.shape, I32)
    tg = jnp.zeros(logits.shape, F32)
    denom = jnp.zeros((logits.shape[0], 1), F32)
    m0 = None
    for kk in range(topk):
        mx = jnp.max(logits, axis=-1, keepdims=True)
        idx = jnp.min(jnp.where(logits == mx, lane, LANES), axis=-1, keepdims=True)
        if kk == 0:
            m0 = mx
        e = jnp.exp(mx - m0)
        denom = denom + e
        te = jnp.where(lane == kk, idx, te)
        tg = jnp.where(lane == kk, e, tg)
        logits = jnp.where(lane == idx, -jnp.inf, logits)
    te_ref[...] = te
    tg_ref[...] = tg / denom


def _outproj_router(merged, x2, wo16, gate1, nw, scale, shift, wr16, br_p, seq):
    t, d = x2.shape
    tm = min(seq, 512)
    per_b = seq // tm
    vec = lambda: pl.BlockSpec((None, 1, d), lambda i: (i // per_b, 0, 0))
    return pl.pallas_call(
        functools.partial(_outproj_router_body, topk=TOP_K),
        grid=(t // tm,),
        in_specs=[pl.BlockSpec((tm, d), lambda i: (i, 0)),
                  pl.BlockSpec((tm, d), lambda i: (i, 0)),
                  pl.BlockSpec((d, d), lambda i: (0, 0)),
                  vec(),
                  pl.BlockSpec((1, d), lambda i: (0, 0)),
                  vec(), vec(),
                  pl.BlockSpec((d, LANES), lambda i: (0, 0)),
                  pl.BlockSpec((1, LANES), lambda i: (0, 0))],
        out_specs=[pl.BlockSpec((tm, d), lambda i: (i, 0)),
                   pl.BlockSpec((tm, d), lambda i: (i, 0)),
                   pl.BlockSpec((tm, LANES), lambda i: (i, 0)),
                   pl.BlockSpec((tm, LANES), lambda i: (i, 0))],
        out_shape=[jax.ShapeDtypeStruct((t, d), F32),
                   jax.ShapeDtypeStruct((t, d), F32),
                   jax.ShapeDtypeStruct((t, LANES), I32),
                   jax.ShapeDtypeStruct((t, LANES), F32)],
        compiler_params=_cp("parallel"),
        name="outproj_router",
    )(merged, x2, wo16, gate1[:, None, :], nw.reshape(1, d), scale[:, None, :], shift[:, None, :], wr16, br_p)


def _moe_rank_body(te_ref, pos_ref, cnt_ref, carry_ref, *, tt, topk):
    @pl.when(pl.program_id(0) == 0)
    def _():
        carry_ref[...] = jnp.zeros(carry_ref.shape, F32)

    te = te_ref[...]
    lane = lax.broadcasted_iota(I32, (tt, LANES), 1)
    onehot = jnp.zeros((tt, LANES), F32)
    for kk in range(topk):
        onehot = onehot + jnp.where(lane == te[:, kk:kk + 1], 1.0, 0.0)
    r = lax.broadcasted_iota(I32, (tt, tt), 0)
    c = lax.broadcasted_iota(I32, (tt, tt), 1)
    below = jnp.where(r > c, 1.0, 0.0).astype(BF16)
    rank = _dot(below, onehot.astype(BF16)) + carry_ref[0:1, :]
    pos = jnp.zeros((tt, LANES), I32)
    for kk in range(topk):
        p = jnp.sum(jnp.where(lane == te[:, kk:kk + 1], rank, 0.0), axis=-1, keepdims=True)
        pos = jnp.where(lane == kk, p.astype(I32), pos)
    pos_ref[...] = pos
    carry_ref[...] = carry_ref[...] + jnp.sum(onehot, axis=0, keepdims=True)
    cnt_ref[...] = carry_ref[...]


def _moe_rank(te):
    t = te.shape[0]
    tt = min(t, 512)
    return pl.pallas_call(
        functools.partial(_moe_rank_body, tt=tt, topk=TOP_K),
        grid=(t // tt,),
        in_specs=[pl.BlockSpec((tt, LANES), lambda i: (i, 0))],
        out_specs=[pl.BlockSpec((tt, LANES), lambda i: (i, 0)),
                   pl.BlockSpec((8, LANES), lambda i: (0, 0))],
        out_shape=[jax.ShapeDtypeStruct((t, LANES), I32),
                   jax.ShapeDtypeStruct((8, LANES), F32)],
        scratch_shapes=[pltpu.VMEM((8, LANES), F32)],
        compiler_params=_cp("arbitrary"),
        name="moe_rank",
    )(te)


def _moe_dispatch_body(dest_ref, fill_ref, h_ref, x_hbm, zero_ref, sem, fill_sem, *, tt, topk, bm, n_blocks):
    @pl.when(pl.program_id(0) == 0)
    def _():
        zero_ref[...] = jnp.zeros(zero_ref.shape, zero_ref.dtype)

        def fill_copy(blk):
            return pltpu.make_async_copy(zero_ref, x_hbm.at[pl.ds(pl.multiple_of(blk * bm, bm), bm)], fill_sem)

        def start(blk, _):
            @pl.when(fill_ref[blk] != 0)
            def _():
                fill_copy(blk).start()
            return 0

        def drain(blk, _):
            @pl.when(fill_ref[blk] != 0)
            def _():
                fill_copy(blk).wait()
            return 0

        lax.fori_loop(0, n_blocks, start, 0)
        lax.fori_loop(0, n_blocks, drain, 0)

    base = pl.program_id(0) * tt * topk

    def issue(t, _):
        for kk in range(topk):
            dst = x_hbm.at[pl.ds(dest_ref[base + t * topk + kk], 1)]
            pltpu.make_async_copy(h_ref.at[pl.ds(t, 1)], dst, sem).start()
        return 0

    lax.fori_loop(0, tt, issue, 0, unroll=4)
    for _ in range(topk):
        pltpu.make_async_copy(h_ref, x_hbm.at[pl.ds(0, tt)], sem).wait()


def _moe_dispatch(dest_flat, fill_flags, h2, n_rows):
    t, d = h2.shape
    tt = min(t, 128)
    bm = MOE_BM
    grid_spec = pltpu.PrefetchScalarGridSpec(
        num_scalar_prefetch=2,
        grid=(t // tt,),
        in_specs=[pl.BlockSpec((tt, d), lambda i, dest, fill: (i, 0))],
        out_specs=pl.BlockSpec(memory_space=pl.ANY),
        scratch_shapes=[pltpu.VMEM((bm, d), F32), pltpu.SemaphoreType.DMA(()), pltpu.SemaphoreType.DMA(())])
    return pl.pallas_call(
        functools.partial(_moe_dispatch_body, tt=tt, topk=TOP_K, bm=bm, n_blocks=n_rows // bm),
        grid_spec=grid_spec,
        out_shape=jax.ShapeDtypeStruct((n_rows, d), F32),
        compiler_params=_cp("arbitrary"),
        name="moe_dispatch",
    )(dest_flat, fill_flags, h2)


def _moe_gu_body(be_ref, na_ref, x_ref, wg_ref, wu_ref, bg_ref, bu_ref, act_ref, wg16_ref, wu16_ref):
    i = pl.program_id(1)
    prev = be_ref[jnp.maximum(i - 1, 0)]

    @pl.when((i == 0) | (be_ref[i] != prev))
    def _():
        wg16_ref[...] = wg_ref[...].astype(BF16)
        wu16_ref[...] = wu_ref[...].astype(BF16)

    @pl.when(i < na_ref[0])
    def _():
        x = x_ref[...].astype(BF16)
        g = jnp.minimum(_dot(x, wg16_ref[...]) + bg_ref[...], SWIGLU_LIMIT)
        u = jnp.clip(_dot(x, wu16_ref[...]) + bu_ref[...], -SWIGLU_LIMIT, SWIGLU_LIMIT)
        act_ref[...] = ((u + 1.0) * (g * _sigmoid(SWIGLU_ALPHA * g))).astype(act_ref.dtype)

    @pl.when(i >= na_ref[0])
    def _():
        act_ref[...] = jnp.zeros(act_ref.shape, act_ref.dtype)


def _moe_gu(block_e, n_act, x_rows, w_gu, b_gu):
    n_rows, d = x_rows.shape
    ne, _, ff2 = w_gu.shape
    ff = ff2 // 2
    bm = MOE_BM
    tn = _tile(ff, 1024)
    nt = ff // tn
    rowblk = lambda j, i, be, na: (jnp.minimum(i, na[0] - 1), 0)
    grid_spec = pltpu.PrefetchScalarGridSpec(
        num_scalar_prefetch=2,
        grid=(nt, n_rows // bm),
        in_specs=[pl.BlockSpec((bm, d), rowblk),
                  pl.BlockSpec((None, d, tn), lambda j, i, be, na: (be[i], 0, j)),
                  pl.BlockSpec((None, d, tn), lambda j, i, be, na: (be[i], 0, nt + j)),
                  pl.BlockSpec((None, 1, tn), lambda j, i, be, na: (be[i], 0, j)),
                  pl.BlockSpec((None, 1, tn), lambda j, i, be, na: (be[i], 0, nt + j))],
        out_specs=pl.BlockSpec((bm, tn), lambda j, i, be, na: (i, j)),
        scratch_shapes=[pltpu.VMEM((d, tn), BF16), pltpu.VMEM((d, tn), BF16)])
    return pl.pallas_call(
        _moe_gu_body,
        grid_spec=grid_spec,
        out_shape=jax.ShapeDtypeStruct((n_rows, ff), BF16),
        compiler_params=_cp("arbitrary", "arbitrary"),
        name="moe_gu",
    )(block_e, n_act, x_rows, w_gu, w_gu, b_gu.reshape(ne, 1, ff2), b_gu.reshape(ne, 1, ff2))


def _moe_down_body(be_ref, na_ref, a_ref, w_ref, b_ref, y_ref, w16_ref):
    i = pl.program_id(1)
    prev = be_ref[jnp.maximum(i - 1, 0)]

    @pl.when((i == 0) | (be_ref[i] != prev))
    def _():
        w16_ref[...] = w_ref[...].astype(BF16)

    @pl.when(i < na_ref[0])
    def _():
        y_ref[...] = _dot(a_ref[...], w16_ref[...]) + b_ref[...]

    @pl.when(i >= na_ref[0])
    def _():
        y_ref[...] = jnp.zeros(y_ref.shape, y_ref.dtype)


def _moe_down(block_e, n_act, act, w_down, b_down):
    n_rows, ff = act.shape
    ne, _, d = w_down.shape
    bm = MOE_BM
    tn = _tile(d, 2048)
    grid_spec = pltpu.PrefetchScalarGridSpec(
        num_scalar_prefetch=2,
        grid=(d // tn, n_rows // bm),
        in_specs=[pl.BlockSpec((bm, ff), lambda j, i, be, na: (jnp.minimum(i, na[0] - 1), 0)),
                  pl.BlockSpec((None, ff, tn), lambda j, i, be, na: (be[i], 0, j)),
                  pl.BlockSpec((None, 1, tn), lambda j, i, be, na: (be[i], 0, j))],
        out_specs=pl.BlockSpec((bm, tn), lambda j, i, be, na: (i, j)),
        scratch_shapes=[pltpu.VMEM((ff, tn), BF16)])
    return pl.pallas_call(
        _moe_down_body,
        grid_spec=grid_spec,
        out_shape=jax.ShapeDtypeStruct((n_rows, d), F32),
        compiler_params=_cp("arbitrary", "arbitrary"),
        name="moe_down",
    )(block_e, n_act, act, w_down, b_down.reshape(ne, 1, d))


def _moe_combine_body(dest_ref, x1_ref, tg_ref, g2_ref, fw_ref, y_hbm, o_ref, buf_ref, sem, *, tt, topk):
    i = pl.program_id(0)
    slot = i % 2

    def gather(step, sl):
        base = step * tt * topk

        def issue(t, _):
            for kk in range(topk):
                src = y_hbm.at[pl.ds(dest_ref[base + t * topk + kk], 1)]
                pltpu.make_async_copy(src, buf_ref.at[sl, kk, pl.ds(t, 1)], sem.at[sl]).start()
            return 0

        lax.fori_loop(0, tt, issue, 0, unroll=4)

    @pl.when(i == 0)
    def _():
        gather(0, 0)

    @pl.when(i + 1 < pl.num_programs(0))
    def _():
        gather(i + 1, 1 - slot)

    for kk in range(topk):
        pltpu.make_async_copy(y_hbm.at[pl.ds(0, tt)], buf_ref.at[slot, kk], sem.at[slot]).wait()
    tg = tg_ref[...]
    y = tg[:, 0:1] * buf_ref[slot, 0]
    for kk in range(1, topk):
        y = y + tg[:, kk:kk + 1] * buf_ref[slot, kk]
    o_ref[...] = _rms(x1_ref[...] + g2_ref[...] * y, fw_ref[...])


def _moe_combine(dest_flat, x1, tg, gate2, final_w, y_rows, seq):
    t, d = x1.shape
    tt = min(seq, 128)
    per_b = seq // tt
    grid_spec = pltpu.PrefetchScalarGridSpec(
        num_scalar_prefetch=1,
        grid=(t // tt,),
        in_specs=[pl.BlockSpec((tt, d), lambda i, dest: (i, 0)),
                  pl.BlockSpec((tt, LANES), lambda i, dest: (i, 0)),
                  pl.BlockSpec((None, 1, d), lambda i, dest: (i // per_b, 0, 0)),
                  pl.BlockSpec((1, d), lambda i, dest: (0, 0)),
                  pl.BlockSpec(memory_space=pl.ANY)],
        out_specs=pl.BlockSpec((tt, d), lambda i, dest: (i, 0)),
        scratch_shapes=[pltpu.VMEM((2, TOP_K, tt, d), F32), pltpu.SemaphoreType.DMA((2,))])
    return pl.pallas_call(
        functools.partial(_moe_combine_body, tt=tt, topk=TOP_K),
        grid_spec=grid_spec,
        out_shape=jax.ShapeDtypeStruct((t, d), F32),
        compiler_params=_cp("arbitrary"),
        name="moe_combine",
    )(dest_flat, x1, tg, gate2[:, None, :], final_w.reshape(1, d), y_rows)


def _pad_lanes(v, off=0, fill=0.0):
    out = jnp.full((1, LANES), fill, F32)
    return out.at[0, off:off + v.shape[0]].set(v.astype(F32))


def _layer(x2, c, bsz, seq, p, final_w):
    d = x2.shape[1]
    qr, nh, dh = p["w_uq"].shape
    kvr = p["w_uk"].shape[0]
    nhi, di = p["w_uqi"].shape[1:]
    hd = p["gdn_norm_w"].shape[0]
    nhv = p["a_log"].shape[0]
    v_w = nhv * hd
    qk_w = (p["conv_w"].shape[1] - v_w) // 2
    nqk = qk_w // hd
    ne = p["w_router"].shape[1]

    mod = _ada(c, p["ada_w"], p["ada_b"])
    shift1, scale1, gate1, shift2, scale2, gate2 = jnp.split(mod, 6, axis=-1)

    widths = (qr, kvr, di, nhi, qk_w, qk_w, v_w, nhv, nhv, v_w, d, d)
    offs = [0]
    for wd in widths:
        offs.append(offs[-1] + wd)
    col = lambda k: p["w_in"][:, offs[k]:offs[k + 1]]
    pad = jnp.zeros((d, LANES - 2 * nhv - nhi), F32)
    w_in = jnp.concatenate([col(4), col(5), col(6), col(9), col(10), col(11),
                            col(0), col(1), col(2), col(7), col(8), col(3), pad], axis=1).astype(BF16)
    lat_w = qr + kvr + di + LANES
    gq_col = 0
    z_col = gq_col + 2 * qk_w + v_w
    ga_col = z_col + v_w
    gb_col = ga_col + d
    lat_col = gb_col + d
    assert lat_col % lat_w == 0

    proj = _inproj(x2, p["norm1_w"], scale1, shift1, w_in, p["conv_w"], seq, gq_col, qk_w, hd)

    wuq_t = p["w_uq"].reshape(qr, nh * dh).T.astype(BF16)
    wuqi_t = p["w_uqi"].reshape(qr, nhi * di).T.astype(BF16)
    wuk_r = jnp.transpose(p["w_uk"], (1, 0, 2)).astype(BF16)
    wuv_t = jnp.transpose(p["w_uv"], (1, 2, 0)).astype(BF16)
    qabs_t, qi_t, ckv, ckvt, ki, wi_t = _dsa_prep(
        proj, bsz, seq, p["q_lat_norm_w"], p["kv_lat_norm_w"], p["idx_k_norm_w"], wuq_t, wuqi_t, wuk_r,
        (qr, kvr, di, nh, dh, nhi, 2 * nhv, lat_col // lat_w))
    o_a = _dsa_attn(ki, ckv, ckvt, qi_t, qabs_t, wi_t, wuv_t).reshape(bsz * seq, nh * dh)

    gc, beta, gct = _gdn_gates(proj, _pad_lanes(p["a_log"]), _pad_lanes(p["dt_bias"]), bsz, seq,
                               (lat_col + lat_w - LANES) // LANES, nhv)
    gct4 = gct.reshape(bsz, LANES, seq // GDN_CHUNK, GDN_CHUNK)
    o_b = _gdn_core(proj, gc, beta, gct4, p["gdn_norm_w"], bsz, seq, nqk, nhv, hd, gq_col, z_col)

    merged = _merge(o_a, o_b, p["w_branch_a"].astype(BF16), p["w_branch_b"].astype(BF16), proj, ga_col, gb_col)

    wr16 = jnp.zeros((d, LANES), F32).at[:, :ne].set(p["w_router"]).astype(BF16)
    br_p = _pad_lanes(p["b_router"], fill=-1e30)
    x1, h2, te, tg = _outproj_router(merged, x2, p["w_out"].astype(BF16), gate1, p["norm2_w"], scale2, shift2,
                                     wr16, br_p, seq)

    t = bsz * seq
    pos, cnt = _moe_rank(te)
    counts = cnt[0, :ne].astype(I32)
    padded = (counts + MOE_BM - 1) // MOE_BM * MOE_BM
    pad_end = jnp.cumsum(padded)
    pad_start = pad_end - padded
    dest = (pad_start[te[:, :TOP_K]] + pos[:, :TOP_K]).reshape(-1).astype(I32)
    n_blocks = -(-(t * TOP_K) // MOE_BM) + ne
    n_act = (pad_end[-1] // MOE_BM).astype(I32).reshape(1)
    blk = jnp.minimum(jnp.arange(n_blocks, dtype=I32), n_act[0] - 1) * MOE_BM
    block_e = jnp.minimum(jnp.sum(pad_end[None, :] <= blk[:, None], axis=1), ne - 1).astype(I32)
    blk_end = (jnp.arange(n_blocks, dtype=I32) + 1) * MOE_BM
    has_pad = jnp.any((blk_end[:, None] == pad_end[None, :]) & (padded[None, :] > 0), axis=1)
    fill_flags = (has_pad | (blk_end > pad_end[-1])).astype(I32)
    x_rows = _moe_dispatch(dest, fill_flags, h2, n_blocks * MOE_BM)
    act = _moe_gu(block_e, n_act, x_rows, p["w_gu"], p["b_gu"])
    y_rows = _moe_down(block_e, n_act, act, p["w_down"], p["b_down"])
    return _moe_combine(dest, x1, tg, gate2, final_w, y_rows, seq)


@jax.jit
def kernel(x, c, ada_w, ada_b, norm1_w, w_in, q_lat_norm_w, kv_lat_norm_w, idx_k_norm_w, w_uq, w_uqi, w_uk, w_uv, conv_w, a_log, dt_bias, gdn_norm_w, w_branch_a, w_branch_b, w_out, norm2_w, w_router, b_router, w_gu, b_gu, w_down, b_down, final_norm_w):
    bsz, seq, d = x.shape
    stacked = dict(ada_w=ada_w, ada_b=ada_b, norm1_w=norm1_w, w_in=w_in, q_lat_norm_w=q_lat_norm_w,
                   kv_lat_norm_w=kv_lat_norm_w, idx_k_norm_w=idx_k_norm_w, w_uq=w_uq, w_uqi=w_uqi, w_uk=w_uk,
                   w_uv=w_uv, conv_w=conv_w, a_log=a_log, dt_bias=dt_bias, gdn_norm_w=gdn_norm_w,
                   w_branch_a=w_branch_a, w_branch_b=w_branch_b, w_out=w_out, norm2_w=norm2_w,
                   w_router=w_router, b_router=b_router, w_gu=w_gu, b_gu=b_gu, w_down=w_down, b_down=b_down)
    depth = ada_w.shape[0]
    assert depth == 1, "the final norm is fused into the last layer's combine kernel"
    x2 = x.reshape(bsz * seq, d)
    p = {k: v[0] for k, v in stacked.items()}
    out = _layer(x2, c, bsz, seq, p, final_norm_w)
    return out.reshape(bsz, seq, d)
```
